```python
import jax
import jax.numpy as jnp
from jax import lax
import numpy as np

D_MODEL = 2048
BATCH = 32
SEQ = 256
DEPTH = 2
DEC_BATCH = 2
DEC_SEQ = 4096
PAST_LEN = 256

GRID_W = 64
MIX_W = D_MODEL
GROUP_W = MIX_W // 4
N_POOL_GROUPS = 4
POOL_WINDOWS = (2, 4, 8, 16)
POOL_GW = GROUP_W // N_POOL_GROUPS
HG_HEADS = 4
HG_DK = GROUP_W // HG_HEADS
HG_DV = HG_DK
HG_CHUNK = 16
ATT_HEADS = 8
ATT_KV_HEADS = 2
ATT_GROUP = ATT_HEADS // ATT_KV_HEADS
HEAD_DIM = GROUP_W // ATT_HEADS
WINDOW = 128
BLOCK = 128
ROPE_BASE = 10000.0
S5_CH = 16
S5_GROUPS = GROUP_W // S5_CH
S5_N = 64
FFN_DIM = 5632
CONV_W = 3
EPS = 1e-6
NEG_INF = -1e30

OFF_POOL = 0
OFF_HG_Q = OFF_POOL + GROUP_W
OFF_HG_FF = OFF_HG_Q + GROUP_W
OFF_HG_FB = OFF_HG_FF + GROUP_W
OFF_HG_I = OFF_HG_FB + GROUP_W
OFF_HG_G = OFF_HG_I + GROUP_W
OFF_ATT_Q = OFF_HG_G + GROUP_W
OFF_ATT_K = OFF_ATT_Q + ATT_HEADS * HEAD_DIM
OFF_ATT_V = OFF_ATT_K + ATT_KV_HEADS * HEAD_DIM
OFF_S5 = OFF_ATT_V + ATT_KV_HEADS * HEAD_DIM
IN_COLS = OFF_S5 + GROUP_W

kernel_name = "hybrid_pool_hgrn2_swa_s5_prefix_dit_step"


def rms_norm(x, g):
    xf = x.astype(jnp.float32)
    y = xf * lax.rsqrt(jnp.mean(xf * xf, axis=-1, keepdims=True) + EPS)
    return y * g.astype(jnp.float32)


def centred_mean(x, w):
    T = x.shape[1]
    cs = jnp.concatenate([jnp.zeros_like(x[:, :1]), jnp.cumsum(x, axis=1)], axis=1)
    t = jnp.arange(T)
    lo = jnp.clip(t - w // 2, 0, T)
    hi = jnp.clip(t + w // 2, 0, T)
    cnt = (hi - lo).astype(x.dtype)
    return (cs[:, hi] - cs[:, lo]) / cnt[None, :, None]


def pool_mixer(u, w_lin, scale):
    B, T, _ = u.shape
    ug = u.astype(jnp.float32).reshape(B, T, N_POOL_GROUPS, POOL_GW)
    pooled = jnp.stack([centred_mean(ug[:, :, gi], w) for gi, w in enumerate(POOL_WINDOWS)], axis=2) - ug
    y = jnp.einsum('btgc,gce->btge', pooled, w_lin.astype(jnp.float32))
    return y.reshape(B, T, GROUP_W) * scale.astype(jnp.float32)


def hgrn_chunk_scan(q, k, v, logf, s0):
    B, T, H, K = q.shape
    V = v.shape[-1]
    C = HG_CHUNK
    n = T // C
    rs = lambda a: a.reshape(B, n, C, H, a.shape[-1])
    qc, kc, vc, gc = rs(q), rs(k), rs(v), rs(logf)
    b = jnp.cumsum(gc, axis=2)
    mask = jnp.tril(jnp.ones((C, C), bool))[None, None, :, :, None, None]
    dec = jnp.exp(jnp.where(mask, b[:, :, :, None] - b[:, :, None, :], -jnp.inf))
    att = jnp.einsum('bnthk,bntshk,bnshk->bnths', qc, dec, kc)
    intra = jnp.einsum('bnths,bnshv->bnthv', att, vc)
    b_last = b[:, :, -1]
    k_dec = kc * jnp.exp(b_last[:, :, None] - b)
    chunk_kv = jnp.einsum('bnshk,bnshv->bnhkv', k_dec, vc)
    decay_last = jnp.exp(b_last)

    def step(S, inp):
        dl, kv = inp
        return dl[..., None] * S + kv, S

    s_final, s_in = lax.scan(step, s0.astype(jnp.float32),
                             (jnp.moveaxis(decay_last, 1, 0), jnp.moveaxis(chunk_kv, 1, 0)))
    s_in = jnp.moveaxis(s_in, 0, 1)
    inter = jnp.einsum('bnthk,bnhkv->bnthv', qc * jnp.exp(b), s_in)
    return (intra + inter).reshape(B, T, H, V), s_final


def hgrn_mixer(q, ff, fb, i, g, lb, s0, norm_g):
    B, T, _ = q.shape
    heads = lambda a: a.astype(jnp.float32).reshape(B, T, HG_HEADS, HG_DK)
    qh, vh = heads(q), heads(i)
    o = jnp.zeros((B, T, HG_HEADS, HG_DV), jnp.float32)
    finals = []
    for d, fx in enumerate((ff, fb)):
        l = lb[d]
        logf = heads(jnp.logaddexp(jnp.log(l), jnp.log1p(-l) + jax.nn.log_sigmoid(fx.astype(jnp.float32))))
        kh = -jnp.expm1(logf)
        if d == 0:
            od, sd = hgrn_chunk_scan(qh, kh, vh, logf, s0[:, 0])
        else:
            fl = lambda a: jnp.flip(a, axis=1)
            od, sd = hgrn_chunk_scan(fl(qh), fl(kh), fl(vh), fl(logf), s0[:, 1])
            od = fl(od)
        o = o + od
        finals.append(sd)
    o = rms_norm(o, norm_g).reshape(B, T, GROUP_W) * jax.nn.silu(g.astype(jnp.float32))
    return o, jnp.stack(finals, axis=1)


def axial_rope(x, rows, cols):
    half = HEAD_DIM // 2
    nf = half // 2
    inv = ROPE_BASE ** (-jnp.arange(nf, dtype=jnp.float32) / nf)

    def rot(xa, pos):
        ang = pos.astype(jnp.float32)[:, None] * inv[None]
        cos, sin = jnp.cos(ang)[None, :, None], jnp.sin(ang)[None, :, None]
        x1, x2 = xa[..., :nf], xa[..., nf:]
        return jnp.concatenate([x1 * cos - x2 * sin, x1 * sin + x2 * cos], axis=-1)

    return jnp.concatenate([rot(x[..., :half], rows), rot(x[..., half:], cols)], axis=-1)


def attend_context(q, k, v, sink):
    B, L = q.shape[:2]
    qg = q.reshape(B, L, ATT_KV_HEADS, ATT_GROUP, HEAD_DIM)
    s = jnp.einsum('blkgd,bmkd->bkglm', qg, k) * HEAD_DIM ** -0.5
    sk = jnp.broadcast_to(sink.astype(jnp.float32).reshape(1, ATT_KV_HEADS, ATT_GROUP, 1, 1), s.shape[:-1] + (1,))
    p = jax.nn.softmax(jnp.concatenate([s, sk], axis=-1), axis=-1)[..., :L]
    o = jnp.einsum('bkglm,bmkd->blkgd', p, v)
    return o.reshape(B, L, GROUP_W)


def attend_latent(q, k, v, kc, vc, sink):
    B, T = q.shape[:2]
    L = kc.shape[1]
    nb = T // BLOCK
    S = 3 * BLOCK
    scale = HEAD_DIM ** -0.5
    qb = q.reshape(B, nb, BLOCK, ATT_KV_HEADS, ATT_GROUP, HEAD_DIM)
    pad = ((0, 0), (BLOCK, BLOCK), (0, 0), (0, 0))
    idx = jnp.arange(nb)[:, None] * BLOCK + jnp.arange(S)[None, :]
    kb = jnp.pad(k, pad)[:, idx]
    vb = jnp.pad(v, pad)[:, idx]
    qpos = jnp.arange(T).reshape(nb, BLOCK)
    kpos = idx - BLOCK
    valid = ((kpos[:, None, :] >= 0) & (kpos[:, None, :] < T)
             & (jnp.abs(qpos[:, :, None] - kpos[:, None, :]) <= WINDOW))
    s_loc = jnp.einsum('bnqkgd,bnskd->bnkgqs', qb, kb) * scale
    s_loc = jnp.where(valid[None, :, None, None], s_loc, NEG_INF)
    s_ctx = jnp.einsum('bnqkgd,bmkd->bnkgqm', qb, kc.astype(jnp.float32)) * scale
    sk = jnp.broadcast_to(sink.astype(jnp.float32).reshape(1, 1, ATT_KV_HEADS, ATT_GROUP, 1, 1), s_loc.shape[:-1] + (1,))
    p = jax.nn.softmax(jnp.concatenate([s_loc, s_ctx, sk], axis=-1), axis=-1)
    o = (jnp.einsum('bnkgqs,bnskd->bnqkgd', p[..., :S], vb)
         + jnp.einsum('bnkgqm,bmkd->bnqkgd', p[..., S:S + L], vc.astype(jnp.float32)))
    return o.reshape(B, T, GROUP_W)


def _lin_combine(left, right):
    a1, b1 = left
    a2, b2 = right
    return a2 * a1, a2 * b1 + b2


def s5_discretise(a_re, a_im, log_dt, b_re, b_im):
    lam = lax.complex(a_re.astype(jnp.float32), a_im.astype(jnp.float32))
    dt = jnp.exp(log_dt.astype(jnp.float32))[:, None]
    lam_bar = jnp.exp(lam * dt)
    b = lax.complex(b_re.astype(jnp.float32), b_im.astype(jnp.float32))
    b_bar = ((lam_bar - 1.0) / lam)[..., None] * b
    return lam_bar, b_bar


def s5_scan(u, lam_bar, b_bar, h0):
    bu = jnp.einsum('gnc,btgc->btgn', b_bar, u.astype(jnp.complex64))
    bu = bu.at[:, 0].add(lam_bar * h0)
    a = jnp.broadcast_to(lam_bar, bu.shape)
    _, hs = lax.associative_scan(_lin_combine, (a, bu), axis=1)
    return hs


def s5_mixer(u, p, h0):
    B, T, _ = u.shape
    ug = u.astype(jnp.float32).reshape(B, T, S5_GROUPS, S5_CH)
    out = p['s5_d'].astype(jnp.float32).reshape(S5_GROUPS, S5_CH) * ug
    finals = []
    for d in range(2):
        lam_bar, b_bar = s5_discretise(p['s5_a_re'][d], p['s5_a_im'][d], p['s5_log_dt'][d],
                                       p['s5_b_re'][d], p['s5_b_im'][d])
        src = ug if d == 0 else jnp.flip(ug, axis=1)
        hs = s5_scan(src, lam_bar, b_bar, h0[:, d])
        finals.append(hs[:, -1])
        if d == 1:
            hs = jnp.flip(hs, axis=1)
        c_mat = lax.complex(p['s5_c_re'][d].astype(jnp.float32), p['s5_c_im'][d].astype(jnp.float32))
        out = out + jnp.real(jnp.einsum('gcn,btgn->btgc', c_mat, hs))
    y = jax.nn.gelu(out.reshape(B, T, GROUP_W))
    zz = jnp.einsum('btc,ce->bte', y, p['s5_w_glu'].astype(jnp.float32))
    return zz[..., :GROUP_W] * jax.nn.sigmoid(zz[..., GROUP_W:]), jnp.stack(finals, axis=1)


def mixers(h, p, lb, ctx, pos):
    B, T, _ = h.shape
    cols = jnp.einsum('btd,de->bte', h, p['w_in'].astype(jnp.float32))
    sl = lambda off, n: cols[..., off:off + n]
    y_pool = pool_mixer(sl(OFF_POOL, GROUP_W), p['pool_w'], p['pool_scale'])
    hg0 = jnp.zeros((B, 2, HG_HEADS, HG_DK, HG_DV), jnp.float32) if ctx is None else ctx['hgrn'].astype(jnp.float32)
    y_hg, s_hg = hgrn_mixer(sl(OFF_HG_Q, GROUP_W), sl(OFF_HG_FF, GROUP_W), sl(OFF_HG_FB, GROUP_W),
                            sl(OFF_HG_I, GROUP_W), sl(OFF_HG_G, GROUP_W), lb, hg0, p['hg_norm_g'])
    q = rms_norm(sl(OFF_ATT_Q, ATT_HEADS * HEAD_DIM).reshape(B, T, ATT_HEADS, HEAD_DIM), p['q_norm_g'])
    k = rms_norm(sl(OFF_ATT_K, ATT_KV_HEADS * HEAD_DIM).reshape(B, T, ATT_KV_HEADS, HEAD_DIM), p['k_norm_g'])
    v = sl(OFF_ATT_V, ATT_KV_HEADS * HEAD_DIM).reshape(B, T, ATT_KV_HEADS, HEAD_DIM)
    if ctx is None:
        y_att = attend_context(q, k, v, p['att_sink'])
        s5_0 = jnp.zeros((B, 2, S5_GROUPS, S5_N), jnp.complex64)
    else:
        rows, cols_ = pos
        y_att = attend_latent(axial_rope(q, rows, cols_), axial_rope(k, rows, cols_), v,
                              ctx['k'], ctx['v'], p['att_sink'])
        s5_0 = ctx['s5']
    y_s5, s_s5 = s5_mixer(sl(OFF_S5, GROUP_W), p, s5_0)
    y = jnp.einsum('btc,cd->btd', jnp.concatenate([y_pool, y_hg, y_att, y_s5], axis=-1),
                   p['w_out'].astype(jnp.float32))
    new_ctx = (k, v, s_hg, s_s5) if ctx is None else None
    return y, new_ctx


def conv_ffn(h, w_up, conv_w, conv_b, w_down):
    T = h.shape[1]
    u = jnp.einsum('btd,df->btf', h, w_up.astype(jnp.float32))
    a, b = u[..., :FFN_DIM], u[..., FFN_DIM:]
    half = CONV_W // 2
    ap = jnp.pad(a, ((0, 0), (half, half), (0, 0)))
    cw = conv_w.astype(jnp.float32)
    a = ap[:, 0:T] * cw[0] + ap[:, 1:T + 1] * cw[1] + ap[:, 2:T + 2] * cw[2] + conv_b.astype(jnp.float32)
    return jnp.einsum('btf,fd->btd', jax.nn.silu(a) * b, w_down.astype(jnp.float32))


def trunk_layer(x, mod, p, lb, ctx, pos):
    sh1, sc1, g1, sh2, sc2, g2 = [mod[:, j][:, None, :] for j in range(6)]
    h = rms_norm(x, p['norm1_g']) * (1.0 + sc1) + sh1
    y, new_ctx = mixers(h, p, lb, ctx, pos)
    x = x + (g1 * y).astype(x.dtype)
    h2 = rms_norm(x, p['norm2_g']) * (1.0 + sc2) + sh2
    x = x + (g2 * conv_ffn(h2, p['ffn_w_up'], p['ffn_conv_w'], p['ffn_conv_b'], p['ffn_w_down'])).astype(x.dtype)
    return x, new_ctx


def setup_inputs(seed: int = 0) -> dict:
    key = jax.random.key(seed)
    ks = iter(jax.random.split(key, 48))
    f32 = jnp.float32

    def nrm(shape, s=1.0):
        return jax.random.normal(next(ks), shape, f32) * s

    n_idx = jnp.arange(S5_N, dtype=f32)
    s5_shape = (DEPTH, 2, S5_GROUPS, S5_N)
    return {
        'x_prompt': nrm((BATCH, SEQ, D_MODEL)),
        'x_sample': nrm((DEC_BATCH, DEC_SEQ, D_MODEL)),
        'cache_k': nrm((DEC_BATCH, DEPTH, PAST_LEN, ATT_KV_HEADS, HEAD_DIM)),
        'cache_v': nrm((DEC_BATCH, DEPTH, PAST_LEN, ATT_KV_HEADS, HEAD_DIM)),
        'state_hgrn': nrm((DEC_BATCH, DEPTH, 2, HG_HEADS, HG_DK, HG_DV), 0.5),
        'state_s5_re': nrm((DEC_BATCH, DEPTH, 2, S5_GROUPS, S5_N), 0.5),
        'state_s5_im': nrm((DEC_BATCH, DEPTH, 2, S5_GROUPS, S5_N), 0.5),
        'c': nrm((DEC_BATCH, D_MODEL)),
        'c_ctx': nrm((D_MODEL,)),
        'norm1_g': 1.0 + nrm((DEPTH, D_MODEL), 0.02),
        'norm2_g': 1.0 + nrm((DEPTH, D_MODEL), 0.02),
        'ada_w': nrm((DEPTH, D_MODEL, 6 * D_MODEL), 0.5 * D_MODEL ** -0.5),
        'ada_b': nrm((DEPTH, 6 * D_MODEL), 0.02),
        'w_in': nrm((DEPTH, D_MODEL, IN_COLS), D_MODEL ** -0.5),
        'w_out': nrm((DEPTH, MIX_W, D_MODEL), MIX_W ** -0.5),
        'pool_w': nrm((DEPTH, N_POOL_GROUPS, POOL_GW, POOL_GW), POOL_GW ** -0.5),
        'pool_scale': 1.0 + nrm((DEPTH, GROUP_W), 0.02),
        'hg_lb_raw': nrm((DEPTH, 2, GROUP_W)),
        'hg_norm_g': 1.0 + nrm((DEPTH, HG_DV), 0.02),
        'q_norm_g': 1.0 + nrm((DEPTH, HEAD_DIM), 0.02),
        'k_norm_g': 1.0 + nrm((DEPTH, HEAD_DIM), 0.02),
        'att_sink': nrm((DEPTH, ATT_HEADS), 0.5),
        's5_a_re': -0.5 + nrm(s5_shape, 0.02),
        's5_a_im': jnp.pi * n_idx + nrm(s5_shape, 0.02),
        's5_log_dt': jax.random.uniform(next(ks), (DEPTH, 2, S5_GROUPS), f32, np.log(1e-3), np.log(1e-1)),
        's5_b_re': nrm((DEPTH, 2, S5_GROUPS, S5_N, S5_CH), (2 * S5_CH) ** -0.5),
        's5_b_im': nrm((DEPTH, 2, S5_GROUPS, S5_N, S5_CH), (2 * S5_CH) ** -0.5),
        's5_c_re': nrm((DEPTH, 2, S5_GROUPS, S5_CH, S5_N), (2 * S5_N) ** -0.5),
        's5_c_im': nrm((DEPTH, 2, S5_GROUPS, S5_CH, S5_N), (2 * S5_N) ** -0.5),
        's5_d': nrm((DEPTH, GROUP_W), 0.5),
        's5_w_glu': nrm((DEPTH, GROUP_W, 2 * GROUP_W), GROUP_W ** -0.5),
        'ffn_w_up': nrm((DEPTH, D_MODEL, 2 * FFN_DIM), D_MODEL ** -0.5),
        'ffn_conv_w': nrm((DEPTH, CONV_W, FFN_DIM), CONV_W ** -0.5),
        'ffn_conv_b': nrm((DEPTH, FFN_DIM), 0.01),
        'ffn_w_down': nrm((DEPTH, FFN_DIM, D_MODEL), FFN_DIM ** -0.5),
    }


def reference(x_prompt, x_sample, cache_k, cache_v, state_hgrn, state_s5_re, state_s5_im, c, c_ctx,
              norm1_g, norm2_g, ada_w, ada_b, w_in, w_out, pool_w, pool_scale, hg_lb_raw, hg_norm_g,
              q_norm_g, k_norm_g, att_sink, s5_a_re, s5_a_im, s5_log_dt, s5_b_re, s5_b_im, s5_c_re,
              s5_c_im, s5_d, s5_w_glu, ffn_w_up, ffn_conv_w, ffn_conv_b, ffn_w_down):
    f32 = jnp.float32
    T = x_sample.shape[1]
    n_rows = T // GRID_W
    t_idx = jnp.arange(n_rows * GRID_W)
    rows, cols = t_idx // GRID_W, t_idx % GRID_W
    lb_cum = jnp.cumsum(jax.nn.softmax(hg_lb_raw.astype(f32), axis=0), axis=0)
    hg_lb = lb_cum - lb_cum[:1]
    silu_ctx = jax.nn.silu(c_ctx.astype(f32))[None]
    silu_c = jax.nn.silu(c.astype(f32))
    y, z = x_prompt, x_sample
    ks_, vs_, hs_, s5r_, s5i_ = [], [], [], [], []
    for l in range(DEPTH):
        p = {
            'norm1_g': norm1_g[l], 'norm2_g': norm2_g[l], 'w_in': w_in[l], 'w_out': w_out[l],
            'pool_w': pool_w[l], 'pool_scale': pool_scale[l], 'hg_norm_g': hg_norm_g[l],
            'q_norm_g': q_norm_g[l], 'k_norm_g': k_norm_g[l], 'att_sink': att_sink[l],
            's5_a_re': s5_a_re[l], 's5_a_im': s5_a_im[l], 's5_log_dt': s5_log_dt[l],
            's5_b_re': s5_b_re[l], 's5_b_im': s5_b_im[l], 's5_c_re': s5_c_re[l], 's5_c_im': s5_c_im[l],
            's5_d': s5_d[l], 's5_w_glu': s5_w_glu[l], 'ffn_w_up': ffn_w_up[l],
            'ffn_conv_w': ffn_conv_w[l], 'ffn_conv_b': ffn_conv_b[l], 'ffn_w_down': ffn_w_down[l],
        }
        aw, ab = ada_w[l].astype(f32), ada_b[l].astype(f32)
        mod_ctx = (silu_ctx @ aw + ab).reshape(1, 6, D_MODEL)
        mod_lat = (silu_c @ aw + ab).reshape(-1, 6, D_MODEL)
        y, (k_l, v_l, hg_l, s5_l) = trunk_layer(y, mod_ctx, p, hg_lb[l], None, None)
        ks_.append(k_l)
        vs_.append(v_l)
        hs_.append(hg_l)
        s5r_.append(jnp.real(s5_l))
        s5i_.append(jnp.imag(s5_l))
        ctx_l = {
            'k': cache_k[:, l], 'v': cache_v[:, l], 'hgrn': state_hgrn[:, l],
            's5': lax.complex(state_s5_re[:, l].astype(f32), state_s5_im[:, l].astype(f32)),
        }
        z, _ = trunk_layer(z, mod_lat, p, hg_lb[l], ctx_l, (rows, cols))
    new_cache_k = jnp.stack(ks_, axis=1)
    new_cache_v = jnp.stack(vs_, axis=1)
    new_state_hgrn = jnp.stack(hs_, axis=1)
    new_state_s5_re = jnp.stack(s5r_, axis=1)
    new_state_s5_im = jnp.stack(s5i_, axis=1)
    return (y, z, new_cache_k, new_cache_v, new_state_hgrn, new_state_s5_re, new_state_s5_im)
```

```python
import functools

import jax
import jax.numpy as jnp
import numpy as np
from jax import lax
from jax.experimental import pallas as pl
from jax.experimental.pallas import tpu as pltpu

F32 = jnp.float32
BF16 = jnp.bfloat16
HIGHEST = lax.Precision.HIGHEST

D_MODEL = 2048
BATCH = 32
SEQ = 256
DEPTH = 2
DEC_BATCH = 2
DEC_SEQ = 4096
PAST_LEN = 256
GRID_W = 64
GROUP_W = 512
POOL_WINDOWS = (2, 4, 8, 16)
POOL_GW = 128
HG_HEADS = 4
HG_DK = 128
ATT_HEADS = 8
ATT_KV_HEADS = 2
ATT_GROUP = 4
HEAD_DIM = 64
WINDOW = 128
BLOCK = 128
ROPE_BASE = 10000.0
S5_CH = 16
S5_GROUPS = 32
S5_N = 64
FFN_DIM = 5632
EPS = 1e-6
NEG_INF = -1e30

N_CTX = BATCH * SEQ
N_LAT = DEC_BATCH * DEC_SEQ
N_TOK = N_CTX + N_LAT
IN_COLS = 4352

CB_POOL, CB_HQ, CB_FF, CB_FB, CB_HI, CB_HG, CB_AQ, CB_S5 = range(8)
CB_AK, CB_AV = 32, 33

S5_L = 16
HG_C = 128
VMEM_LIMIT = 56 * 1024 * 1024


def _cparams(sem):
    return pltpu.CompilerParams(dimension_semantics=sem, vmem_limit_bytes=VMEM_LIMIT)


def _dot(a, b):
    return jnp.dot(a, b, preferred_element_type=F32)


def _dot_nt(a, b):
    return lax.dot_general(a, b, (((1,), (1,)), ((), ())), preferred_element_type=F32)


def _dot_tn(a, b):
    return lax.dot_general(a, b, (((0,), (0,)), ((), ())), preferred_element_type=F32)


def _mod_index(row0):
    return jnp.where(row0 < N_CTX, 0, 1 + (row0 - N_CTX) // DEC_SEQ)


def _mod_kernel(cond_ref, w_ref, b_ref, o_ref):
    x = cond_ref[...]
    s = x * jax.nn.sigmoid(x)
    o_ref[...] = _dot(s.astype(BF16), w_ref[...].astype(BF16)) + b_ref[...]


def adaln_mod(cond, ada_w, ada_b, tn=1024):
    depth, d, n6 = ada_w.shape
    out = pl.pallas_call(
        _mod_kernel,
        grid=(depth, n6 // tn),
        in_specs=[pl.BlockSpec((8, d), lambda l, j: (0, 0)),
                  pl.BlockSpec((None, d, tn), lambda l, j: (l, 0, j)),
                  pl.BlockSpec((None, 1, tn), lambda l, j: (l, 0, j))],
        out_specs=pl.BlockSpec((None, 8, tn), lambda l, j: (l, 0, j)),
        out_shape=jax.ShapeDtypeStruct((depth, 8, n6), F32),
        compiler_params=_cparams(("parallel", "parallel")),
        name="adaln_mod",
    )(cond, ada_w, ada_b.reshape(depth, 1, n6))
    return out.reshape(depth, 8, 6, d)


def _inproj_kernel(x_ref, mod_ref, g_ref, w_ref, o_ref):
    x = x_ref[...]
    ms = jnp.mean(x * x, axis=-1, keepdims=True)
    h = x * lax.rsqrt(ms + EPS) * g_ref[...]
    h = h * (1.0 + mod_ref[1:2, :]) + mod_ref[0:1, :]
    o_ref[...] = _dot(h.astype(BF16), w_ref[...])


def in_proj(x, mod_l, g1, w_in, tm=512, tn=2176):
    n, d = x.shape
    ncols = w_in.shape[1]
    return pl.pallas_call(
        _inproj_kernel,
        grid=(ncols // tn, n // tm),
        in_specs=[pl.BlockSpec((tm, d), lambda j, i: (i, 0)),
                  pl.BlockSpec((None, 6, d), lambda j, i: (_mod_index(i * tm), 0, 0)),
                  pl.BlockSpec((1, d), lambda j, i: (0, 0)),
                  pl.BlockSpec((d, tn), lambda j, i: (0, j))],
        out_specs=pl.BlockSpec((tm, tn), lambda j, i: (i, j)),
        out_shape=jax.ShapeDtypeStruct((n, ncols), F32),
        compiler_params=_cparams(("parallel", "parallel")),
        name="in_proj",
    )(x, mod_l, g1.reshape(1, d), w_in)


def _pool_kernel(u_ref, up_ref, un_ref, w_ref, sc_ref, o_ref, ext_ref, *, tm):
    i = pl.program_id(0)
    row0 = i * tm
    tseq = jnp.where(row0 < N_CTX, SEQ, DEC_SEQ)
    has_prev = (row0 & (tseq - 1)) != 0
    has_next = ((row0 + tm) & (tseq - 1)) != 0
    ext_ref[0:8, :] = jnp.where(has_prev, up_ref[...], 0.0)
    ext_ref[8:8 + tm, :] = u_ref[...]
    ext_ref[8 + tm:16 + tm, :] = jnp.where(has_next, un_ref[...], 0.0)
    pos = (row0 + lax.broadcasted_iota(jnp.int32, (tm, 1), 0)) & (tseq - 1)
    outs = []
    for gi, w in enumerate(POOL_WINDOWS):
        ls = slice(gi * POOL_GW, (gi + 1) * POOL_GW)
        acc = ext_ref[8 - w // 2:8 - w // 2 + tm, ls]
        for o in range(-w // 2 + 1, w // 2):
            acc = acc + ext_ref[8 + o:8 + o + tm, ls]
        lo = jnp.maximum(pos - w // 2, 0)
        hi = jnp.minimum(pos + w // 2, tseq)
        cnt = (hi - lo).astype(F32)
        pooled = acc / cnt - ext_ref[8:8 + tm, ls]
        outs.append(_dot(pooled.astype(BF16), w_ref[gi]))
    o_ref[...] = (jnp.concatenate(outs, axis=1) * sc_ref[...]).astype(o_ref.dtype)


def pool_mixer(cols, pool_w, pool_scale, tm=256):
    n = cols.shape[0]
    nb8 = n // 8
    return pl.pallas_call(
        functools.partial(_pool_kernel, tm=tm),
        grid=(n // tm,),
        in_specs=[pl.BlockSpec((tm, GROUP_W), lambda i: (i, CB_POOL)),
                  pl.BlockSpec((8, GROUP_W), lambda i: (jnp.maximum(i * (tm // 8) - 1, 0), CB_POOL)),
                  pl.BlockSpec((8, GROUP_W), lambda i: (jnp.minimum((i + 1) * (tm // 8), nb8 - 1), CB_POOL)),
                  pl.BlockSpec((4, POOL_GW, POOL_GW), lambda i: (0, 0, 0)),
                  pl.BlockSpec((1, GROUP_W), lambda i: (0, 0))],
        out_specs=pl.BlockSpec((tm, GROUP_W), lambda i: (i, 0)),
        out_shape=jax.ShapeDtypeStruct((n, GROUP_W), BF16),
        scratch_shapes=[pltpu.VMEM((tm + 16, GROUP_W), F32)],
        compiler_params=_cparams(("parallel",)),
        name="pool_mixer",
    )(cols, cols, cols, pool_w, pool_scale.reshape(1, GROUP_W))


def _block_ref_rows(x, c, m, rev):
    w = x.shape[1]
    if 2 * m >= 8:
        r = m if rev else m - 1
        xr = x.reshape(c // (2 * m), 2 * m, w)[:, r:r + 1, :]
        return jnp.broadcast_to(xr, (c // (2 * m), 2 * m, w)).reshape(c, w)
    x8 = x.reshape(c // 8, 8, w)
    sub = lax.broadcasted_iota(jnp.int32, (c // 8, 8, w), 1)
    out = None
    for kb in range(8 // (2 * m)):
        r = kb * 2 * m + (m if rev else m - 1)
        cand = jnp.broadcast_to(x8[:, r:r + 1, :], (c // 8, 8, w))
        out = cand if out is None else jnp.where(sub >= kb * 2 * m, cand, out)
    return out.reshape(c, w)


def _hgrn_chunk(q, fx, v, log_l, log1m_l, st_ref, rev, c):
    ls = jnp.minimum(fx, 0.0) - jnp.log1p(jnp.exp(-jnp.abs(fx)))
    bb = log1m_l + ls
    logf = jnp.maximum(log_l, bb) + jnp.log1p(jnp.exp(-jnp.abs(log_l - bb)))
    kk = 1.0 - jnp.exp(logf)
    row = lax.broadcasted_iota(jnp.int32, (c, c), 0)
    col = lax.broadcasted_iota(jnp.int32, (c, c), 1)
    causal = (row <= col) if rev else (row >= col)
    b = jnp.dot(causal.astype(F32), logf, precision=HIGHEST, preferred_element_type=F32)
    b_last = b[0:1] if rev else b[c - 1:c]
    qt = (q * jnp.exp(b)).astype(BF16)
    kd = (kk * jnp.exp(b_last - b)).astype(BF16)
    dl = jnp.exp(b_last)
    xor = row ^ col
    levels = []
    m = c // 2
    while m >= 1:
        wgt = jnp.exp(-jnp.abs(b - _block_ref_rows(b, c, m, rev)))
        levels.append(((q * wgt).astype(BF16), (kk * wgt).astype(BF16), (xor >> (m.bit_length() - 1)) == 1))
        m //= 2
    qb = q.astype(BF16)
    kb = kk.astype(BF16)
    vb = v.astype(BF16)
    outs = []
    for h in range(HG_HEADS):
        sl = slice(h * HG_DK, (h + 1) * HG_DK)
        att = jnp.where(xor == 0, _dot_nt(qb[:, sl], kb[:, sl]), 0.0)
        for ql, kl, msk in levels:
            att = att + jnp.where(msk, _dot_nt(ql[:, sl], kl[:, sl]), 0.0)
        att = jnp.where(causal, att, 0.0)
        intra = _dot(att.astype(BF16), vb[:, sl])
        st = st_ref[h]
        inter = _dot_nt(qt[:, sl], st.astype(BF16))
        st_ref[h] = st * dl[:, sl] + _dot_tn(vb[:, sl], kd[:, sl])
        outs.append(intra + inter)
    return jnp.concatenate(outs, axis=1)


def _hgrn_kernel(qf_ref, ff_ref, vf_ref, qb_ref, fb_ref, vb_ref, lp_ref, s0_ref,
                 of_ref, ob_ref, sfin_ref, st_ref, *, c, nchunks):
    ci = pl.program_id(1)

    @pl.when(ci == 0)
    def _():
        st_ref[...] = s0_ref[...]

    of_ref[...] = _hgrn_chunk(qf_ref[...], ff_ref[...], vf_ref[...], lp_ref[0, 0:1, :], lp_ref[0, 1:2, :],
                              st_ref.at[0], False, c)
    ob_ref[...] = _hgrn_chunk(qb_ref[...], fb_ref[...], vb_ref[...], lp_ref[1, 0:1, :], lp_ref[1, 1:2, :],
                              st_ref.at[1], True, c)

    @pl.when(ci == nchunks - 1)
    def _():
        for d in range(2):
            for h in range(HG_HEADS):
                sfin_ref[d, h] = st_ref[d, h].T


def hgrn_scan(cols, lp, s0_t, row_off, nseq, t, c=HG_C):
    nchunks = t // c
    base = row_off // c

    def fwd(cb):
        return pl.BlockSpec((c, GROUP_W), lambda s, i: (base + s * nchunks + i, cb))

    def bwd(cb):
        return pl.BlockSpec((c, GROUP_W), lambda s, i: (base + s * nchunks + nchunks - 1 - i, cb))

    st_spec = pl.BlockSpec((None, 2, HG_HEADS, HG_DK, HG_DK), lambda s, i: (s, 0, 0, 0, 0))
    return pl.pallas_call(
        functools.partial(_hgrn_kernel, c=c, nchunks=nchunks),
        grid=(nseq, nchunks),
        in_specs=[fwd(CB_HQ), fwd(CB_FF), fwd(CB_HI), bwd(CB_HQ), bwd(CB_FB), bwd(CB_HI),
                  pl.BlockSpec((2, 2, GROUP_W), lambda s, i: (0, 0, 0)), st_spec],
        out_specs=[pl.BlockSpec((c, GROUP_W), lambda s, i: (s * nchunks + i, 0)),
                   pl.BlockSpec((c, GROUP_W), lambda s, i: (s * nchunks + nchunks - 1 - i, 0)),
                   st_spec],
        out_shape=[jax.ShapeDtypeStruct((nseq * t, GROUP_W), F32),
                   jax.ShapeDtypeStruct((nseq * t, GROUP_W), F32),
                   jax.ShapeDtypeStruct((nseq, 2, HG_HEADS, HG_DK, HG_DK), F32)],
        scratch_shapes=[pltpu.VMEM((2, HG_HEADS, HG_DK, HG_DK), F32)],
        compiler_params=_cparams(("parallel", "arbitrary")),
        name="hgrn_scan",
    )(cols, cols, cols, cols, cols, cols, lp, s0_t)


def _hgrn_finish_kernel(of_ref, ob_ref, g_ref, ng_ref, o_ref):
    o = of_ref[...] + ob_ref[...]
    g = g_ref[...]
    gate = g * jax.nn.sigmoid(g)
    outs = []
    for h in range(HG_HEADS):
        sl = slice(h * HG_DK, (h + 1) * HG_DK)
        oh = o[:, sl]
        ms = jnp.mean(oh * oh, axis=-1, keepdims=True)
        outs.append(oh * lax.rsqrt(ms + EPS) * ng_ref[...])
    o_ref[...] = (jnp.concatenate(outs, axis=1) * gate).astype(o_ref.dtype)


def hgrn_finish(o_f, o_b, cols, norm_g, tm=1024):
    n = o_f.shape[0]
    return pl.pallas_call(
        _hgrn_finish_kernel,
        grid=(n // tm,),
        in_specs=[pl.BlockSpec((tm, GROUP_W), lambda i: (i, 0)),
                  pl.BlockSpec((tm, GROUP_W), lambda i: (i, 0)),
                  pl.BlockSpec((tm, GROUP_W), lambda i: (i, CB_HG)),
                  pl.BlockSpec((1, HG_DK), lambda i: (0, 0))],
        out_specs=pl.BlockSpec((tm, GROUP_W), lambda i: (i, 0)),
        out_shape=jax.ShapeDtypeStruct((n, GROUP_W), BF16),
        compiler_params=_cparams(("parallel",)),
        name="hgrn_finish",
    )(o_f, o_b, cols, norm_g.reshape(1, HG_DK))


def _swap16(x):
    w = x.shape[1]
    lane = lax.broadcasted_iota(jnp.int32, x.shape, 1)
    return jnp.where((lane & 31) < 16, pltpu.roll(x, w - 16, 1), pltpu.roll(x, 16, 1))


def _qkprep_kernel(q_ref, k_ref, cos_ref, sin_ref, gq_ref, gk_ref, bq_ref, qo_ref, ko_ref):
    cos = cos_ref[...]
    sin = sin_ref[...]
    q = q_ref[...]
    msq = jnp.dot(q * q, bq_ref[...], precision=HIGHEST, preferred_element_type=F32) * (1.0 / HEAD_DIM)
    qn = q * lax.rsqrt(msq + EPS) * gq_ref[...]
    cos4 = jnp.concatenate([cos] * 4, axis=1)
    sin4 = jnp.concatenate([sin] * 4, axis=1)
    qr = qn * cos4 + _swap16(qn) * sin4
    qo_ref[...] = (qr * (HEAD_DIM ** -0.5)).astype(qo_ref.dtype)
    k = k_ref[...]
    msk = jnp.dot(k * k, bq_ref[0:128, 0:128], precision=HIGHEST, preferred_element_type=F32) * (1.0 / HEAD_DIM)
    kn = k * lax.rsqrt(msk + EPS) * gk_ref[...]
    ko_ref[...] = kn * cos + _swap16(kn) * sin


def qk_prep(cols, cos_t, sin_t, q_norm_g, k_norm_g, tm=512):
    n = cols.shape[0]
    head = np.arange(GROUP_W) // HEAD_DIM
    bones = jnp.asarray((head[:, None] == head[None, :]).astype(np.float32))
    return pl.pallas_call(
        _qkprep_kernel,
        grid=(n // tm,),
        in_specs=[pl.BlockSpec((tm, GROUP_W), lambda i: (i, CB_AQ)),
                  pl.BlockSpec((tm, 128), lambda i: (i, CB_AK)),
                  pl.BlockSpec((tm, 128), lambda i: (i, 0)),
                  pl.BlockSpec((tm, 128), lambda i: (i, 0)),
                  pl.BlockSpec((1, GROUP_W), lambda i: (0, 0)),
                  pl.BlockSpec((1, 128), lambda i: (0, 0)),
                  pl.BlockSpec((GROUP_W, GROUP_W), lambda i: (0, 0))],
        out_specs=[pl.BlockSpec((tm, GROUP_W), lambda i: (i, 0)),
                   pl.BlockSpec((tm, 128), lambda i: (i, 0))],
        out_shape=[jax.ShapeDtypeStruct((n, GROUP_W), BF16),
                   jax.ShapeDtypeStruct((n, 128), F32)],
        compiler_params=_cparams(("parallel",)),
        name="qk_prep",
    )(cols, cols, cos_t, sin_t, jnp.tile(q_norm_g, ATT_HEADS).reshape(1, GROUP_W),
      jnp.tile(k_norm_g, ATT_KV_HEADS).reshape(1, 128), bones)


def _stack_heads(q, kh, rows):
    return jnp.concatenate(
        [q[:, (kh * ATT_GROUP + g) * HEAD_DIM:(kh * ATT_GROUP + g + 1) * HEAD_DIM] for g in range(ATT_GROUP)], axis=0)


def _sink_col(sink_ref, kh, rows):
    return jnp.concatenate(
        [jnp.full((rows, 1), sink_ref[kh * ATT_GROUP + g], F32) for g in range(ATT_GROUP)], axis=0)


def _attn_ctx_kernel(sink_ref, q_ref, k_ref, v_ref, o_ref, *, t):
    q = q_ref[...]
    k = k_ref[...].astype(BF16)
    v = v_ref[...].astype(BF16)
    pieces = [None] * ATT_HEADS
    for kh in range(ATT_KV_HEADS):
        hs = slice(kh * HEAD_DIM, (kh + 1) * HEAD_DIM)
        s = _dot_nt(_stack_heads(q, kh, t), k[:, hs])
        sk = _sink_col(sink_ref, kh, t)
        mx = jnp.maximum(jnp.max(s, axis=1, keepdims=True), sk)
        p = jnp.exp(s - mx)
        den = jnp.sum(p, axis=1, keepdims=True) + jnp.exp(sk - mx)
        o = _dot(p.astype(BF16), v[:, hs]) / den
        for g in range(ATT_GROUP):
            pieces[kh * ATT_GROUP + g] = o[g * t:(g + 1) * t]
    o_ref[...] = jnp.concatenate(pieces, axis=1).astype(o_ref.dtype)


def attend_ctx(qs, kn, cols, sink, nseq, t):
    return pl.pallas_call(
        functools.partial(_attn_ctx_kernel, t=t),
        grid=(nseq,),
        in_specs=[pl.BlockSpec(memory_space=pltpu.SMEM),
                  pl.BlockSpec((t, GROUP_W), lambda s: (s, 0)),
                  pl.BlockSpec((t, 128), lambda s: (s, 0)),
                  pl.BlockSpec((t, 128), lambda s: (s, CB_AV))],
        out_specs=pl.BlockSpec((t, GROUP_W), lambda s: (s, 0)),
        out_shape=jax.ShapeDtypeStruct((nseq * t, GROUP_W), BF16),
        compiler_params=_cparams(("parallel",)),
        name="attend_ctx",
    )(sink, qs, kn, cols)


def _attn_lat_kernel(sink_ref, q_ref, kp_ref, kc_ref, kn_ref, vp_ref, vc_ref, vn_ref, ck_ref, cv_ref, o_ref,
                     *, t, blk):
    i = pl.program_id(1)
    q = q_ref[...]
    kl = jnp.concatenate([kp_ref[...], kc_ref[...], kn_ref[...]], axis=0).astype(BF16)
    vl = jnp.concatenate([vp_ref[...], vc_ref[...], vn_ref[...]], axis=0).astype(BF16)
    ck = ck_ref[...].astype(BF16)
    cv = cv_ref[...].astype(BF16)
    rows = ATT_GROUP * blk
    qpos = i * blk + (lax.broadcasted_iota(jnp.int32, (rows, 3 * blk), 0) & (blk - 1))
    kpos = (i - 1) * blk + lax.broadcasted_iota(jnp.int32, (rows, 3 * blk), 1)
    valid = (kpos >= 0) & (kpos < t) & (jnp.abs(qpos - kpos) <= WINDOW)
    pieces = [None] * ATT_HEADS
    for kh in range(ATT_KV_HEADS):
        hs = slice(kh * HEAD_DIM, (kh + 1) * HEAD_DIM)
        qh = _stack_heads(q, kh, blk)
        s_loc = jnp.where(valid, _dot_nt(qh, kl[:, hs]), NEG_INF)
        s_ctx = _dot_nt(qh, ck[:, hs])
        sk = _sink_col(sink_ref, kh, blk)
        mx = jnp.maximum(jnp.maximum(jnp.max(s_loc, axis=1, keepdims=True),
                                     jnp.max(s_ctx, axis=1, keepdims=True)), sk)
        p_loc = jnp.exp(s_loc - mx)
        p_ctx = jnp.exp(s_ctx - mx)
        den = (jnp.sum(p_loc, axis=1, keepdims=True) + jnp.sum(p_ctx, axis=1, keepdims=True)
               + jnp.exp(sk - mx))
        o = (_dot(p_loc.astype(BF16), vl[:, hs]) + _dot(p_ctx.astype(BF16), cv[:, hs])) / den
        for g in range(ATT_GROUP):
            pieces[kh * ATT_GROUP + g] = o[g * blk:(g + 1) * blk]
    o_ref[...] = jnp.concatenate(pieces, axis=1).astype(o_ref.dtype)


def attend_lat(qs, kn, cols, cache_k, cache_v, sink, row_off, nseq, t, blk=BLOCK):
    nb = t // blk
    base = row_off // blk
    lctx = cache_k.shape[1]

    def nbr(cb, off):
        return pl.BlockSpec((blk, 128), lambda b, i: (base + b * nb + jnp.clip(i + off, 0, nb - 1), cb))

    return pl.pallas_call(
        functools.partial(_attn_lat_kernel, t=t, blk=blk),
        grid=(nseq, nb),
        in_specs=[pl.BlockSpec(memory_space=pltpu.SMEM),
                  pl.BlockSpec((blk, GROUP_W), lambda b, i: (base + b * nb + i, 0)),
                  nbr(0, -1), nbr(0, 0), nbr(0, 1),
                  nbr(CB_AV, -1), nbr(CB_AV, 0), nbr(CB_AV, 1),
                  pl.BlockSpec((None, lctx, 128), lambda b, i: (b, 0, 0)),
                  pl.BlockSpec((None, lctx, 128), lambda b, i: (b, 0, 0))],
        out_specs=pl.BlockSpec((blk, GROUP_W), lambda b, i: (b * nb + i, 0)),
        out_shape=jax.ShapeDtypeStruct((nseq * t, GROUP_W), BF16),
        compiler_params=_cparams(("parallel", "parallel")),
        name="attend_lat",
    )(sink, qs, kn, kn, kn, cols, cols, cols, cache_k, cache_v)


def s5_operators(a_re, a_im, log_dt, b_re, b_im, c_re, c_im):
    L = S5_L
    dt = jnp.exp(log_dt.astype(F32))[..., None]
    are, aim = a_re.astype(F32), a_im.astype(F32)

    def lam_pow(p):
        mag = jnp.exp(are * dt * p)
        return mag * jnp.cos(aim * dt * p), mag * jnp.sin(aim * dt * p)

    lre, lim = lam_pow(1.0)
    den = are * are + aim * aim
    fre = ((lre - 1.0) * are + lim * aim) / den
    fim = (lim * are - (lre - 1.0) * aim) / den
    bre = fre[..., None] * b_re - fim[..., None] * b_im
    bim = fre[..., None] * b_im + fim[..., None] * b_re
    pw = jnp.arange(L + 1, dtype=F32)
    pre, pim = lam_pow(pw[:, None, None, None])
    pre = jnp.moveaxis(pre, 0, 2)
    pim = jnp.moveaxis(pim, 0, 2)
    cre, cim = c_re.astype(F32), c_im.astype(F32)
    ein = functools.partial(jnp.einsum, precision=HIGHEST)
    cpr = cre[:, :, None] * pre[:, :, :, None] - cim[:, :, None] * pim[:, :, :, None]
    cpi = cre[:, :, None] * pim[:, :, :, None] + cim[:, :, None] * pre[:, :, :, None]
    ktau = ein('dgtcn,dgne->dgtce', cpr, bre) - ein('dgtcn,dgne->dgtce', cpi, bim)
    s_idx = jnp.arange(L)[:, None]
    t_idx = jnp.arange(L)[None, :]
    kms = []
    for d in range(2):
        tau = (t_idx - s_idx) if d == 0 else (s_idx - t_idx)
        ok = tau >= 0
        kt = ktau[d][:, jnp.clip(tau, 0, L)]
        kt = jnp.where(ok[None, :, :, None, None], kt, 0.0)
        kms.append(jnp.transpose(kt, (0, 1, 4, 2, 3)).reshape(-1, L * S5_CH, L * S5_CH))
    kmat = jnp.stack(kms)
    wsts, wouts = [], []
    for d in range(2):
        sp = (L - 1 - jnp.arange(L)) if d == 0 else jnp.arange(L)
        pr, pi = pre[d][:, sp], pim[d][:, sp]
        wr = pr[:, :, None, :] * jnp.swapaxes(bre[d], 1, 2)[:, None] - pi[:, :, None, :] * jnp.swapaxes(bim[d], 1, 2)[:, None]
        wi = pr[:, :, None, :] * jnp.swapaxes(bim[d], 1, 2)[:, None] + pi[:, :, None, :] * jnp.swapaxes(bre[d], 1, 2)[:, None]
        wsts.append(jnp.concatenate([wr, wi], axis=-1).reshape(-1, L * S5_CH, 2 * S5_N))
        tp = (jnp.arange(L) + 1) if d == 0 else (L - jnp.arange(L))
        orr = cpr[d][:, tp]
        oii = cpi[d][:, tp]
        wo = jnp.concatenate([jnp.transpose(orr, (0, 3, 1, 2)), -jnp.transpose(oii, (0, 3, 1, 2))], axis=1)
        wouts.append(wo.reshape(-1, 2 * S5_N, L * S5_CH))
    wst = jnp.stack(wsts)
    wout = jnp.stack(wouts)
    jre, jim = lam_pow((L * 2.0 ** jnp.arange(9, dtype=F32))[:, None, None, None])
    a1 = jnp.concatenate([jre, jre], axis=-1)
    a2 = jnp.concatenate([-jim, jim], axis=-1)
    lam = jnp.transpose(jnp.stack([a1, a2], axis=1), (2, 3, 0, 1, 4))
    return kmat.astype(BF16), wst.astype(BF16), wout.astype(BF16), lam


def _cmul(a_ref, d, j, x):
    return a_ref[d, j, 0:1, :] * x + a_ref[d, j, 1:2, :] * pltpu.roll(x, S5_N, 1)


def _s5_kernel(*refs, seglen, nseq, has_h0):
    if has_h0:
        u_ref, km_ref, ws_ref, wo_ref, lam_ref, h0_ref, y_ref, hfin_ref = refs
    else:
        u_ref, km_ref, ws_ref, wo_ref, lam_ref, y_ref, hfin_ref = refs
    r = nseq * seglen
    u = u_ref[...].astype(BF16)
    rowi = lax.broadcasted_iota(jnp.int32, (r, 1), 0)
    cl = rowi & (seglen - 1)
    y = None
    for d in range(2):
        edge = 0 if d == 0 else seglen - 1
        h0rows = jnp.zeros((r, 2 * S5_N), F32)
        x = _dot(u, ws_ref[d])
        if has_h0:
            for s in range(nseq):
                h0rows = jnp.where(rowi == s * seglen + edge, h0_ref[d, s:s + 1, :], h0rows)
            x = x + _cmul(lam_ref, d, 0, h0rows)
        sh, j = 1, 0
        while sh < seglen:
            if d == 0:
                xs, ok = pltpu.roll(x, sh, 0), cl >= sh
            else:
                xs, ok = pltpu.roll(x, r - sh, 0), cl < seglen - sh
            x = x + jnp.where(ok, _cmul(lam_ref, d, j, xs), 0.0)
            sh, j = sh * 2, j + 1
        hfin_ref[d] = x
        if d == 0:
            hin = jnp.where(cl >= 1, pltpu.roll(x, 1, 0), h0rows)
        else:
            hin = jnp.where(cl < seglen - 1, pltpu.roll(x, r - 1, 0), h0rows)
        yd = _dot(u, km_ref[d]) + _dot(hin.astype(BF16), wo_ref[d])
        y = yd if y is None else y + yd
    y_ref[...] = y


def s5_scan(ug, ops, h0, row_off, nseq, seglen):
    kmat, wst, wout, lam = ops
    g = ug.shape[0]
    r = nseq * seglen
    blk = row_off // r
    assert row_off % r == 0
    lw = S5_L * S5_CH
    in_specs = [pl.BlockSpec((None, r, lw), lambda gi: (gi, blk, 0)),
                pl.BlockSpec((2, None, lw, lw), lambda gi: (0, gi, 0, 0)),
                pl.BlockSpec((2, None, lw, 2 * S5_N), lambda gi: (0, gi, 0, 0)),
                pl.BlockSpec((2, None, 2 * S5_N, lw), lambda gi: (0, gi, 0, 0)),
                pl.BlockSpec((2, None, 9, 2, 2 * S5_N), lambda gi: (0, gi, 0, 0, 0))]
    args = [ug, kmat, wst, wout, lam]
    if h0 is not None:
        in_specs.append(pl.BlockSpec((None, 2, h0.shape[2], 2 * S5_N), lambda gi: (gi, 0, 0, 0)))
        args.append(h0)
    return pl.pallas_call(
        functools.partial(_s5_kernel, seglen=seglen, nseq=nseq, has_h0=h0 is not None),
        grid=(g,),
        in_specs=in_specs,
        out_specs=[pl.BlockSpec((None, r, lw), lambda gi: (gi, 0, 0)),
                   pl.BlockSpec((None, 2, r, 2 * S5_N), lambda gi: (gi, 0, 0, 0))],
        out_shape=[jax.ShapeDtypeStruct((g, r, lw), F32),
                   jax.ShapeDtypeStruct((g, 2, r, 2 * S5_N), F32)],
        compiler_params=_cparams(("parallel",)),
        name="s5_scan",
    )(*args)


def _s5_post_kernel(u_ref, y_ref, d_ref, w_ref, o_ref):
    out = d_ref[...] * u_ref[...] + y_ref[...]
    yy = jax.nn.gelu(out)
    zz = _dot(yy.astype(BF16), w_ref[...])
    o_ref[...] = (zz[:, :GROUP_W] * jax.nn.sigmoid(zz[:, GROUP_W:])).astype(o_ref.dtype)


def s5_post(cols, y, s5_d, w_glu, tm=1024):
    n = cols.shape[0]
    return pl.pallas_call(
        _s5_post_kernel,
        grid=(n // tm,),
        in_specs=[pl.BlockSpec((tm, GROUP_W), lambda i: (i, CB_S5)),
                  pl.BlockSpec((tm, GROUP_W), lambda i: (i, 0)),
                  pl.BlockSpec((1, GROUP_W), lambda i: (0, 0)),
                  pl.BlockSpec((GROUP_W, 2 * GROUP_W), lambda i: (0, 0))],
        out_specs=pl.BlockSpec((tm, GROUP_W), lambda i: (i, 0)),
        out_shape=jax.ShapeDtypeStruct((n, GROUP_W), BF16),
        compiler_params=_cparams(("parallel",)),
        name="s5_post",
    )(cols, y, s5_d.reshape(1, GROUP_W), w_glu)


def _outproj_kernel(p0_ref, p1_ref, p2_ref, p3_ref, w_ref, x_ref, mod_ref, g_ref, x1_ref, h2_ref):
    y = _dot(p0_ref[...], w_ref[0:GROUP_W, :])
    y = y + _dot(p1_ref[...], w_ref[GROUP_W:2 * GROUP_W, :])
    y = y + _dot(p2_ref[...], w_ref[2 * GROUP_W:3 * GROUP_W, :])
    y = y + _dot(p3_ref[...], w_ref[3 * GROUP_W:4 * GROUP_W, :])
    x1 = x_ref[...] + mod_ref[2:3, :] * y
    x1_ref[...] = x1
    ms = jnp.mean(x1 * x1, axis=-1, keepdims=True)
    h = x1 * lax.rsqrt(ms + EPS) * g_ref[...]
    h2_ref[...] = (h * (1.0 + mod_ref[4:5, :]) + mod_ref[3:4, :]).astype(h2_ref.dtype)


def out_proj(parts, w_out, x, mod_l, g2, tm=512):
    n, d = x.shape
    part_spec = pl.BlockSpec((tm, GROUP_W), lambda i: (i, 0))
    return pl.pallas_call(
        _outproj_kernel,
        grid=(n // tm,),
        in_specs=[part_spec, part_spec, part_spec, part_spec,
                  pl.BlockSpec((4 * GROUP_W, d), lambda i: (0, 0)),
                  pl.BlockSpec((tm, d), lambda i: (i, 0)),
                  pl.BlockSpec((None, 6, d), lambda i: (_mod_index(i * tm), 0, 0)),
                  pl.BlockSpec((1, d), lambda i: (0, 0))],
        out_specs=[pl.BlockSpec((tm, d), lambda i: (i, 0)),
                   pl.BlockSpec((tm, d), lambda i: (i, 0))],
        out_shape=[jax.ShapeDtypeStruct((n, d), F32),
                   jax.ShapeDtypeStruct((n, d), BF16)],
        compiler_params=_cparams(("parallel",)),
        name="out_proj",
    )(*parts, w_out, x, mod_l, g2.reshape(1, d))


FFN_HALO = 16


def _ffn_kernel(h_ref, hp_ref, hn_ref, wa_ref, wb_ref, wd_ref, cw_ref, cb_ref, x1_ref, mod_ref, o_ref,
                ext_ref, aext_ref, acc_ref, *, tm, nf):
    i = pl.program_id(0)
    j = pl.program_id(1)

    @pl.when(j == 0)
    def _():
        ext_ref[0:FFN_HALO, :] = hp_ref[...]
        ext_ref[FFN_HALO:FFN_HALO + tm, :] = h_ref[...]
        ext_ref[FFN_HALO + tm:2 * FFN_HALO + tm, :] = hn_ref[...]
        acc_ref[...] = jnp.zeros_like(acc_ref)

    aext_ref[...] = _dot(ext_ref[...], wa_ref[...])
    b = _dot(h_ref[...], wb_ref[...])
    row0 = i * tm
    tseq = jnp.where(row0 < N_CTX, SEQ, DEC_SEQ)
    pos = (row0 + lax.broadcasted_iota(jnp.int32, (tm, 1), 0)) & (tseq - 1)
    a_prev = jnp.where(pos == 0, 0.0, aext_ref[FFN_HALO - 1:FFN_HALO - 1 + tm, :])
    a_mid = aext_ref[FFN_HALO:FFN_HALO + tm, :]
    a_next = jnp.where(pos == tseq - 1, 0.0, aext_ref[FFN_HALO + 1:FFN_HALO + 1 + tm, :])
    a = a_prev * cw_ref[0:1, :] + a_mid * cw_ref[1:2, :] + a_next * cw_ref[2:3, :] + cb_ref[...]
    act = a * jax.nn.sigmoid(a) * b
    acc_ref[...] += _dot(act.astype(BF16), wd_ref[...])

    @pl.when(j == nf - 1)
    def _():
        o_ref[...] = x1_ref[...] + mod_ref[5:6, :] * acc_ref[...]


def conv_ffn(h2, x1, mod_l, w_up, conv_w, conv_b, w_down, tm=512, tf=512):
    n, d = x1.shape
    f = w_down.shape[0]
    nf = f // tf
    nhb = n // FFN_HALO
    return pl.pallas_call(
        functools.partial(_ffn_kernel, tm=tm, nf=nf),
        grid=(n // tm, nf),
        in_specs=[pl.BlockSpec((tm, d), lambda i, j: (i, 0)),
                  pl.BlockSpec((FFN_HALO, d), lambda i, j: (jnp.maximum(i * (tm // FFN_HALO) - 1, 0), 0)),
                  pl.BlockSpec((FFN_HALO, d), lambda i, j: (jnp.minimum((i + 1) * (tm // FFN_HALO), nhb - 1), 0)),
                  pl.BlockSpec((d, tf), lambda i, j: (0, j)),
                  pl.BlockSpec((d, tf), lambda i, j: (0, nf + j)),
                  pl.BlockSpec((tf, d), lambda i, j: (j, 0)),
                  pl.BlockSpec((3, tf), lambda i, j: (0, j)),
                  pl.BlockSpec((1, tf), lambda i, j: (0, j)),
                  pl.BlockSpec((tm, d), lambda i, j: (i, 0)),
                  pl.BlockSpec((None, 6, d), lambda i, j: (_mod_index(i * tm), 0, 0))],
        out_specs=pl.BlockSpec((tm, d), lambda i, j: (i, 0)),
        out_shape=jax.ShapeDtypeStruct((n, d), F32),
        scratch_shapes=[pltpu.VMEM((tm + 2 * FFN_HALO, d), BF16),
                        pltpu.VMEM((tm + 2 * FFN_HALO, tf), F32),
                        pltpu.VMEM((tm, d), F32)],
        compiler_params=_cparams(("parallel", "arbitrary")),
        name="conv_ffn",
    )(h2, h2, h2, w_up, w_up, w_down, conv_w, conv_b.reshape(1, f), x1, mod_l)


def _rope_tables():
    half = HEAD_DIM // 2
    nf = half // 2
    inv = ROPE_BASE ** (-jnp.arange(nf, dtype=F32) / nf)
    t = jnp.arange(DEC_SEQ)
    rows = (t // GRID_W).astype(F32)
    cols = (t % GRID_W).astype(F32)
    ang = jnp.concatenate([rows[:, None] * inv[None]] * 2 + [cols[:, None] * inv[None]] * 2, axis=1)
    sign = jnp.tile(jnp.concatenate([-jnp.ones(nf, F32), jnp.ones(nf, F32)]), 2)
    cos = jnp.tile(jnp.cos(ang), (DEC_BATCH, ATT_KV_HEADS))
    sin = jnp.tile(jnp.sin(ang) * sign, (DEC_BATCH, ATT_KV_HEADS))
    cos = jnp.concatenate([jnp.ones((N_CTX, 128), F32), cos], axis=0)
    sin = jnp.concatenate([jnp.zeros((N_CTX, 128), F32), sin], axis=0)
    return cos, sin


def _to_groups(u):
    n = u.shape[0]
    return u.reshape(n // S5_L, S5_L, S5_GROUPS, S5_CH).transpose(2, 0, 1, 3).reshape(S5_GROUPS, n // S5_L, S5_L * S5_CH)


def _from_groups(y):
    n = y.shape[1] * S5_L
    return y.reshape(S5_GROUPS, n // S5_L, S5_L, S5_CH).transpose(1, 2, 0, 3).reshape(n, GROUP_W)


def kernel(x_prompt, x_sample, cache_k, cache_v, state_hgrn, state_s5_re, state_s5_im, c, c_ctx, norm1_g, norm2_g, ada_w, ada_b, w_in, w_out, pool_w, pool_scale, hg_lb_raw, hg_norm_g, q_norm_g, k_norm_g, att_sink, s5_a_re, s5_a_im, s5_log_dt, s5_b_re, s5_b_im, s5_c_re, s5_c_im, s5_d, s5_w_glu, ffn_w_up, ffn_conv_w, ffn_conv_b, ffn_w_down):
    d = D_MODEL
    x = jnp.concatenate([x_prompt.reshape(N_CTX, d), x_sample.reshape(N_LAT, d)], axis=0).astype(F32)

    cond = jnp.concatenate([c_ctx[None].astype(F32), c.astype(F32), jnp.zeros((8 - 1 - DEC_BATCH, d), F32)], axis=0)
    mod = adaln_mod(cond, ada_w, ada_b)

    lb_cum = jnp.cumsum(jax.nn.softmax(hg_lb_raw.astype(F32), axis=0), axis=0)
    hg_lb = lb_cum - lb_cum[:1]
    lp = jnp.stack([jnp.log(hg_lb), jnp.log1p(-hg_lb)], axis=2)

    cos_t, sin_t = _rope_tables()
    s0_ctx = jnp.zeros((BATCH, 2, HG_HEADS, HG_DK, HG_DK), F32)
    nchunk_ctx = N_CTX // S5_L

    ks_, vs_, hs_, s5_ = [], [], [], []
    for l in range(DEPTH):
        w_in_l = w_in[l]
        w_in_p = jnp.concatenate([w_in_l[:, :3584], w_in_l[:, 3840:4352], w_in_l[:, 3584:3840]], axis=1).astype(BF16)
        cols = in_proj(x, mod[l], norm1_g[l], w_in_p)

        y_pool = pool_mixer(cols, pool_w[l].astype(BF16), pool_scale[l])

        s0_lat = jnp.swapaxes(state_hgrn[:, l].astype(F32), -1, -2)
        of_c, ob_c, sfin = hgrn_scan(cols, lp[l], s0_ctx, 0, BATCH, SEQ)
        of_l, ob_l, _ = hgrn_scan(cols, lp[l], s0_lat, N_CTX, DEC_BATCH, DEC_SEQ)
        y_hg = hgrn_finish(jnp.concatenate([of_c, of_l], axis=0), jnp.concatenate([ob_c, ob_l], axis=0),
                           cols, hg_norm_g[l])
        hs_.append(sfin)

        qs, kn = qk_prep(cols, cos_t, sin_t, q_norm_g[l], k_norm_g[l])
        sink = att_sink[l].astype(F32)
        ya_c = attend_ctx(qs, kn, cols, sink, BATCH, SEQ)
        ya_l = attend_lat(qs, kn, cols, cache_k[:, l].reshape(DEC_BATCH, PAST_LEN, 128).astype(F32),
                          cache_v[:, l].reshape(DEC_BATCH, PAST_LEN, 128).astype(F32), sink, N_CTX, DEC_BATCH, DEC_SEQ)
        y_att = jnp.concatenate([ya_c, ya_l], axis=0)
        ks_.append(kn[:N_CTX].reshape(BATCH, SEQ, ATT_KV_HEADS, HEAD_DIM))
        vs_.append(cols[:N_CTX, 4224:4352].reshape(BATCH, SEQ, ATT_KV_HEADS, HEAD_DIM))

        ops = s5_operators(s5_a_re[l], s5_a_im[l], s5_log_dt[l], s5_b_re[l], s5_b_im[l], s5_c_re[l], s5_c_im[l])
        ug = _to_groups(cols[:, 3584:4096])
        h0_lat = jnp.concatenate([state_s5_re[:, l], state_s5_im[:, l]], axis=-1).astype(F32)
        h0_lat = jnp.pad(jnp.transpose(h0_lat, (2, 1, 0, 3)), ((0, 0), (0, 0), (0, 8 - DEC_BATCH), (0, 0)))
        yg_c, hf_c = s5_scan(ug, ops, None, 0, BATCH, SEQ // S5_L)
        yg_l, _ = s5_scan(ug, ops, h0_lat, nchunk_ctx, DEC_BATCH, DEC_SEQ // S5_L)
        y_s5 = s5_post(cols, _from_groups(jnp.concatenate([yg_c, yg_l], axis=1)), s5_d[l], s5_w_glu[l].astype(BF16))
        hf = hf_c.reshape(S5_GROUPS, 2, BATCH, SEQ // S5_L, 2 * S5_N)
        fin = jnp.stack([hf[:, 0, :, -1], hf[:, 1, :, 0]], axis=1)
        s5_.append(jnp.transpose(fin, (2, 1, 0, 3)))

        x1, h2 = out_proj((y_pool, y_hg, y_att, y_s5), w_out[l].astype(BF16), x, mod[l], norm2_g[l])
        x = conv_ffn(h2, x1, mod[l], ffn_w_up[l].astype(BF16), ffn_conv_w[l].astype(F32), ffn_conv_b[l].astype(F32),
                     ffn_w_down[l].astype(BF16))

    y_prompt = x[:N_CTX].reshape(BATCH, SEQ, d).astype(x_prompt.dtype)
    y_sample = x[N_CTX:].reshape(DEC_BATCH, DEC_SEQ, d).astype(x_sample.dtype)
    s5_all = jnp.stack(s5_, axis=1)
    return (y_prompt, y_sample, jnp.stack(ks_, axis=1), jnp.stack(vs_, axis=1), jnp.stack(hs_, axis=1),
            s5_all[..., :S5_N], s5_all[..., S5_N:])
```

```python
import functools

import jax
import jax.numpy as jnp
import numpy as np
from jax import lax
from jax.experimental import pallas as pl
from jax.experimental.pallas import tpu as pltpu

F32 = jnp.float32
BF16 = jnp.bfloat16
HIGHEST = lax.Precision.HIGHEST

D_MODEL = 2048
BATCH = 32
SEQ = 256
DEPTH = 2
DEC_BATCH = 2
DEC_SEQ = 4096
PAST_LEN = 256
GRID_W = 64
GROUP_W = 512
POOL_WINDOWS = (2, 4, 8, 16)
POOL_GW = 128
HG_HEADS = 4
HG_DK = 128
ATT_HEADS = 8
ATT_KV_HEADS = 2
ATT_GROUP = 4
HEAD_DIM = 64
WINDOW = 128
BLOCK = 128
ROPE_BASE = 10000.0
S5_CH = 16
S5_GROUPS = 32
S5_N = 64
FFN_DIM = 5632
EPS = 1e-6
NEG_INF = -1e30

N_CTX = BATCH * SEQ
N_LAT = DEC_BATCH * DEC_SEQ
N_TOK = N_CTX + N_LAT
IN_COLS = 4352

CB_POOL, CB_HQ, CB_FF, CB_FB, CB_HI, CB_HG, CB_AQ, CB_S5 = range(8)
CB_AK, CB_AV = 32, 33

S5_L = 16
HG_C = 128
VMEM_LIMIT = 56 * 1024 * 1024


def _cparams(sem):
    return pltpu.CompilerParams(dimension_semantics=sem, vmem_limit_bytes=VMEM_LIMIT)


def _dot(a, b):
    return jnp.dot(a, b, preferred_element_type=F32)


def _dot_nt(a, b):
    return lax.dot_general(a, b, (((1,), (1,)), ((), ())), preferred_element_type=F32)


def _dot_tn(a, b):
    return lax.dot_general(a, b, (((0,), (0,)), ((), ())), preferred_element_type=F32)


def _mod_index(row0):
    return jnp.where(row0 < N_CTX, 0, 1 + (row0 - N_CTX) // DEC_SEQ)


def _mod_kernel(cond_ref, w_ref, b_ref, o_ref):
    x = cond_ref[...]
    s = x * jax.nn.sigmoid(x)
    o_ref[...] = _dot(s.astype(BF16), w_ref[...].astype(BF16)) + b_ref[...]


def adaln_mod(cond, ada_w, ada_b, tn=1024):
    depth, d, n6 = ada_w.shape
    out = pl.pallas_call(
        _mod_kernel,
        grid=(depth, n6 // tn),
        in_specs=[pl.BlockSpec((8, d), lambda l, j: (0, 0)),
                  pl.BlockSpec((None, d, tn), lambda l, j: (l, 0, j)),
                  pl.BlockSpec((None, 1, tn), lambda l, j: (l, 0, j))],
        out_specs=pl.BlockSpec((None, 8, tn), lambda l, j: (l, 0, j)),
        out_shape=jax.ShapeDtypeStruct((depth, 8, n6), F32),
        compiler_params=_cparams(("parallel", "parallel")),
        name="adaln_mod",
    )(cond, ada_w, ada_b.reshape(depth, 1, n6))
    return out.reshape(depth, 8, 6, d)


def _split_specs(tm, d, row_tile):
    nc = N_CTX // tm
    return (pl.BlockSpec((tm, d), lambda *g: (jnp.minimum(row_tile(*g), nc - 1), 0)),
            pl.BlockSpec((tm, d), lambda *g: (jnp.maximum(row_tile(*g) - nc, 0), 0)))


def _inproj_kernel(xc_ref, xl_ref, mod_ref, g_ref, w_ref, o_ref, *, tm):
    x = jnp.where(pl.program_id(1) * tm < N_CTX, xc_ref[...], xl_ref[...])
    ms = jnp.mean(x * x, axis=-1, keepdims=True)
    h = x * lax.rsqrt(ms + EPS) * g_ref[...]
    h = h * (1.0 + mod_ref[1:2, :]) + mod_ref[0:1, :]
    o_ref[...] = _dot(h.astype(BF16), w_ref[...])


def in_proj(x_ctx, x_lat, mod_l, g1, w_in, tm=512, tn=2176):
    d = x_ctx.shape[1]
    n = x_ctx.shape[0] + x_lat.shape[0]
    ncols = w_in.shape[1]
    xc_spec, xl_spec = _split_specs(tm, d, lambda j, i: i)
    return pl.pallas_call(
        functools.partial(_inproj_kernel, tm=tm),
        grid=(ncols // tn, n // tm),
        in_specs=[xc_spec, xl_spec,
                  pl.BlockSpec((None, 6, d), lambda j, i: (_mod_index(i * tm), 0, 0)),
                  pl.BlockSpec((1, d), lambda j, i: (0, 0)),
                  pl.BlockSpec((d, tn), lambda j, i: (0, j))],
        out_specs=pl.BlockSpec((tm, tn), lambda j, i: (i, j)),
        out_shape=jax.ShapeDtypeStruct((n, ncols), F32),
        compiler_params=_cparams(("parallel", "parallel")),
        name="in_proj",
    )(x_ctx, x_lat, mod_l, g1.reshape(1, d), w_in)


def _pool_kernel(u_ref, up_ref, un_ref, w_ref, sc_ref, o_ref, ext_ref, *, tm):
    i = pl.program_id(0)
    row0 = i * tm
    tseq = jnp.where(row0 < N_CTX, SEQ, DEC_SEQ)
    has_prev = (row0 & (tseq - 1)) != 0
    has_next = ((row0 + tm) & (tseq - 1)) != 0
    ext_ref[0:8, :] = jnp.where(has_prev, up_ref[...], 0.0)
    ext_ref[8:8 + tm, :] = u_ref[...]
    ext_ref[8 + tm:16 + tm, :] = jnp.where(has_next, un_ref[...], 0.0)
    pos = (row0 + lax.broadcasted_iota(jnp.int32, (tm, 1), 0)) & (tseq - 1)
    outs = []
    for gi, w in enumerate(POOL_WINDOWS):
        ls = slice(gi * POOL_GW, (gi + 1) * POOL_GW)
        acc = ext_ref[8 - w // 2:8 - w // 2 + tm, ls]
        for o in range(-w // 2 + 1, w // 2):
            acc = acc + ext_ref[8 + o:8 + o + tm, ls]
        lo = jnp.maximum(pos - w // 2, 0)
        hi = jnp.minimum(pos + w // 2, tseq)
        cnt = (hi - lo).astype(F32)
        pooled = acc / cnt - ext_ref[8:8 + tm, ls]
        outs.append(_dot(pooled.astype(BF16), w_ref[gi]))
    o_ref[...] = (jnp.concatenate(outs, axis=1) * sc_ref[...]).astype(o_ref.dtype)


def pool_mixer(cols, pool_w, pool_scale, tm=256):
    n = cols.shape[0]
    nb8 = n // 8
    return pl.pallas_call(
        functools.partial(_pool_kernel, tm=tm),
        grid=(n // tm,),
        in_specs=[pl.BlockSpec((tm, GROUP_W), lambda i: (i, CB_POOL)),
                  pl.BlockSpec((8, GROUP_W), lambda i: (jnp.maximum(i * (tm // 8) - 1, 0), CB_POOL)),
                  pl.BlockSpec((8, GROUP_W), lambda i: (jnp.minimum((i + 1) * (tm // 8), nb8 - 1), CB_POOL)),
                  pl.BlockSpec((4, POOL_GW, POOL_GW), lambda i: (0, 0, 0)),
                  pl.BlockSpec((1, GROUP_W), lambda i: (0, 0))],
        out_specs=pl.BlockSpec((tm, GROUP_W), lambda i: (i, 0)),
        out_shape=jax.ShapeDtypeStruct((n, GROUP_W), BF16),
        scratch_shapes=[pltpu.VMEM((tm + 16, GROUP_W), F32)],
        compiler_params=_cparams(("parallel",)),
        name="pool_mixer",
    )(cols, cols, cols, pool_w, pool_scale.reshape(1, GROUP_W))


def _block_ref_rows(x, c, m, rev):
    w = x.shape[1]
    if 2 * m >= 8:
        r = m if rev else m - 1
        xr = x.reshape(c // (2 * m), 2 * m, w)[:, r:r + 1, :]
        return jnp.broadcast_to(xr, (c // (2 * m), 2 * m, w)).reshape(c, w)
    x8 = x.reshape(c // 8, 8, w)
    sub = lax.broadcasted_iota(jnp.int32, (c // 8, 8, w), 1)
    out = None
    for kb in range(8 // (2 * m)):
        r = kb * 2 * m + (m if rev else m - 1)
        cand = jnp.broadcast_to(x8[:, r:r + 1, :], (c // 8, 8, w))
        out = cand if out is None else jnp.where(sub >= kb * 2 * m, cand, out)
    return out.reshape(c, w)


def _hgrn_chunk(q, fx, v, log_l, log1m_l, st_ref, rev, c):
    ls = jnp.minimum(fx, 0.0) - jnp.log1p(jnp.exp(-jnp.abs(fx)))
    bb = log1m_l + ls
    logf = jnp.maximum(log_l, bb) + jnp.log1p(jnp.exp(-jnp.abs(log_l - bb)))
    kk = 1.0 - jnp.exp(logf)
    row = lax.broadcasted_iota(jnp.int32, (c, c), 0)
    col = lax.broadcasted_iota(jnp.int32, (c, c), 1)
    causal = (row <= col) if rev else (row >= col)
    b = jnp.dot(causal.astype(F32), logf, precision=HIGHEST, preferred_element_type=F32)
    b_last = b[0:1] if rev else b[c - 1:c]
    qt = (q * jnp.exp(b)).astype(BF16)
    kd = (kk * jnp.exp(b_last - b)).astype(BF16)
    dl = jnp.exp(b_last)
    xor = row ^ col
    levels = []
    m = c // 2
    while m >= 1:
        wgt = jnp.exp(-jnp.abs(b - _block_ref_rows(b, c, m, rev)))
        levels.append(((q * wgt).astype(BF16), (kk * wgt).astype(BF16), (xor >> (m.bit_length() - 1)) == 1))
        m //= 2
    qb = q.astype(BF16)
    kb = kk.astype(BF16)
    vb = v.astype(BF16)
    outs = []
    for h in range(HG_HEADS):
        sl = slice(h * HG_DK, (h + 1) * HG_DK)
        att = jnp.where(xor == 0, _dot_nt(qb[:, sl], kb[:, sl]), 0.0)
        for ql, kl, msk in levels:
            att = att + jnp.where(msk, _dot_nt(ql[:, sl], kl[:, sl]), 0.0)
        att = jnp.where(causal, att, 0.0)
        intra = _dot(att.astype(BF16), vb[:, sl])
        st = st_ref[h]
        inter = _dot_nt(qt[:, sl], st.astype(BF16))
        st_ref[h] = st * dl[:, sl] + _dot_tn(vb[:, sl], kd[:, sl])
        outs.append(intra + inter)
    return jnp.concatenate(outs, axis=1)


def _hgrn_kernel(qf_ref, ff_ref, vf_ref, qb_ref, fb_ref, vb_ref, lp_ref, s0_ref, *rest, c, nchunks):
    of_ref, ob_ref, sfin_ref, st_ref = rest[-4:]
    ci = pl.program_id(1)

    @pl.when(ci == 0)
    def _():
        st_ref[...] = s0_ref[...]

    of_ref[...] = _hgrn_chunk(qf_ref[...], ff_ref[...], vf_ref[...], lp_ref[0, 0:1, :], lp_ref[0, 1:2, :],
                              st_ref.at[0], False, c)
    ob_ref[...] = _hgrn_chunk(qb_ref[...], fb_ref[...], vb_ref[...], lp_ref[1, 0:1, :], lp_ref[1, 1:2, :],
                              st_ref.at[1], True, c)

    @pl.when(ci == nchunks - 1)
    def _():
        for d in range(2):
            for h in range(HG_HEADS):
                sfin_ref[d, h] = st_ref[d, h].T


def hgrn_scan(cols, lp, s0_t, row_off, nseq, t, prev=None, c=HG_C):
    n = cols.shape[0]
    nchunks = t // c
    base = row_off // c
    extra = [] if prev is None else list(prev)
    any_spec = pl.BlockSpec(memory_space=pl.ANY)

    def fwd(cb):
        return pl.BlockSpec((c, GROUP_W), lambda s, i: (base + s * nchunks + i, cb))

    def bwd(cb):
        return pl.BlockSpec((c, GROUP_W), lambda s, i: (base + s * nchunks + nchunks - 1 - i, cb))

    st_spec = pl.BlockSpec((None, 2, HG_HEADS, HG_DK, HG_DK), lambda s, i: (s, 0, 0, 0, 0))
    return pl.pallas_call(
        functools.partial(_hgrn_kernel, c=c, nchunks=nchunks),
        grid=(nseq, nchunks),
        in_specs=[fwd(CB_HQ), fwd(CB_FF), fwd(CB_HI), bwd(CB_HQ), bwd(CB_FB), bwd(CB_HI),
                  pl.BlockSpec((2, 2, GROUP_W), lambda s, i: (0, 0, 0)), st_spec] + [any_spec] * len(extra),
        out_specs=[fwd(0), bwd(0), st_spec],
        out_shape=[jax.ShapeDtypeStruct((n, GROUP_W), F32),
                   jax.ShapeDtypeStruct((n, GROUP_W), F32),
                   jax.ShapeDtypeStruct((nseq, 2, HG_HEADS, HG_DK, HG_DK), F32)],
        input_output_aliases={8 + k: k for k in range(len(extra))},
        scratch_shapes=[pltpu.VMEM((2, HG_HEADS, HG_DK, HG_DK), F32)],
        compiler_params=_cparams(("parallel", "arbitrary")),
        name="hgrn_scan",
    )(cols, cols, cols, cols, cols, cols, lp, s0_t, *extra)


def _hgrn_finish_kernel(of_ref, ob_ref, g_ref, ng_ref, o_ref):
    o = of_ref[...] + ob_ref[...]
    g = g_ref[...]
    gate = g * jax.nn.sigmoid(g)
    outs = []
    for h in range(HG_HEADS):
        sl = slice(h * HG_DK, (h + 1) * HG_DK)
        oh = o[:, sl]
        ms = jnp.mean(oh * oh, axis=-1, keepdims=True)
        outs.append(oh * lax.rsqrt(ms + EPS) * ng_ref[...])
    o_ref[...] = (jnp.concatenate(outs, axis=1) * gate).astype(o_ref.dtype)


def hgrn_finish(o_f, o_b, cols, norm_g, tm=1024):
    n = o_f.shape[0]
    return pl.pallas_call(
        _hgrn_finish_kernel,
        grid=(n // tm,),
        in_specs=[pl.BlockSpec((tm, GROUP_W), lambda i: (i, 0)),
                  pl.BlockSpec((tm, GROUP_W), lambda i: (i, 0)),
                  pl.BlockSpec((tm, GROUP_W), lambda i: (i, CB_HG)),
                  pl.BlockSpec((1, HG_DK), lambda i: (0, 0))],
        out_specs=pl.BlockSpec((tm, GROUP_W), lambda i: (i, 0)),
        out_shape=jax.ShapeDtypeStruct((n, GROUP_W), BF16),
        compiler_params=_cparams(("parallel",)),
        name="hgrn_finish",
    )(o_f, o_b, cols, norm_g.reshape(1, HG_DK))


def _swap16(x):
    w = x.shape[1]
    lane = lax.broadcasted_iota(jnp.int32, x.shape, 1)
    return jnp.where((lane & 31) < 16, pltpu.roll(x, w - 16, 1), pltpu.roll(x, 16, 1))


def _qkprep_kernel(q_ref, k_ref, cos_ref, sin_ref, gq_ref, gk_ref, bq_ref, qo_ref, ko_ref):
    cos = cos_ref[...]
    sin = sin_ref[...]
    q = q_ref[...]
    msq = jnp.dot(q * q, bq_ref[...], precision=HIGHEST, preferred_element_type=F32) * (1.0 / HEAD_DIM)
    qn = q * lax.rsqrt(msq + EPS) * gq_ref[...]
    cos4 = jnp.concatenate([cos] * 4, axis=1)
    sin4 = jnp.concatenate([sin] * 4, axis=1)
    qr = qn * cos4 + _swap16(qn) * sin4
    qo_ref[...] = (qr * (HEAD_DIM ** -0.5)).astype(qo_ref.dtype)
    k = k_ref[...]
    msk = jnp.dot(k * k, bq_ref[0:128, 0:128], precision=HIGHEST, preferred_element_type=F32) * (1.0 / HEAD_DIM)
    kn = k * lax.rsqrt(msk + EPS) * gk_ref[...]
    ko_ref[...] = kn * cos + _swap16(kn) * sin


def qk_prep(cols, cos_t, sin_t, q_norm_g, k_norm_g, tm=512):
    n = cols.shape[0]
    head = np.arange(GROUP_W) // HEAD_DIM
    bones = jnp.asarray((head[:, None] == head[None, :]).astype(np.float32))
    return pl.pallas_call(
        _qkprep_kernel,
        grid=(n // tm,),
        in_specs=[pl.BlockSpec((tm, GROUP_W), lambda i: (i, CB_AQ)),
                  pl.BlockSpec((tm, 128), lambda i: (i, CB_AK)),
                  pl.BlockSpec((tm, 128), lambda i: (i, 0)),
                  pl.BlockSpec((tm, 128), lambda i: (i, 0)),
                  pl.BlockSpec((1, GROUP_W), lambda i: (0, 0)),
                  pl.BlockSpec((1, 128), lambda i: (0, 0)),
                  pl.BlockSpec((GROUP_W, GROUP_W), lambda i: (0, 0))],
        out_specs=[pl.BlockSpec((tm, GROUP_W), lambda i: (i, 0)),
                   pl.BlockSpec((tm, 128), lambda i: (i, 0))],
        out_shape=[jax.ShapeDtypeStruct((n, GROUP_W), BF16),
                   jax.ShapeDtypeStruct((n, 128), F32)],
        compiler_params=_cparams(("parallel",)),
        name="qk_prep",
    )(cols, cols, cos_t, sin_t, jnp.tile(q_norm_g, ATT_HEADS).reshape(1, GROUP_W),
      jnp.tile(k_norm_g, ATT_KV_HEADS).reshape(1, 128), bones)


def _stack_heads(q, kh, rows):
    return jnp.concatenate(
        [q[:, (kh * ATT_GROUP + g) * HEAD_DIM:(kh * ATT_GROUP + g + 1) * HEAD_DIM] for g in range(ATT_GROUP)], axis=0)


def _sink_col(sink_ref, kh, rows):
    return jnp.concatenate(
        [jnp.full((rows, 1), sink_ref[kh * ATT_GROUP + g], F32) for g in range(ATT_GROUP)], axis=0)


def _attn_ctx_kernel(sink_ref, q_ref, k_ref, v_ref, o_ref, *, t):
    q = q_ref[...]
    k = k_ref[...].astype(BF16)
    v = v_ref[...].astype(BF16)
    pieces = [None] * ATT_HEADS
    for kh in range(ATT_KV_HEADS):
        hs = slice(kh * HEAD_DIM, (kh + 1) * HEAD_DIM)
        s = _dot_nt(_stack_heads(q, kh, t), k[:, hs])
        sk = _sink_col(sink_ref, kh, t)
        mx = jnp.maximum(jnp.max(s, axis=1, keepdims=True), sk)
        p = jnp.exp(s - mx)
        den = jnp.sum(p, axis=1, keepdims=True) + jnp.exp(sk - mx)
        o = _dot(p.astype(BF16), v[:, hs]) / den
        for g in range(ATT_GROUP):
            pieces[kh * ATT_GROUP + g] = o[g * t:(g + 1) * t]
    o_ref[...] = jnp.concatenate(pieces, axis=1).astype(o_ref.dtype)


def attend_ctx(qs, kn, cols, sink, nseq, t):
    return pl.pallas_call(
        functools.partial(_attn_ctx_kernel, t=t),
        grid=(nseq,),
        in_specs=[pl.BlockSpec(memory_space=pltpu.SMEM),
                  pl.BlockSpec((t, GROUP_W), lambda s: (s, 0)),
                  pl.BlockSpec((t, 128), lambda s: (s, 0)),
                  pl.BlockSpec((t, 128), lambda s: (s, CB_AV))],
        out_specs=pl.BlockSpec((t, GROUP_W), lambda s: (s, 0)),
        out_shape=jax.ShapeDtypeStruct((qs.shape[0], GROUP_W), BF16),
        compiler_params=_cparams(("parallel",)),
        name="attend_ctx",
    )(sink, qs, kn, cols)


def _attn_lat_kernel(sink_ref, q_ref, kp_ref, kc_ref, kn_ref, vp_ref, vc_ref, vn_ref, ck_ref, cv_ref, prev_ref,
                     o_ref, *, t, blk):
    del prev_ref
    i = pl.program_id(1)
    q = q_ref[...]
    kl = jnp.concatenate([kp_ref[...], kc_ref[...], kn_ref[...]], axis=0).astype(BF16)
    vl = jnp.concatenate([vp_ref[...], vc_ref[...], vn_ref[...]], axis=0).astype(BF16)
    ck = ck_ref[...].astype(BF16)
    cv = cv_ref[...].astype(BF16)
    rows = ATT_GROUP * blk
    qpos = i * blk + (lax.broadcasted_iota(jnp.int32, (rows, 3 * blk), 0) & (blk - 1))
    kpos = (i - 1) * blk + lax.broadcasted_iota(jnp.int32, (rows, 3 * blk), 1)
    valid = (kpos >= 0) & (kpos < t) & (jnp.abs(qpos - kpos) <= WINDOW)
    pieces = [None] * ATT_HEADS
    for kh in range(ATT_KV_HEADS):
        hs = slice(kh * HEAD_DIM, (kh + 1) * HEAD_DIM)
        qh = _stack_heads(q, kh, blk)
        s_loc = jnp.where(valid, _dot_nt(qh, kl[:, hs]), NEG_INF)
        s_ctx = _dot_nt(qh, ck[:, hs])
        sk = _sink_col(sink_ref, kh, blk)
        mx = jnp.maximum(jnp.maximum(jnp.max(s_loc, axis=1, keepdims=True),
                                     jnp.max(s_ctx, axis=1, keepdims=True)), sk)
        p_loc = jnp.exp(s_loc - mx)
        p_ctx = jnp.exp(s_ctx - mx)
        den = (jnp.sum(p_loc, axis=1, keepdims=True) + jnp.sum(p_ctx, axis=1, keepdims=True)
               + jnp.exp(sk - mx))
        o = (_dot(p_loc.astype(BF16), vl[:, hs]) + _dot(p_ctx.astype(BF16), cv[:, hs])) / den
        for g in range(ATT_GROUP):
            pieces[kh * ATT_GROUP + g] = o[g * blk:(g + 1) * blk]
    o_ref[...] = jnp.concatenate(pieces, axis=1).astype(o_ref.dtype)


def attend_lat(qs, kn, cols, cache_k, cache_v, sink, prev, row_off, nseq, t, blk=BLOCK):
    nb = t // blk
    base = row_off // blk
    lctx = cache_k.shape[1]

    def nbr(cb, off):
        return pl.BlockSpec((blk, 128), lambda b, i: (base + b * nb + jnp.clip(i + off, 0, nb - 1), cb))

    return pl.pallas_call(
        functools.partial(_attn_lat_kernel, t=t, blk=blk),
        grid=(nseq, nb),
        in_specs=[pl.BlockSpec(memory_space=pltpu.SMEM),
                  pl.BlockSpec((blk, GROUP_W), lambda b, i: (base + b * nb + i, 0)),
                  nbr(0, -1), nbr(0, 0), nbr(0, 1),
                  nbr(CB_AV, -1), nbr(CB_AV, 0), nbr(CB_AV, 1),
                  pl.BlockSpec((None, lctx, 128), lambda b, i: (b, 0, 0)),
                  pl.BlockSpec((None, lctx, 128), lambda b, i: (b, 0, 0)),
                  pl.BlockSpec(memory_space=pl.ANY)],
        out_specs=pl.BlockSpec((blk, GROUP_W), lambda b, i: (base + b * nb + i, 0)),
        out_shape=jax.ShapeDtypeStruct(prev.shape, BF16),
        input_output_aliases={10: 0},
        compiler_params=_cparams(("parallel", "parallel")),
        name="attend_lat",
    )(sink, qs, kn, kn, kn, cols, cols, cols, cache_k, cache_v, prev)


def s5_operators(a_re, a_im, log_dt, b_re, b_im, c_re, c_im):
    L = S5_L
    dt = jnp.exp(log_dt.astype(F32))[..., None]
    are, aim = a_re.astype(F32), a_im.astype(F32)
    adr, adi = (are * dt)[..., None, :], (aim * dt)[..., None, :]

    def lam_pow(p):
        p = jnp.asarray(p, F32)[None, :, None, :, None]
        mag = jnp.exp(adr * p)
        return mag * jnp.cos(adi * p), mag * jnp.sin(adi * p)

    ar = np.arange(L)
    one = np.ones((2, 1))
    lre, lim = lam_pow(one)
    lre, lim = lre[..., 0, :], lim[..., 0, :]
    den = are * are + aim * aim
    fre = ((lre - 1.0) * are + lim * aim) / den
    fim = (lim * are - (lre - 1.0) * aim) / den
    bre = fre[..., None] * b_re - fim[..., None] * b_im
    bim = fre[..., None] * b_im + fim[..., None] * b_re
    bret, bimt = jnp.swapaxes(bre, -1, -2), jnp.swapaxes(bim, -1, -2)
    cre, cim = c_re.astype(F32)[..., None, :, :], c_im.astype(F32)[..., None, :, :]
    ein = functools.partial(jnp.einsum, precision=HIGHEST)

    def c_lam(p):
        pr, pi = lam_pow(p)
        pr, pi = pr[..., None, :], pi[..., None, :]
        return cre * pr - cim * pi, cre * pi + cim * pr

    cpr, cpi = c_lam(one * ar)
    ktau = ein('ldgtcn,ldgne->ldgtce', cpr, bre) - ein('ldgtcn,ldgne->ldgtce', cpi, bim)
    lag = ar[None, :] - ar[:, None]
    shift = np.stack([(lag[None] == ar[:, None, None]), (-lag[None] == ar[:, None, None])]).astype(np.float32)
    kmat = ein('dxst,ldgxce->ldgsetc', shift, ktau).reshape(ktau.shape[:3] + (L * S5_CH, L * S5_CH))
    pr, pi = lam_pow(np.stack([L - 1 - ar, ar]))
    pr, pi = pr[..., None, :], pi[..., None, :]
    wr = pr * bret[..., None, :, :] - pi * bimt[..., None, :, :]
    wi = pr * bimt[..., None, :, :] + pi * bret[..., None, :, :]
    wst = jnp.concatenate([wr, wi], axis=-1).reshape(ktau.shape[:3] + (L * S5_CH, 2 * S5_N))
    orr, oii = c_lam(np.stack([ar + 1, L - ar]))
    wout = jnp.concatenate([jnp.moveaxis(orr, -1, -3), -jnp.moveaxis(oii, -1, -3)], axis=-3)
    wout = wout.reshape(ktau.shape[:3] + (2 * S5_N, L * S5_CH))
    jre, jim = lam_pow(one * (L * 2.0 ** np.arange(9)))
    lam = jnp.stack([jnp.concatenate([jre, jre], axis=-1), jnp.concatenate([-jim, jim], axis=-1)], axis=-2)
    return kmat.astype(BF16), wst.astype(BF16), wout.astype(BF16), lam


def _cmul(a_ref, d, j, x):
    return a_ref[d, j, 0:1, :] * x + a_ref[d, j, 1:2, :] * pltpu.roll(x, S5_N, 1)


S5_BUNDLE = 128 // S5_CH


def _s5_kernel(*refs, seglen, nseq, has_h0, has_prev, has_fin):
    u_ref, km_ref, ws_ref, wo_ref, lam_ref = refs[:5]
    h0_ref = refs[5] if has_h0 else None
    y_ref = refs[5 + has_h0 + has_prev]
    hfin_ref = refs[6 + has_h0 + has_prev] if has_fin else None
    gl = pl.program_id(1)
    r = nseq * seglen
    lane_blk = lax.broadcasted_iota(jnp.int32, (1, 128), 1) // S5_CH
    halves = []
    for v in range(S5_L // S5_BUNDLE):
        acc = None
        for k in range(S5_BUNDLE):
            slab = u_ref[pl.ds(v * S5_BUNDLE + k, r, stride=S5_L), :]
            rolled = pltpu.roll(slab, ((k - gl) & (S5_BUNDLE - 1)) * S5_CH, 1)
            acc = rolled if acc is None else jnp.where(lane_blk == k, rolled, acc)
        halves.append(acc)
    u = jnp.concatenate(halves, axis=1).astype(BF16)
    rowi = lax.broadcasted_iota(jnp.int32, (r, 1), 0)
    cl = rowi & (seglen - 1)
    y = None
    for d in range(2):
        edge = 0 if d == 0 else seglen - 1
        h0rows = jnp.zeros((r, 2 * S5_N), F32)
        x = _dot(u, ws_ref[d])
        if has_h0:
            for s in range(nseq):
                h0rows = jnp.where(rowi == s * seglen + edge, h0_ref[d, s:s + 1, :], h0rows)
            x = x + _cmul(lam_ref, d, 0, h0rows)
        sh, j = 1, 0
        while sh < seglen:
            if d == 0:
                xs, ok = pltpu.roll(x, sh, 0), cl >= sh
            else:
                xs, ok = pltpu.roll(x, r - sh, 0), cl < seglen - sh
            x = x + jnp.where(ok, _cmul(lam_ref, d, j, xs), 0.0)
            sh, j = sh * 2, j + 1
        if has_fin:
            hfin_ref[d] = x
        if d == 0:
            hin = jnp.where(cl >= 1, pltpu.roll(x, 1, 0), h0rows)
        else:
            hin = jnp.where(cl < seglen - 1, pltpu.roll(x, r - 1, 0), h0rows)
        yd = _dot(u, km_ref[d]) + _dot(hin.astype(BF16), wo_ref[d])
        y = yd if y is None else y + yd

    @pl.when(gl == 0)
    def _():
        y_ref[...] = jnp.zeros_like(y_ref)

    for v in range(S5_L // S5_BUNDLE):
        yv = y[:, v * 128:(v + 1) * 128]
        for k in range(S5_BUNDLE):
            idx = pl.ds(v * S5_BUNDLE + k, r, stride=S5_L)
            rolled = pltpu.roll(yv, ((gl - k) & (S5_BUNDLE - 1)) * S5_CH, 1)
            y_ref[idx, :] = jnp.where(lane_blk == gl, rolled, y_ref[idx, :])


def s5_scan(cols, ops, layer, h0, prev, row_off, nseq, seglen):
    kmat, wst, wout, lam = ops
    n = cols.shape[0]
    r = nseq * seglen
    rows = r * S5_L
    rb = row_off // rows
    assert row_off % rows == 0
    lw = S5_L * S5_CH
    cb0 = CB_S5 * (GROUP_W // 128)
    gidx = lambda j, gl: j * S5_BUNDLE + gl
    in_specs = [pl.BlockSpec((rows, 128), lambda j, gl: (rb, cb0 + j)),
                pl.BlockSpec((None, 2, None, lw, lw), lambda j, gl: (layer, 0, gidx(j, gl), 0, 0)),
                pl.BlockSpec((None, 2, None, lw, 2 * S5_N), lambda j, gl: (layer, 0, gidx(j, gl), 0, 0)),
                pl.BlockSpec((None, 2, None, 2 * S5_N, lw), lambda j, gl: (layer, 0, gidx(j, gl), 0, 0)),
                pl.BlockSpec((None, 2, None, 9, 2, 2 * S5_N), lambda j, gl: (layer, 0, gidx(j, gl), 0, 0, 0))]
    args = [cols, kmat, wst, wout, lam]
    if h0 is not None:
        in_specs.append(pl.BlockSpec((None, 2, h0.shape[2], 2 * S5_N), lambda j, gl: (gidx(j, gl), 0, 0, 0)))
        args.append(h0)
    aliases = {}
    if prev is not None:
        aliases = {len(args): 0}
        in_specs.append(pl.BlockSpec(memory_space=pl.ANY))
        args.append(prev)
    out_specs = [pl.BlockSpec((rows, 128), lambda j, gl: (rb, j))]
    out_shape = [jax.ShapeDtypeStruct((n, GROUP_W), F32)]
    has_fin = prev is None
    if has_fin:
        out_specs.append(pl.BlockSpec((None, 2, r, 2 * S5_N), lambda j, gl: (gidx(j, gl), 0, 0, 0)))
        out_shape.append(jax.ShapeDtypeStruct((S5_GROUPS, 2, r, 2 * S5_N), F32))
    return pl.pallas_call(
        functools.partial(_s5_kernel, seglen=seglen, nseq=nseq, has_h0=h0 is not None,
                          has_prev=prev is not None, has_fin=has_fin),
        grid=(S5_GROUPS // S5_BUNDLE, S5_BUNDLE),
        in_specs=in_specs,
        out_specs=out_specs,
        out_shape=out_shape,
        input_output_aliases=aliases,
        compiler_params=_cparams(("parallel", "arbitrary")),
        name="s5_scan",
    )(*args)


def _s5_post_kernel(u_ref, y_ref, d_ref, w_ref, o_ref):
    out = d_ref[...] * u_ref[...] + y_ref[...]
    yy = jax.nn.gelu(out)
    zz = _dot(yy.astype(BF16), w_ref[...])
    o_ref[...] = (zz[:, :GROUP_W] * jax.nn.sigmoid(zz[:, GROUP_W:])).astype(o_ref.dtype)


def s5_post(cols, y, s5_d, w_glu, tm=1024):
    n = cols.shape[0]
    return pl.pallas_call(
        _s5_post_kernel,
        grid=(n // tm,),
        in_specs=[pl.BlockSpec((tm, GROUP_W), lambda i: (i, CB_S5)),
                  pl.BlockSpec((tm, GROUP_W), lambda i: (i, 0)),
                  pl.BlockSpec((1, GROUP_W), lambda i: (0, 0)),
                  pl.BlockSpec((GROUP_W, 2 * GROUP_W), lambda i: (0, 0))],
        out_specs=pl.BlockSpec((tm, GROUP_W), lambda i: (i, 0)),
        out_shape=jax.ShapeDtypeStruct((n, GROUP_W), BF16),
        compiler_params=_cparams(("parallel",)),
        name="s5_post",
    )(cols, y, s5_d.reshape(1, GROUP_W), w_glu)


def _outproj_kernel(p0_ref, p1_ref, p2_ref, p3_ref, w_ref, xc_ref, xl_ref, mod_ref, g_ref, x1_ref, h2_ref, *, tm):
    y = _dot(p0_ref[...], w_ref[0:GROUP_W, :])
    y = y + _dot(p1_ref[...], w_ref[GROUP_W:2 * GROUP_W, :])
    y = y + _dot(p2_ref[...], w_ref[2 * GROUP_W:3 * GROUP_W, :])
    y = y + _dot(p3_ref[...], w_ref[3 * GROUP_W:4 * GROUP_W, :])
    x = jnp.where(pl.program_id(0) * tm < N_CTX, xc_ref[...], xl_ref[...])
    x1 = x + mod_ref[2:3, :] * y
    x1_ref[...] = x1
    ms = jnp.mean(x1 * x1, axis=-1, keepdims=True)
    h = x1 * lax.rsqrt(ms + EPS) * g_ref[...]
    h2_ref[...] = (h * (1.0 + mod_ref[4:5, :]) + mod_ref[3:4, :]).astype(h2_ref.dtype)


def out_proj(parts, w_out, x_ctx, x_lat, mod_l, g2, tm=512):
    d = x_ctx.shape[1]
    n = x_ctx.shape[0] + x_lat.shape[0]
    part_spec = pl.BlockSpec((tm, GROUP_W), lambda i: (i, 0))
    xc_spec, xl_spec = _split_specs(tm, d, lambda i: i)
    return pl.pallas_call(
        functools.partial(_outproj_kernel, tm=tm),
        grid=(n // tm,),
        in_specs=[part_spec, part_spec, part_spec, part_spec,
                  pl.BlockSpec((4 * GROUP_W, d), lambda i: (0, 0)),
                  xc_spec, xl_spec,
                  pl.BlockSpec((None, 6, d), lambda i: (_mod_index(i * tm), 0, 0)),
                  pl.BlockSpec((1, d), lambda i: (0, 0))],
        out_specs=[pl.BlockSpec((tm, d), lambda i: (i, 0)),
                   pl.BlockSpec((tm, d), lambda i: (i, 0))],
        out_shape=[jax.ShapeDtypeStruct((n, d), F32),
                   jax.ShapeDtypeStruct((n, d), BF16)],
        compiler_params=_cparams(("parallel",)),
        name="out_proj",
    )(*parts, w_out, x_ctx, x_lat, mod_l, g2.reshape(1, d))


FFN_HALO = 16


def _ffn_kernel(h_ref, hp_ref, hn_ref, wa_ref, wb_ref, wd_ref, cw_ref, cb_ref, x1_ref, mod_ref, oc_ref, ol_ref,
                ext_ref, aext_ref, acc_ref, *, tm, nf):
    i = pl.program_id(0)
    j = pl.program_id(1)

    @pl.when(j == 0)
    def _():
        ext_ref[0:FFN_HALO, :] = hp_ref[...]
        ext_ref[FFN_HALO:FFN_HALO + tm, :] = h_ref[...]
        ext_ref[FFN_HALO + tm:2 * FFN_HALO + tm, :] = hn_ref[...]
        acc_ref[...] = jnp.zeros_like(acc_ref)

    aext_ref[...] = _dot(ext_ref[...], wa_ref[...])
    b = _dot(h_ref[...], wb_ref[...])
    row0 = i * tm
    tseq = jnp.where(row0 < N_CTX, SEQ, DEC_SEQ)
    pos = (row0 + lax.broadcasted_iota(jnp.int32, (tm, 1), 0)) & (tseq - 1)
    a_prev = jnp.where(pos == 0, 0.0, aext_ref[FFN_HALO - 1:FFN_HALO - 1 + tm, :])
    a_mid = aext_ref[FFN_HALO:FFN_HALO + tm, :]
    a_next = jnp.where(pos == tseq - 1, 0.0, aext_ref[FFN_HALO + 1:FFN_HALO + 1 + tm, :])
    a = a_prev * cw_ref[0:1, :] + a_mid * cw_ref[1:2, :] + a_next * cw_ref[2:3, :] + cb_ref[...]
    act = a * jax.nn.sigmoid(a) * b
    acc_ref[...] += _dot(act.astype(BF16), wd_ref[...])

    @pl.when((j == nf - 1) & (row0 < N_CTX))
    def _():
        oc_ref[...] = x1_ref[...] + mod_ref[5:6, :] * acc_ref[...]

    @pl.when((j == nf - 1) & (row0 >= N_CTX))
    def _():
        ol_ref[...] = x1_ref[...] + mod_ref[5:6, :] * acc_ref[...]


def conv_ffn(h2, x1, mod_l, w_up, conv_w, conv_b, w_down, tm=512, tf=512):
    n, d = x1.shape
    f = w_down.shape[0]
    nf = f // tf
    nhb = n // FFN_HALO
    oc_spec, ol_spec = _split_specs(tm, d, lambda i, j: i)
    return pl.pallas_call(
        functools.partial(_ffn_kernel, tm=tm, nf=nf),
        grid=(n // tm, nf),
        in_specs=[pl.BlockSpec((tm, d), lambda i, j: (i, 0)),
                  pl.BlockSpec((FFN_HALO, d), lambda i, j: (jnp.maximum(i * (tm // FFN_HALO) - 1, 0), 0)),
                  pl.BlockSpec((FFN_HALO, d), lambda i, j: (jnp.minimum((i + 1) * (tm // FFN_HALO), nhb - 1), 0)),
                  pl.BlockSpec((d, tf), lambda i, j: (0, j)),
                  pl.BlockSpec((d, tf), lambda i, j: (0, nf + j)),
                  pl.BlockSpec((tf, d), lambda i, j: (j, 0)),
                  pl.BlockSpec((3, tf), lambda i, j: (0, j)),
                  pl.BlockSpec((1, tf), lambda i, j: (0, j)),
                  pl.BlockSpec((tm, d), lambda i, j: (i, 0)),
                  pl.BlockSpec((None, 6, d), lambda i, j: (_mod_index(i * tm), 0, 0))],
        out_specs=[oc_spec, ol_spec],
        out_shape=[jax.ShapeDtypeStruct((N_CTX, d), F32), jax.ShapeDtypeStruct((n - N_CTX, d), F32)],
        scratch_shapes=[pltpu.VMEM((tm + 2 * FFN_HALO, d), BF16),
                        pltpu.VMEM((tm + 2 * FFN_HALO, tf), F32),
                        pltpu.VMEM((tm, d), F32)],
        compiler_params=_cparams(("arbitrary", "arbitrary")),
        name="conv_ffn",
    )(h2, h2, h2, w_up, w_up, w_down, conv_w, conv_b.reshape(1, f), x1, mod_l)


def _rope_tables():
    half = HEAD_DIM // 2
    nf = half // 2
    inv = ROPE_BASE ** (-jnp.arange(nf, dtype=F32) / nf)
    t = jnp.arange(DEC_SEQ)
    rows = (t // GRID_W).astype(F32)
    cols = (t % GRID_W).astype(F32)
    ang = jnp.concatenate([rows[:, None] * inv[None]] * 2 + [cols[:, None] * inv[None]] * 2, axis=1)
    sign = jnp.tile(jnp.concatenate([-jnp.ones(nf, F32), jnp.ones(nf, F32)]), 2)
    cos = jnp.tile(jnp.cos(ang), (DEC_BATCH, ATT_KV_HEADS))
    sin = jnp.tile(jnp.sin(ang) * sign, (DEC_BATCH, ATT_KV_HEADS))
    cos = jnp.concatenate([jnp.ones((N_CTX, 128), F32), cos], axis=0)
    sin = jnp.concatenate([jnp.zeros((N_CTX, 128), F32), sin], axis=0)
    return cos, sin


def kernel(x_prompt, x_sample, cache_k, cache_v, state_hgrn, state_s5_re, state_s5_im, c, c_ctx, norm1_g, norm2_g, ada_w, ada_b, w_in, w_out, pool_w, pool_scale, hg_lb_raw, hg_norm_g, q_norm_g, k_norm_g, att_sink, s5_a_re, s5_a_im, s5_log_dt, s5_b_re, s5_b_im, s5_c_re, s5_c_im, s5_d, s5_w_glu, ffn_w_up, ffn_conv_w, ffn_conv_b, ffn_w_down):
    d = D_MODEL
    x_ctx = x_prompt.reshape(N_CTX, d).astype(F32)
    x_lat = x_sample.reshape(N_LAT, d).astype(F32)

    cond = jnp.concatenate([c_ctx[None].astype(F32), c.astype(F32), jnp.zeros((8 - 1 - DEC_BATCH, d), F32)], axis=0)
    mod = adaln_mod(cond, ada_w, ada_b)

    lb_cum = jnp.cumsum(jax.nn.softmax(hg_lb_raw.astype(F32), axis=0), axis=0)
    hg_lb = lb_cum - lb_cum[:1]
    lp = jnp.stack([jnp.log(hg_lb), jnp.log1p(-hg_lb)], axis=2)

    cos_t, sin_t = _rope_tables()
    s0_ctx = jnp.zeros((BATCH, 2, HG_HEADS, HG_DK, HG_DK), F32)
    s5_ops = s5_operators(s5_a_re, s5_a_im, s5_log_dt, s5_b_re, s5_b_im, s5_c_re, s5_c_im)

    ks_, vs_, hs_, s5_ = [], [], [], []
    for l in range(DEPTH):
        w_in_l = w_in[l]
        w_in_p = jnp.concatenate([w_in_l[:, :3584], w_in_l[:, 3840:4352], w_in_l[:, 3584:3840]], axis=1).astype(BF16)
        cols = in_proj(x_ctx, x_lat, mod[l], norm1_g[l], w_in_p)

        y_pool = pool_mixer(cols, pool_w[l].astype(BF16), pool_scale[l])

        s0_lat = jnp.swapaxes(state_hgrn[:, l].astype(F32), -1, -2)
        o_f, o_b, sfin = hgrn_scan(cols, lp[l], s0_ctx, 0, BATCH, SEQ)
        o_f, o_b, _ = hgrn_scan(cols, lp[l], s0_lat, N_CTX, DEC_BATCH, DEC_SEQ, prev=(o_f, o_b))
        y_hg = hgrn_finish(o_f, o_b, cols, hg_norm_g[l])
        hs_.append(sfin)

        qs, kn = qk_prep(cols, cos_t, sin_t, q_norm_g[l], k_norm_g[l])
        sink = att_sink[l].astype(F32)
        y_att = attend_ctx(qs, kn, cols, sink, BATCH, SEQ)
        y_att = attend_lat(qs, kn, cols, cache_k[:, l].reshape(DEC_BATCH, PAST_LEN, 128).astype(F32),
                           cache_v[:, l].reshape(DEC_BATCH, PAST_LEN, 128).astype(F32), sink, y_att,
                           N_CTX, DEC_BATCH, DEC_SEQ)
        ks_.append(kn[:N_CTX].reshape(BATCH, SEQ, ATT_KV_HEADS, HEAD_DIM))
        vs_.append(cols[:N_CTX, 4224:4352].reshape(BATCH, SEQ, ATT_KV_HEADS, HEAD_DIM))

        h0_lat = jnp.concatenate([state_s5_re[:, l], state_s5_im[:, l]], axis=-1).astype(F32)
        h0_lat = jnp.pad(jnp.transpose(h0_lat, (2, 1, 0, 3)), ((0, 0), (0, 0), (0, 8 - DEC_BATCH), (0, 0)))
        ys, hf_c = s5_scan(cols, s5_ops, l, None, None, 0, BATCH, SEQ // S5_L)
        ys, = s5_scan(cols, s5_ops, l, h0_lat, ys, N_CTX, DEC_BATCH, DEC_SEQ // S5_L)
        y_s5 = s5_post(cols, ys, s5_d[l], s5_w_glu[l].astype(BF16))
        hf = hf_c.reshape(S5_GROUPS, 2, BATCH, SEQ // S5_L, 2 * S5_N)
        fin = jnp.stack([hf[:, 0, :, -1], hf[:, 1, :, 0]], axis=1)
        s5_.append(jnp.transpose(fin, (2, 1, 0, 3)))

        x1, h2 = out_proj((y_pool, y_hg, y_att, y_s5), w_out[l].astype(BF16), x_ctx, x_lat, mod[l], norm2_g[l])
        x_ctx, x_lat = conv_ffn(h2, x1, mod[l], ffn_w_up[l].astype(BF16), ffn_conv_w[l].astype(F32),
                                ffn_conv_b[l].astype(F32), ffn_w_down[l].astype(BF16))

    y_prompt = x_ctx.reshape(BATCH, SEQ, d).astype(x_prompt.dtype)
    y_sample = x_lat.reshape(DEC_BATCH, DEC_SEQ, d).astype(x_sample.dtype)
    s5_all = jnp.stack(s5_, axis=1)
    return (y_prompt, y_sample, jnp.stack(ks_, axis=1), jnp.stack(vs_, axis=1), jnp.stack(hs_, axis=1),
            s5_all[..., :S5_N], s5_all[..., S5_N:])
```

```python
import functools

import jax
import jax.numpy as jnp
import numpy as np
from jax import lax
from jax.experimental import pallas as pl
from jax.experimental.pallas import tpu as pltpu

F32 = jnp.float32
BF16 = jnp.bfloat16
HIGHEST = lax.Precision.HIGHEST
LOG2E = 1.4426950408889634

D_MODEL = 2048
BATCH = 32
SEQ = 256
DEPTH = 2
DEC_BATCH = 2
DEC_SEQ = 4096
PAST_LEN = 256
GRID_W = 64
GROUP_W = 512
POOL_WINDOWS = (2, 4, 8, 16)
POOL_GW = 128
HG_HEADS = 4
HG_DK = 128
ATT_HEADS = 8
ATT_KV_HEADS = 2
ATT_GROUP = 4
HEAD_DIM = 64
WINDOW = 128
BLOCK = 128
ROPE_BASE = 10000.0
S5_CH = 16
S5_GROUPS = 32
S5_N = 64
FFN_DIM = 5632
EPS = 1e-6
NEG_INF = -1e30

N_CTX = BATCH * SEQ
N_LAT = DEC_BATCH * DEC_SEQ
N_TOK = N_CTX + N_LAT
IN_COLS = 4352

CB_POOL, CB_HQ, CB_FF, CB_FB, CB_HI, CB_HG, CB_AQ, CB_S5 = range(8)
CB_AK, CB_AV = 32, 33

S5_L = 16
HG_C = 128
VMEM_LIMIT = 56 * 1024 * 1024


def _cparams(sem):
    return pltpu.CompilerParams(dimension_semantics=sem, vmem_limit_bytes=VMEM_LIMIT)


def _dot(a, b):
    return jnp.dot(a, b, preferred_element_type=F32)


def _dot_nt(a, b):
    return lax.dot_general(a, b, (((1,), (1,)), ((), ())), preferred_element_type=F32)


def _dot_tn(a, b):
    return lax.dot_general(a, b, (((0,), (0,)), ((), ())), preferred_element_type=F32)


def _mod_index(row0):
    return jnp.where(row0 < N_CTX, 0, 1 + (row0 - N_CTX) // DEC_SEQ)


def _mod_kernel(cond_ref, w_ref, b_ref, o_ref):
    x = cond_ref[...]
    s = x * jax.nn.sigmoid(x)
    o_ref[...] = _dot(s.astype(BF16), w_ref[...].astype(BF16)) + b_ref[...]


def adaln_mod(cond, ada_w, ada_b, tn=1024):
    depth, d, n6 = ada_w.shape
    out = pl.pallas_call(
        _mod_kernel,
        grid=(depth, n6 // tn),
        in_specs=[pl.BlockSpec((8, d), lambda l, j: (0, 0)),
                  pl.BlockSpec((None, d, tn), lambda l, j: (l, 0, j)),
                  pl.BlockSpec((None, 1, tn), lambda l, j: (l, 0, j))],
        out_specs=pl.BlockSpec((None, 8, tn), lambda l, j: (l, 0, j)),
        out_shape=jax.ShapeDtypeStruct((depth, 8, n6), F32),
        compiler_params=_cparams(("parallel", "parallel")),
        name="adaln_mod",
    )(cond, ada_w, ada_b.reshape(depth, 1, n6))
    return out.reshape(depth, 8, 6, d)


def _split_specs(tm, d, row_tile):
    nc = N_CTX // tm
    return (pl.BlockSpec((tm, d), lambda *g: (jnp.minimum(row_tile(*g), nc - 1), 0)),
            pl.BlockSpec((tm, d), lambda *g: (jnp.maximum(row_tile(*g) - nc, 0), 0)))


def _inproj_kernel(xc_ref, xl_ref, mod_ref, g_ref, w_ref, o_ref, *, tm):
    x = jnp.where(pl.program_id(1) * tm < N_CTX, xc_ref[...], xl_ref[...])
    ms = jnp.mean(x * x, axis=-1, keepdims=True)
    h = x * lax.rsqrt(ms + EPS) * g_ref[...]
    h = h * (1.0 + mod_ref[1:2, :]) + mod_ref[0:1, :]
    o_ref[...] = _dot(h.astype(BF16), w_ref[...])


def in_proj(x_ctx, x_lat, mod_l, g1, w_in, tm=512, tn=2176):
    d = x_ctx.shape[1]
    n = x_ctx.shape[0] + x_lat.shape[0]
    ncols = w_in.shape[1]
    xc_spec, xl_spec = _split_specs(tm, d, lambda j, i: i)
    return pl.pallas_call(
        functools.partial(_inproj_kernel, tm=tm),
        grid=(ncols // tn, n // tm),
        in_specs=[xc_spec, xl_spec,
                  pl.BlockSpec((None, 6, d), lambda j, i: (_mod_index(i * tm), 0, 0)),
                  pl.BlockSpec((1, d), lambda j, i: (0, 0)),
                  pl.BlockSpec((d, tn), lambda j, i: (0, j))],
        out_specs=pl.BlockSpec((tm, tn), lambda j, i: (i, j)),
        out_shape=jax.ShapeDtypeStruct((n, ncols), F32),
        compiler_params=_cparams(("parallel", "parallel")),
        name="in_proj",
    )(x_ctx, x_lat, mod_l, g1.reshape(1, d), w_in)


def _pool_kernel(u_ref, up_ref, un_ref, w_ref, sc_ref, o_ref, ext_ref, *, tm):
    i = pl.program_id(0)
    row0 = i * tm
    tseq = jnp.where(row0 < N_CTX, SEQ, DEC_SEQ)
    has_prev = (row0 & (tseq - 1)) != 0
    has_next = ((row0 + tm) & (tseq - 1)) != 0
    ext_ref[0:8, :] = jnp.where(has_prev, up_ref[...], 0.0)
    ext_ref[8:8 + tm, :] = u_ref[...]
    ext_ref[8 + tm:16 + tm, :] = jnp.where(has_next, un_ref[...], 0.0)
    pos = (row0 + lax.broadcasted_iota(jnp.int32, (tm, 1), 0)) & (tseq - 1)
    outs = []
    for gi, w in enumerate(POOL_WINDOWS):
        ls = slice(gi * POOL_GW, (gi + 1) * POOL_GW)
        acc = ext_ref[8 - w // 2:8 - w // 2 + tm, ls]
        for o in range(-w // 2 + 1, w // 2):
            acc = acc + ext_ref[8 + o:8 + o + tm, ls]
        lo = jnp.maximum(pos - w // 2, 0)
        hi = jnp.minimum(pos + w // 2, tseq)
        cnt = (hi - lo).astype(F32)
        pooled = acc / cnt - ext_ref[8:8 + tm, ls]
        outs.append(_dot(pooled.astype(BF16), w_ref[gi]))
    o_ref[...] = (jnp.concatenate(outs, axis=1) * sc_ref[...]).astype(o_ref.dtype)


def pool_mixer(cols, pool_w, pool_scale, tm=256):
    n = cols.shape[0]
    nb8 = n // 8
    return pl.pallas_call(
        functools.partial(_pool_kernel, tm=tm),
        grid=(n // tm,),
        in_specs=[pl.BlockSpec((tm, GROUP_W), lambda i: (i, CB_POOL)),
                  pl.BlockSpec((8, GROUP_W), lambda i: (jnp.maximum(i * (tm // 8) - 1, 0), CB_POOL)),
                  pl.BlockSpec((8, GROUP_W), lambda i: (jnp.minimum((i + 1) * (tm // 8), nb8 - 1), CB_POOL)),
                  pl.BlockSpec((4, POOL_GW, POOL_GW), lambda i: (0, 0, 0)),
                  pl.BlockSpec((1, GROUP_W), lambda i: (0, 0))],
        out_specs=pl.BlockSpec((tm, GROUP_W), lambda i: (i, 0)),
        out_shape=jax.ShapeDtypeStruct((n, GROUP_W), BF16),
        scratch_shapes=[pltpu.VMEM((tm + 16, GROUP_W), F32)],
        compiler_params=_cparams(("parallel",)),
        name="pool_mixer",
    )(cols, cols, cols, pool_w, pool_scale.reshape(1, GROUP_W))


def _block_ref_rows(x, c, m, rev):
    w = x.shape[1]
    if 2 * m >= 8:
        r = m if rev else m - 1
        xr = x.reshape(c // (2 * m), 2 * m, w)[:, r:r + 1, :]
        return jnp.broadcast_to(xr, (c // (2 * m), 2 * m, w)).reshape(c, w)
    x8 = x.reshape(c // 8, 8, w)
    sub = lax.broadcasted_iota(jnp.int32, (c // 8, 8, w), 1)
    out = None
    for kb in range(8 // (2 * m)):
        r = kb * 2 * m + (m if rev else m - 1)
        cand = jnp.broadcast_to(x8[:, r:r + 1, :], (c // 8, 8, w))
        out = cand if out is None else jnp.where(sub >= kb * 2 * m, cand, out)
    return out.reshape(c, w)


def _hgrn_chunk(q, fx, v, log_l, log1m_l, st_ref, rev, c):
    ls = jnp.minimum(fx, 0.0) - jnp.log(1.0 + jnp.exp(-jnp.abs(fx)))
    bb = log1m_l + ls
    logf = jnp.maximum(log_l, bb) + jnp.log(1.0 + jnp.exp(-jnp.abs(log_l - bb)))
    kk = 1.0 - jnp.exp(logf)
    row = lax.broadcasted_iota(jnp.int32, (c, c), 0)
    col = lax.broadcasted_iota(jnp.int32, (c, c), 1)
    causal = (row <= col) if rev else (row >= col)
    b = jnp.dot(causal.astype(F32), logf, precision=HIGHEST, preferred_element_type=F32) * LOG2E
    b_last = b[0:1] if rev else b[c - 1:c]
    qt = (q * jnp.exp2(b)).astype(BF16)
    kd = (kk * jnp.exp2(b_last - b)).astype(BF16)
    dl = jnp.exp2(b_last)
    xor = row ^ col
    tbit = lax.broadcasted_iota(jnp.int32, (c, 1), 0)
    levels = []
    m = c // 2
    while m >= 1:
        wgt = jnp.exp2(-jnp.abs(b - _block_ref_rows(b, c, m, rev)))
        query_side = ((tbit & m) == 0) if rev else ((tbit & m) != 0)
        levels.append(((jnp.where(query_side, q, kk) * wgt).astype(BF16), (xor >> (m.bit_length() - 1)) == 1))
        m //= 2
    qb = q.astype(BF16)
    kb = kk.astype(BF16)
    vb = v.astype(BF16)
    outs = []
    for h in range(HG_HEADS):
        sl = slice(h * HG_DK, (h + 1) * HG_DK)
        att = jnp.where(xor == 0, _dot_nt(qb[:, sl], kb[:, sl]), 0.0)
        for z, msk in levels:
            att = att + jnp.where(msk, _dot_nt(z[:, sl], z[:, sl]), 0.0)
        att = jnp.where(causal, att, 0.0)
        intra = _dot(att.astype(BF16), vb[:, sl])
        st = st_ref[h]
        inter = _dot_nt(qt[:, sl], st.astype(BF16))
        st_ref[h] = st * dl[:, sl] + _dot_tn(vb[:, sl], kd[:, sl])
        outs.append(intra + inter)
    return jnp.concatenate(outs, axis=1)


def _hgrn_kernel(qf_ref, ff_ref, vf_ref, qb_ref, fb_ref, vb_ref, lp_ref, s0_ref, *rest, c, nchunks):
    of_ref, ob_ref, sfin_ref, st_ref = rest[-4:]
    ci = pl.program_id(1)

    @pl.when(ci == 0)
    def _():
        st_ref[...] = s0_ref[...]

    of_ref[...] = _hgrn_chunk(qf_ref[...], ff_ref[...], vf_ref[...], lp_ref[0, 0:1, :], lp_ref[0, 1:2, :],
                              st_ref.at[0], False, c)
    ob_ref[...] = _hgrn_chunk(qb_ref[...], fb_ref[...], vb_ref[...], lp_ref[1, 0:1, :], lp_ref[1, 1:2, :],
                              st_ref.at[1], True, c)

    @pl.when(ci == nchunks - 1)
    def _():
        for d in range(2):
            for h in range(HG_HEADS):
                sfin_ref[d, h] = st_ref[d, h].T


def hgrn_scan(cols, lp, s0_t, row_off, nseq, t, prev=None, c=HG_C):
    n = cols.shape[0]
    nchunks = t // c
    base = row_off // c
    extra = [] if prev is None else list(prev)
    any_spec = pl.BlockSpec(memory_space=pl.ANY)

    def fwd(cb):
        return pl.BlockSpec((c, GROUP_W), lambda s, i: (base + s * nchunks + i, cb))

    def bwd(cb):
        return pl.BlockSpec((c, GROUP_W), lambda s, i: (base + s * nchunks + nchunks - 1 - i, cb))

    st_spec = pl.BlockSpec((None, 2, HG_HEADS, HG_DK, HG_DK), lambda s, i: (s, 0, 0, 0, 0))
    return pl.pallas_call(
        functools.partial(_hgrn_kernel, c=c, nchunks=nchunks),
        grid=(nseq, nchunks),
        in_specs=[fwd(CB_HQ), fwd(CB_FF), fwd(CB_HI), bwd(CB_HQ), bwd(CB_FB), bwd(CB_HI),
                  pl.BlockSpec((2, 2, GROUP_W), lambda s, i: (0, 0, 0)), st_spec] + [any_spec] * len(extra),
        out_specs=[fwd(0), bwd(0), st_spec],
        out_shape=[jax.ShapeDtypeStruct((n, GROUP_W), F32),
                   jax.ShapeDtypeStruct((n, GROUP_W), F32),
                   jax.ShapeDtypeStruct((nseq, 2, HG_HEADS, HG_DK, HG_DK), F32)],
        input_output_aliases={8 + k: k for k in range(len(extra))},
        scratch_shapes=[pltpu.VMEM((2, HG_HEADS, HG_DK, HG_DK), F32)],
        compiler_params=_cparams(("parallel", "arbitrary")),
        name="hgrn_scan",
    )(cols, cols, cols, cols, cols, cols, lp, s0_t, *extra)


def _hgrn_finish_kernel(of_ref, ob_ref, g_ref, ng_ref, o_ref):
    o = of_ref[...] + ob_ref[...]
    g = g_ref[...]
    gate = g * jax.nn.sigmoid(g)
    outs = []
    for h in range(HG_HEADS):
        sl = slice(h * HG_DK, (h + 1) * HG_DK)
        oh = o[:, sl]
        ms = jnp.mean(oh * oh, axis=-1, keepdims=True)
        outs.append(oh * lax.rsqrt(ms + EPS) * ng_ref[...])
    o_ref[...] = (jnp.concatenate(outs, axis=1) * gate).astype(o_ref.dtype)


def hgrn_finish(o_f, o_b, cols, norm_g, tm=1024):
    n = o_f.shape[0]
    return pl.pallas_call(
        _hgrn_finish_kernel,
        grid=(n // tm,),
        in_specs=[pl.BlockSpec((tm, GROUP_W), lambda i: (i, 0)),
                  pl.BlockSpec((tm, GROUP_W), lambda i: (i, 0)),
                  pl.BlockSpec((tm, GROUP_W), lambda i: (i, CB_HG)),
                  pl.BlockSpec((1, HG_DK), lambda i: (0, 0))],
        out_specs=pl.BlockSpec((tm, GROUP_W), lambda i: (i, 0)),
        out_shape=jax.ShapeDtypeStruct((n, GROUP_W), BF16),
        compiler_params=_cparams(("parallel",)),
        name="hgrn_finish",
    )(o_f, o_b, cols, norm_g.reshape(1, HG_DK))


def _swap16(x):
    w = x.shape[1]
    lane = lax.broadcasted_iota(jnp.int32, x.shape, 1)
    return jnp.where((lane & 31) < 16, pltpu.roll(x, w - 16, 1), pltpu.roll(x, 16, 1))


def _qkprep_kernel(q_ref, k_ref, cos_ref, sin_ref, gq_ref, gk_ref, bq_ref, qo_ref, ko_ref):
    cos = cos_ref[...]
    sin = sin_ref[...]
    q = q_ref[...]
    msq = jnp.dot(q * q, bq_ref[...], precision=HIGHEST, preferred_element_type=F32) * (1.0 / HEAD_DIM)
    qn = q * lax.rsqrt(msq + EPS) * gq_ref[...]
    cos4 = jnp.concatenate([cos] * 4, axis=1)
    sin4 = jnp.concatenate([sin] * 4, axis=1)
    qr = qn * cos4 + _swap16(qn) * sin4
    qo_ref[...] = (qr * (HEAD_DIM ** -0.5)).astype(qo_ref.dtype)
    k = k_ref[...]
    msk = jnp.dot(k * k, bq_ref[0:128, 0:128], precision=HIGHEST, preferred_element_type=F32) * (1.0 / HEAD_DIM)
    kn = k * lax.rsqrt(msk + EPS) * gk_ref[...]
    ko_ref[...] = kn * cos + _swap16(kn) * sin


def qk_prep(cols, cos_t, sin_t, q_norm_g, k_norm_g, tm=512):
    n = cols.shape[0]
    head = np.arange(GROUP_W) // HEAD_DIM
    bones = jnp.asarray((head[:, None] == head[None, :]).astype(np.float32))
    return pl.pallas_call(
        _qkprep_kernel,
        grid=(n // tm,),
        in_specs=[pl.BlockSpec((tm, GROUP_W), lambda i: (i, CB_AQ)),
                  pl.BlockSpec((tm, 128), lambda i: (i, CB_AK)),
                  pl.BlockSpec((tm, 128), lambda i: (i, 0)),
                  pl.BlockSpec((tm, 128), lambda i: (i, 0)),
                  pl.BlockSpec((1, GROUP_W), lambda i: (0, 0)),
                  pl.BlockSpec((1, 128), lambda i: (0, 0)),
                  pl.BlockSpec((GROUP_W, GROUP_W), lambda i: (0, 0))],
        out_specs=[pl.BlockSpec((tm, GROUP_W), lambda i: (i, 0)),
                   pl.BlockSpec((tm, 128), lambda i: (i, 0))],
        out_shape=[jax.ShapeDtypeStruct((n, GROUP_W), BF16),
                   jax.ShapeDtypeStruct((n, 128), F32)],
        compiler_params=_cparams(("parallel",)),
        name="qk_prep",
    )(cols, cols, cos_t, sin_t, jnp.tile(q_norm_g, ATT_HEADS).reshape(1, GROUP_W),
      jnp.tile(k_norm_g, ATT_KV_HEADS).reshape(1, 128), bones)


def _stack_heads(q, kh, rows):
    return jnp.concatenate(
        [q[:, (kh * ATT_GROUP + g) * HEAD_DIM:(kh * ATT_GROUP + g + 1) * HEAD_DIM] for g in range(ATT_GROUP)], axis=0)


def _sink_col(sink_ref, kh, rows):
    return jnp.concatenate(
        [jnp.full((rows, 1), sink_ref[kh * ATT_GROUP + g], F32) for g in range(ATT_GROUP)], axis=0)


def _attn_ctx_kernel(sink_ref, q_ref, k_ref, v_ref, o_ref, *, t):
    q = q_ref[...]
    k = k_ref[...].astype(BF16)
    v = v_ref[...].astype(BF16)
    pieces = [None] * ATT_HEADS
    for kh in range(ATT_KV_HEADS):
        hs = slice(kh * HEAD_DIM, (kh + 1) * HEAD_DIM)
        s = _dot_nt(_stack_heads(q, kh, t), k[:, hs])
        sk = _sink_col(sink_ref, kh, t)
        mx = jnp.maximum(jnp.max(s, axis=1, keepdims=True), sk)
        p = jnp.exp(s - mx)
        den = jnp.sum(p, axis=1, keepdims=True) + jnp.exp(sk - mx)
        o = _dot(p.astype(BF16), v[:, hs]) / den
        for g in range(ATT_GROUP):
            pieces[kh * ATT_GROUP + g] = o[g * t:(g + 1) * t]
    o_ref[...] = jnp.concatenate(pieces, axis=1).astype(o_ref.dtype)


def attend_ctx(qs, kn, cols, sink, nseq, t):
    return pl.pallas_call(
        functools.partial(_attn_ctx_kernel, t=t),
        grid=(nseq,),
        in_specs=[pl.BlockSpec(memory_space=pltpu.SMEM),
                  pl.BlockSpec((t, GROUP_W), lambda s: (s, 0)),
                  pl.BlockSpec((t, 128), lambda s: (s, 0)),
                  pl.BlockSpec((t, 128), lambda s: (s, CB_AV))],
        out_specs=pl.BlockSpec((t, GROUP_W), lambda s: (s, 0)),
        out_shape=jax.ShapeDtypeStruct((qs.shape[0], GROUP_W), BF16),
        compiler_params=_cparams(("parallel",)),
        name="attend_ctx",
    )(sink, qs, kn, cols)


def _attn_lat_kernel(sink_ref, q_ref, kp_ref, kc_ref, kn_ref, vp_ref, vc_ref, vn_ref, ck_ref, cv_ref, prev_ref,
                     o_ref, *, t, blk):
    del prev_ref
    i = pl.program_id(1)
    q = q_ref[...]
    kl = jnp.concatenate([kp_ref[...], kc_ref[...], kn_ref[...]], axis=0).astype(BF16)
    vl = jnp.concatenate([vp_ref[...], vc_ref[...], vn_ref[...]], axis=0).astype(BF16)
    ck = ck_ref[...].astype(BF16)
    cv = cv_ref[...].astype(BF16)
    rows = ATT_GROUP * blk
    qpos = i * blk + (lax.broadcasted_iota(jnp.int32, (rows, 3 * blk), 0) & (blk - 1))
    kpos = (i - 1) * blk + lax.broadcasted_iota(jnp.int32, (rows, 3 * blk), 1)
    valid = (kpos >= 0) & (kpos < t) & (jnp.abs(qpos - kpos) <= WINDOW)
    pieces = [None] * ATT_HEADS
    for kh in range(ATT_KV_HEADS):
        hs = slice(kh * HEAD_DIM, (kh + 1) * HEAD_DIM)
        qh = _stack_heads(q, kh, blk)
        s_loc = jnp.where(valid, _dot_nt(qh, kl[:, hs]), NEG_INF)
        s_ctx = _dot_nt(qh, ck[:, hs])
        sk = _sink_col(sink_ref, kh, blk)
        mx = jnp.maximum(jnp.maximum(jnp.max(s_loc, axis=1, keepdims=True),
                                     jnp.max(s_ctx, axis=1, keepdims=True)), sk)
        p_loc = jnp.exp(s_loc - mx)
        p_ctx = jnp.exp(s_ctx - mx)
        den = (jnp.sum(p_loc, axis=1, keepdims=True) + jnp.sum(p_ctx, axis=1, keepdims=True)
               + jnp.exp(sk - mx))
        o = (_dot(p_loc.astype(BF16), vl[:, hs]) + _dot(p_ctx.astype(BF16), cv[:, hs])) / den
        for g in range(ATT_GROUP):
            pieces[kh * ATT_GROUP + g] = o[g * blk:(g + 1) * blk]
    o_ref[...] = jnp.concatenate(pieces, axis=1).astype(o_ref.dtype)


def attend_lat(qs, kn, cols, cache_k, cache_v, sink, prev, row_off, nseq, t, blk=2 * WINDOW):
    nb = t // blk
    base = row_off // blk
    lctx = cache_k.shape[1]

    def nbr(cb, off):
        return pl.BlockSpec((blk, 128), lambda b, i: (base + b * nb + jnp.clip(i + off, 0, nb - 1), cb))

    return pl.pallas_call(
        functools.partial(_attn_lat_kernel, t=t, blk=blk),
        grid=(nseq, nb),
        in_specs=[pl.BlockSpec(memory_space=pltpu.SMEM),
                  pl.BlockSpec((blk, GROUP_W), lambda b, i: (base + b * nb + i, 0)),
                  nbr(0, -1), nbr(0, 0), nbr(0, 1),
                  nbr(CB_AV, -1), nbr(CB_AV, 0), nbr(CB_AV, 1),
                  pl.BlockSpec((None, lctx, 128), lambda b, i: (b, 0, 0)),
                  pl.BlockSpec((None, lctx, 128), lambda b, i: (b, 0, 0)),
                  pl.BlockSpec(memory_space=pl.ANY)],
        out_specs=pl.BlockSpec((blk, GROUP_W), lambda b, i: (base + b * nb + i, 0)),
        out_shape=jax.ShapeDtypeStruct(prev.shape, BF16),
        input_output_aliases={10: 0},
        compiler_params=_cparams(("parallel", "parallel")),
        name="attend_lat",
    )(sink, qs, kn, kn, kn, cols, cols, cols, cache_k, cache_v, prev)


def s5_operators(a_re, a_im, log_dt, b_re, b_im, c_re, c_im):
    L = S5_L
    dt = jnp.exp(log_dt.astype(F32))[..., None]
    are, aim = a_re.astype(F32), a_im.astype(F32)
    adr, adi = (are * dt)[..., None, :], (aim * dt)[..., None, :]

    def lam_pow(p):
        p = jnp.asarray(p, F32)[None, :, None, :, None]
        mag = jnp.exp(adr * p)
        return mag * jnp.cos(adi * p), mag * jnp.sin(adi * p)

    ar = np.arange(L)
    one = np.ones((2, 1))
    lre, lim = lam_pow(one)
    lre, lim = lre[..., 0, :], lim[..., 0, :]
    den = are * are + aim * aim
    fre = ((lre - 1.0) * are + lim * aim) / den
    fim = (lim * are - (lre - 1.0) * aim) / den
    bre = fre[..., None] * b_re - fim[..., None] * b_im
    bim = fre[..., None] * b_im + fim[..., None] * b_re
    bret, bimt = jnp.swapaxes(bre, -1, -2), jnp.swapaxes(bim, -1, -2)
    cre, cim = c_re.astype(F32)[..., None, :, :], c_im.astype(F32)[..., None, :, :]
    ein = functools.partial(jnp.einsum, precision=HIGHEST)

    def c_lam(p):
        pr, pi = lam_pow(p)
        pr, pi = pr[..., None, :], pi[..., None, :]
        return cre * pr - cim * pi, cre * pi + cim * pr

    cpr, cpi = c_lam(one * ar)
    ktau = ein('ldgtcn,ldgne->ldgtce', cpr, bre) - ein('ldgtcn,ldgne->ldgtce', cpi, bim)
    lag = ar[None, :] - ar[:, None]
    shift = np.stack([(lag[None] == ar[:, None, None]), (-lag[None] == ar[:, None, None])]).astype(np.float32)
    kmat = ein('dxst,ldgxce->ldgsetc', shift, ktau).reshape(ktau.shape[:3] + (L * S5_CH, L * S5_CH))
    pr, pi = lam_pow(np.stack([L - 1 - ar, ar]))
    pr, pi = pr[..., None, :], pi[..., None, :]
    wr = pr * bret[..., None, :, :] - pi * bimt[..., None, :, :]
    wi = pr * bimt[..., None, :, :] + pi * bret[..., None, :, :]
    wst = jnp.concatenate([wr, wi], axis=-1).reshape(ktau.shape[:3] + (L * S5_CH, 2 * S5_N))
    orr, oii = c_lam(np.stack([ar + 1, L - ar]))
    wout = jnp.concatenate([jnp.moveaxis(orr, -1, -3), -jnp.moveaxis(oii, -1, -3)], axis=-3)
    wout = wout.reshape(ktau.shape[:3] + (2 * S5_N, L * S5_CH))
    jre, jim = lam_pow(one * (L * 2.0 ** np.arange(9)))
    lam = jnp.stack([jnp.concatenate([jre, jre], axis=-1), jnp.concatenate([-jim, jim], axis=-1)], axis=-2)
    return kmat.astype(BF16), wst.astype(BF16), wout.astype(BF16), lam


S5_BUNDLE = 128 // S5_CH


def _s5_group(gl, u, km_ref, ws_ref, wo_ref, lam_ref, h0_ref, hfin_ref, seglen, nseq):
    r = nseq * seglen
    rowi = lax.broadcasted_iota(jnp.int32, (r, 1), 0)
    cl = rowi & (seglen - 1)

    def cmul(d, j, x):
        return lam_ref[d, gl, j, 0:1, :] * x + lam_ref[d, gl, j, 1:2, :] * pltpu.roll(x, S5_N, 1)

    y = None
    for d in range(2):
        edge = 0 if d == 0 else seglen - 1
        h0rows = jnp.zeros((r, 2 * S5_N), F32)
        x = _dot(u, ws_ref[d, gl])
        if h0_ref is not None:
            for s in range(nseq):
                h0rows = jnp.where(rowi == s * seglen + edge, h0_ref[gl, d, s:s + 1, :], h0rows)
            x = x + cmul(d, 0, h0rows)
        sh, j = 1, 0
        while sh < seglen:
            if d == 0:
                xs, ok = pltpu.roll(x, sh, 0), cl >= sh
            else:
                xs, ok = pltpu.roll(x, r - sh, 0), cl < seglen - sh
            x = x + jnp.where(ok, cmul(d, j, xs), 0.0)
            sh, j = sh * 2, j + 1
        if hfin_ref is not None:
            hfin_ref[gl, d] = x
        if d == 0:
            hin = jnp.where(cl >= 1, pltpu.roll(x, 1, 0), h0rows)
        else:
            hin = jnp.where(cl < seglen - 1, pltpu.roll(x, r - 1, 0), h0rows)
        yd = _dot(u, km_ref[d, gl]) + _dot(hin.astype(BF16), wo_ref[d, gl])
        y = yd if y is None else y + yd
    return y


def _s5_kernel(*refs, seglen, nseq, has_h0, has_prev, has_fin):
    u_ref, km_ref, ws_ref, wo_ref, lam_ref = refs[:5]
    pos = 5
    h0_ref = refs[pos] if has_h0 else None
    pos += has_h0 + has_prev
    y_ref = refs[pos]
    hfin_ref = refs[pos + 1] if has_fin else None
    ycat_ref = refs[-1]
    lane_blk = lax.broadcasted_iota(jnp.int32, (1, 128), 1) // S5_CH
    nv = S5_L // S5_BUNDLE

    def body(gl, carry):
        halves = []
        for v in range(nv):
            acc = None
            for k in range(S5_BUNDLE):
                rolled = pltpu.roll(u_ref[v * S5_BUNDLE + k], ((k - gl) & (S5_BUNDLE - 1)) * S5_CH, 1)
                acc = rolled if acc is None else jnp.where(lane_blk == k, rolled, acc)
            halves.append(acc)
        u = jnp.concatenate(halves, axis=1).astype(BF16)
        ycat_ref[gl] = _s5_group(gl, u, km_ref, ws_ref, wo_ref, lam_ref, h0_ref, hfin_ref, seglen, nseq)
        return carry

    lax.fori_loop(0, S5_BUNDLE, body, 0)
    for v in range(nv):
        for k in range(S5_BUNDLE):
            acc = None
            for gl in range(S5_BUNDLE):
                yv = ycat_ref[gl, :, v * 128:(v + 1) * 128]
                rolled = pltpu.roll(yv, ((gl - k) % S5_BUNDLE) * S5_CH, 1) if gl != k else yv
                acc = rolled if acc is None else jnp.where(lane_blk == gl, rolled, acc)
            y_ref[v * S5_BUNDLE + k] = acc


S5_TILE = 512


def _s5_perm():
    p = np.arange(S5_TILE)
    src = (p % (S5_TILE // S5_L)) * S5_L + p // (S5_TILE // S5_L)
    return (src[:, None] == np.arange(S5_TILE)[None, :]).astype(np.float32)


def _split3(x):
    hi = x.astype(BF16)
    r1 = x - hi.astype(F32)
    mid = r1.astype(BF16)
    lo = (r1 - mid.astype(F32)).astype(BF16)
    return hi, mid, lo


def _permute_rows(p, x, transposed):
    dot = _dot_tn if transposed else _dot
    hi, mid, lo = _split3(x)
    return dot(p, hi) + dot(p, mid) + dot(p, lo)


def _s5_gather_kernel(u_ref, p_ref, o_ref):
    o_ref[...] = _permute_rows(p_ref[...], u_ref[...], False).reshape(o_ref.shape)


def s5_gather(cols):
    n = cols.shape[0]
    cpt = S5_TILE // S5_L
    return pl.pallas_call(
        _s5_gather_kernel,
        grid=(n // S5_TILE,),
        in_specs=[pl.BlockSpec((S5_TILE, GROUP_W), lambda i: (i, CB_S5)),
                  pl.BlockSpec((S5_TILE, S5_TILE), lambda i: (0, 0))],
        out_specs=pl.BlockSpec((S5_L, cpt, GROUP_W), lambda i: (0, i, 0)),
        out_shape=jax.ShapeDtypeStruct((S5_L, n // S5_L, GROUP_W), F32),
        compiler_params=_cparams(("parallel",)),
        name="s5_gather",
    )(cols, jnp.asarray(_s5_perm(), BF16))


def s5_scan(ud, ops, layer, h0, prev, row_off, nseq, seglen):
    kmat, wst, wout, lam = ops
    nchunks = ud.shape[1]
    r = nseq * seglen
    rb = row_off // (r * S5_L)
    assert row_off % (r * S5_L) == 0
    lw = S5_L * S5_CH
    nb = S5_BUNDLE
    in_specs = [pl.BlockSpec((S5_L, r, 128), lambda j: (0, rb, j)),
                pl.BlockSpec((None, 2, nb, lw, lw), lambda j: (layer, 0, j, 0, 0)),
                pl.BlockSpec((None, 2, nb, lw, 2 * S5_N), lambda j: (layer, 0, j, 0, 0)),
                pl.BlockSpec((None, 2, nb, 2 * S5_N, lw), lambda j: (layer, 0, j, 0, 0)),
                pl.BlockSpec((None, 2, nb, 9, 2, 2 * S5_N), lambda j: (layer, 0, j, 0, 0, 0))]
    args = [ud, kmat, wst, wout, lam]
    if h0 is not None:
        in_specs.append(pl.BlockSpec((nb, 2, h0.shape[2], 2 * S5_N), lambda j: (j, 0, 0, 0)))
        args.append(h0)
    aliases = {}
    if prev is not None:
        aliases = {len(args): 0}
        in_specs.append(pl.BlockSpec(memory_space=pl.ANY))
        args.append(prev)
    out_specs = [pl.BlockSpec((S5_L, r, 128), lambda j: (0, rb, j))]
    out_shape = [jax.ShapeDtypeStruct((S5_L, nchunks, GROUP_W), F32)]
    has_fin = prev is None
    if has_fin:
        out_specs.append(pl.BlockSpec((nb, 2, r, 2 * S5_N), lambda j: (j, 0, 0, 0)))
        out_shape.append(jax.ShapeDtypeStruct((S5_GROUPS, 2, r, 2 * S5_N), F32))
    return pl.pallas_call(
        functools.partial(_s5_kernel, seglen=seglen, nseq=nseq, has_h0=h0 is not None,
                          has_prev=prev is not None, has_fin=has_fin),
        grid=(S5_GROUPS // nb,),
        in_specs=in_specs,
        out_specs=out_specs,
        out_shape=out_shape,
        input_output_aliases=aliases,
        scratch_shapes=[pltpu.VMEM((nb, r, lw), F32)],
        compiler_params=_cparams(("parallel",)),
        name="s5_scan",
    )(*args)


def _s5_post_kernel(u_ref, y_ref, p_ref, d_ref, w_ref, o_ref):
    y = _permute_rows(p_ref[...], y_ref[...].reshape(S5_TILE, GROUP_W), True)
    out = d_ref[...] * u_ref[...] + y
    yy = jax.nn.gelu(out)
    zz = _dot(yy.astype(BF16), w_ref[...])
    o_ref[...] = (zz[:, :GROUP_W] * jax.nn.sigmoid(zz[:, GROUP_W:])).astype(o_ref.dtype)


def s5_post(cols, y, s5_d, w_glu):
    n = cols.shape[0]
    cpt = S5_TILE // S5_L
    return pl.pallas_call(
        _s5_post_kernel,
        grid=(n // S5_TILE,),
        in_specs=[pl.BlockSpec((S5_TILE, GROUP_W), lambda i: (i, CB_S5)),
                  pl.BlockSpec((S5_L, cpt, GROUP_W), lambda i: (0, i, 0)),
                  pl.BlockSpec((S5_TILE, S5_TILE), lambda i: (0, 0)),
                  pl.BlockSpec((1, GROUP_W), lambda i: (0, 0)),
                  pl.BlockSpec((GROUP_W, 2 * GROUP_W), lambda i: (0, 0))],
        out_specs=pl.BlockSpec((S5_TILE, GROUP_W), lambda i: (i, 0)),
        out_shape=jax.ShapeDtypeStruct((n, GROUP_W), BF16),
        compiler_params=_cparams(("parallel",)),
        name="s5_post",
    )(cols, y, jnp.asarray(_s5_perm(), BF16), s5_d.reshape(1, GROUP_W), w_glu)


def _outproj_kernel(p0_ref, p1_ref, p2_ref, p3_ref, w_ref, xc_ref, xl_ref, mod_ref, g_ref, x1_ref, h2_ref, *, tm):
    y = _dot(p0_ref[...], w_ref[0:GROUP_W, :])
    y = y + _dot(p1_ref[...], w_ref[GROUP_W:2 * GROUP_W, :])
    y = y + _dot(p2_ref[...], w_ref[2 * GROUP_W:3 * GROUP_W, :])
    y = y + _dot(p3_ref[...], w_ref[3 * GROUP_W:4 * GROUP_W, :])
    x = jnp.where(pl.program_id(0) * tm < N_CTX, xc_ref[...], xl_ref[...])
    x1 = x + mod_ref[2:3, :] * y
    x1_ref[...] = x1
    ms = jnp.mean(x1 * x1, axis=-1, keepdims=True)
    h = x1 * lax.rsqrt(ms + EPS) * g_ref[...]
    h2_ref[...] = (h * (1.0 + mod_ref[4:5, :]) + mod_ref[3:4, :]).astype(h2_ref.dtype)


def out_proj(parts, w_out, x_ctx, x_lat, mod_l, g2, tm=512):
    d = x_ctx.shape[1]
    n = x_ctx.shape[0] + x_lat.shape[0]
    part_spec = pl.BlockSpec((tm, GROUP_W), lambda i: (i, 0))
    xc_spec, xl_spec = _split_specs(tm, d, lambda i: i)
    return pl.pallas_call(
        functools.partial(_outproj_kernel, tm=tm),
        grid=(n // tm,),
        in_specs=[part_spec, part_spec, part_spec, part_spec,
                  pl.BlockSpec((4 * GROUP_W, d), lambda i: (0, 0)),
                  xc_spec, xl_spec,
                  pl.BlockSpec((None, 6, d), lambda i: (_mod_index(i * tm), 0, 0)),
                  pl.BlockSpec((1, d), lambda i: (0, 0))],
        out_specs=[pl.BlockSpec((tm, d), lambda i: (i, 0)),
                   pl.BlockSpec((tm, d), lambda i: (i, 0))],
        out_shape=[jax.ShapeDtypeStruct((n, d), F32),
                   jax.ShapeDtypeStruct((n, d), BF16)],
        compiler_params=_cparams(("parallel",)),
        name="out_proj",
    )(*parts, w_out, x_ctx, x_lat, mod_l, g2.reshape(1, d))


FFN_HALO = 16


def _ffn_kernel(h_ref, hp_ref, hn_ref, wa_ref, wb_ref, wd_ref, cw_ref, cb_ref, x1_ref, mod_ref, oc_ref, ol_ref,
                ext_ref, aext_ref, acc_ref, *, tm, nf):
    i = pl.program_id(0)
    j = pl.program_id(1)

    @pl.when(j == 0)
    def _():
        ext_ref[0:FFN_HALO, :] = hp_ref[...]
        ext_ref[FFN_HALO:FFN_HALO + tm, :] = h_ref[...]
        ext_ref[FFN_HALO + tm:2 * FFN_HALO + tm, :] = hn_ref[...]
        acc_ref[...] = jnp.zeros_like(acc_ref)

    aext_ref[...] = _dot(ext_ref[...], wa_ref[...])
    b = _dot(h_ref[...], wb_ref[...])
    row0 = i * tm
    tseq = jnp.where(row0 < N_CTX, SEQ, DEC_SEQ)
    pos = (row0 + lax.broadcasted_iota(jnp.int32, (tm, 1), 0)) & (tseq - 1)
    a_prev = jnp.where(pos == 0, 0.0, aext_ref[FFN_HALO - 1:FFN_HALO - 1 + tm, :])
    a_mid = aext_ref[FFN_HALO:FFN_HALO + tm, :]
    a_next = jnp.where(pos == tseq - 1, 0.0, aext_ref[FFN_HALO + 1:FFN_HALO + 1 + tm, :])
    a = a_prev * cw_ref[0:1, :] + a_mid * cw_ref[1:2, :] + a_next * cw_ref[2:3, :] + cb_ref[...]
    act = a * jax.nn.sigmoid(a) * b
    acc_ref[...] += _dot(act.astype(BF16), wd_ref[...])

    @pl.when((j == nf - 1) & (row0 < N_CTX))
    def _():
        oc_ref[...] = x1_ref[...] + mod_ref[5:6, :] * acc_ref[...]

    @pl.when((j == nf - 1) & (row0 >= N_CTX))
    def _():
        ol_ref[...] = x1_ref[...] + mod_ref[5:6, :] * acc_ref[...]


def conv_ffn(h2, x1, mod_l, w_up, conv_w, conv_b, w_down, tm=512, tf=512):
    n, d = x1.shape
    f = w_down.shape[0]
    nf = f // tf
    nhb = n // FFN_HALO
    oc_spec, ol_spec = _split_specs(tm, d, lambda i, j: i)
    return pl.pallas_call(
        functools.partial(_ffn_kernel, tm=tm, nf=nf),
        grid=(n // tm, nf),
        in_specs=[pl.BlockSpec((tm, d), lambda i, j: (i, 0)),
                  pl.BlockSpec((FFN_HALO, d), lambda i, j: (jnp.maximum(i * (tm // FFN_HALO) - 1, 0), 0)),
                  pl.BlockSpec((FFN_HALO, d), lambda i, j: (jnp.minimum((i + 1) * (tm // FFN_HALO), nhb - 1), 0)),
                  pl.BlockSpec((d, tf), lambda i, j: (0, j)),
                  pl.BlockSpec((d, tf), lambda i, j: (0, nf + j)),
                  pl.BlockSpec((tf, d), lambda i, j: (j, 0)),
                  pl.BlockSpec((3, tf), lambda i, j: (0, j)),
                  pl.BlockSpec((1, tf), lambda i, j: (0, j)),
                  pl.BlockSpec((tm, d), lambda i, j: (i, 0)),
                  pl.BlockSpec((None, 6, d), lambda i, j: (_mod_index(i * tm), 0, 0))],
        out_specs=[oc_spec, ol_spec],
        out_shape=[jax.ShapeDtypeStruct((N_CTX, d), F32), jax.ShapeDtypeStruct((n - N_CTX, d), F32)],
        scratch_shapes=[pltpu.VMEM((tm + 2 * FFN_HALO, d), BF16),
                        pltpu.VMEM((tm + 2 * FFN_HALO, tf), F32),
                        pltpu.VMEM((tm, d), F32)],
        compiler_params=_cparams(("arbitrary", "arbitrary")),
        name="conv_ffn",
    )(h2, h2, h2, w_up, w_up, w_down, conv_w, conv_b.reshape(1, f), x1, mod_l)


def _rope_tables():
    half = HEAD_DIM // 2
    nf = half // 2
    inv = ROPE_BASE ** (-jnp.arange(nf, dtype=F32) / nf)
    t = jnp.arange(DEC_SEQ)
    rows = (t // GRID_W).astype(F32)
    cols = (t % GRID_W).astype(F32)
    ang = jnp.concatenate([rows[:, None] * inv[None]] * 2 + [cols[:, None] * inv[None]] * 2, axis=1)
    sign = jnp.tile(jnp.concatenate([-jnp.ones(nf, F32), jnp.ones(nf, F32)]), 2)
    cos = jnp.tile(jnp.cos(ang), (DEC_BATCH, ATT_KV_HEADS))
    sin = jnp.tile(jnp.sin(ang) * sign, (DEC_BATCH, ATT_KV_HEADS))
    cos = jnp.concatenate([jnp.ones((N_CTX, 128), F32), cos], axis=0)
    sin = jnp.concatenate([jnp.zeros((N_CTX, 128), F32), sin], axis=0)
    return cos, sin


def kernel(x_prompt, x_sample, cache_k, cache_v, state_hgrn, state_s5_re, state_s5_im, c, c_ctx, norm1_g, norm2_g, ada_w, ada_b, w_in, w_out, pool_w, pool_scale, hg_lb_raw, hg_norm_g, q_norm_g, k_norm_g, att_sink, s5_a_re, s5_a_im, s5_log_dt, s5_b_re, s5_b_im, s5_c_re, s5_c_im, s5_d, s5_w_glu, ffn_w_up, ffn_conv_w, ffn_conv_b, ffn_w_down):
    d = D_MODEL
    x_ctx = x_prompt.reshape(N_CTX, d).astype(F32)
    x_lat = x_sample.reshape(N_LAT, d).astype(F32)

    cond = jnp.concatenate([c_ctx[None].astype(F32), c.astype(F32), jnp.zeros((8 - 1 - DEC_BATCH, d), F32)], axis=0)
    mod = adaln_mod(cond, ada_w, ada_b)

    lb_cum = jnp.cumsum(jax.nn.softmax(hg_lb_raw.astype(F32), axis=0), axis=0)
    hg_lb = lb_cum - lb_cum[:1]
    lp = jnp.stack([jnp.log(hg_lb), jnp.log1p(-hg_lb)], axis=2)

    cos_t, sin_t = _rope_tables()
    s0_ctx = jnp.zeros((BATCH, 2, HG_HEADS, HG_DK, HG_DK), F32)
    s5_ops = s5_operators(s5_a_re, s5_a_im, s5_log_dt, s5_b_re, s5_b_im, s5_c_re, s5_c_im)

    ks_, vs_, hs_, s5_ = [], [], [], []
    for l in range(DEPTH):
        w_in_l = w_in[l]
        w_in_p = jnp.concatenate([w_in_l[:, :3584], w_in_l[:, 3840:4352], w_in_l[:, 3584:3840]], axis=1).astype(BF16)
        cols = in_proj(x_ctx, x_lat, mod[l], norm1_g[l], w_in_p)

        y_pool = pool_mixer(cols, pool_w[l].astype(BF16), pool_scale[l])

        s0_lat = jnp.swapaxes(state_hgrn[:, l].astype(F32), -1, -2)
        o_f, o_b, sfin = hgrn_scan(cols, lp[l], s0_ctx, 0, BATCH, SEQ)
        o_f, o_b, _ = hgrn_scan(cols, lp[l], s0_lat, N_CTX, DEC_BATCH, DEC_SEQ, prev=(o_f, o_b))
        y_hg = hgrn_finish(o_f, o_b, cols, hg_norm_g[l])
        hs_.append(sfin)

        qs, kn = qk_prep(cols, cos_t, sin_t, q_norm_g[l], k_norm_g[l])
        sink = att_sink[l].astype(F32)
        y_att = attend_ctx(qs, kn, cols, sink, BATCH, SEQ)
        y_att = attend_lat(qs, kn, cols, cache_k[:, l].reshape(DEC_BATCH, PAST_LEN, 128).astype(F32),
                           cache_v[:, l].reshape(DEC_BATCH, PAST_LEN, 128).astype(F32), sink, y_att,
                           N_CTX, DEC_BATCH, DEC_SEQ)
        ks_.append(kn[:N_CTX].reshape(BATCH, SEQ, ATT_KV_HEADS, HEAD_DIM))
        vs_.append(cols[:N_CTX, 4224:4352].reshape(BATCH, SEQ, ATT_KV_HEADS, HEAD_DIM))

        h0_lat = jnp.concatenate([state_s5_re[:, l], state_s5_im[:, l]], axis=-1).astype(F32)
        h0_lat = jnp.pad(jnp.transpose(h0_lat, (2, 1, 0, 3)), ((0, 0), (0, 0), (0, 8 - DEC_BATCH), (0, 0)))
        ud = s5_gather(cols)
        ys, hf_c = s5_scan(ud, s5_ops, l, None, None, 0, BATCH, SEQ // S5_L)
        ys, = s5_scan(ud, s5_ops, l, h0_lat, ys, N_CTX, DEC_BATCH, DEC_SEQ // S5_L)
        y_s5 = s5_post(cols, ys, s5_d[l], s5_w_glu[l].astype(BF16))
        hf = hf_c.reshape(S5_GROUPS, 2, BATCH, SEQ // S5_L, 2 * S5_N)
        fin = jnp.stack([hf[:, 0, :, -1], hf[:, 1, :, 0]], axis=1)
        s5_.append(jnp.transpose(fin, (2, 1, 0, 3)))

        x1, h2 = out_proj((y_pool, y_hg, y_att, y_s5), w_out[l].astype(BF16), x_ctx, x_lat, mod[l], norm2_g[l])
        x_ctx, x_lat = conv_ffn(h2, x1, mod[l], ffn_w_up[l].astype(BF16), ffn_conv_w[l].astype(F32),
                                ffn_conv_b[l].astype(F32), ffn_w_down[l].astype(BF16))

    y_prompt = x_ctx.reshape(BATCH, SEQ, d).astype(x_prompt.dtype)
    y_sample = x_lat.reshape(DEC_BATCH, DEC_SEQ, d).astype(x_sample.dtype)
    s5_all = jnp.stack(s5_, axis=1)
    return (y_prompt, y_sample, jnp.stack(ks_, axis=1), jnp.stack(vs_, axis=1), jnp.stack(hs_, axis=1),
            s5_all[..., :S5_N], s5_all[..., S5_N:])
```

```python
import functools

import jax
import jax.numpy as jnp
import numpy as np
from jax import lax
from jax.experimental import pallas as pl
from jax.experimental.pallas import tpu as pltpu

F32 = jnp.float32
BF16 = jnp.bfloat16
HIGHEST = lax.Precision.HIGHEST
LOG2E = 1.4426950408889634

D_MODEL = 2048
BATCH = 32
SEQ = 256
DEPTH = 2
DEC_BATCH = 2
DEC_SEQ = 4096
PAST_LEN = 256
GRID_W = 64
GROUP_W = 512
POOL_WINDOWS = (2, 4, 8, 16)
POOL_GW = 128
HG_HEADS = 4
HG_DK = 128
ATT_HEADS = 8
ATT_KV_HEADS = 2
ATT_GROUP = 4
HEAD_DIM = 64
WINDOW = 128
BLOCK = 128
ROPE_BASE = 10000.0
S5_CH = 16
S5_GROUPS = 32
S5_N = 64
FFN_DIM = 5632
EPS = 1e-6
NEG_INF = -1e30

N_CTX = BATCH * SEQ
N_LAT = DEC_BATCH * DEC_SEQ
N_TOK = N_CTX + N_LAT
IN_COLS = 4352

CB_POOL, CB_HQ, CB_FF, CB_FB, CB_HI, CB_HG, CB_AQ, CB_S5 = range(8)
CB_AK, CB_AV = 32, 33

S5_L = 16
HG_C = 128
VMEM_LIMIT = 56 * 1024 * 1024


def _cparams(sem):
    return pltpu.CompilerParams(dimension_semantics=sem, vmem_limit_bytes=VMEM_LIMIT)


def _dot(a, b):
    return jnp.dot(a, b, preferred_element_type=F32)


def _dot_nt(a, b):
    return lax.dot_general(a, b, (((1,), (1,)), ((), ())), preferred_element_type=F32)


def _dot_tn(a, b):
    return lax.dot_general(a, b, (((0,), (0,)), ((), ())), preferred_element_type=F32)


def _split3(x):
    hi = x.astype(BF16)
    r1 = x - hi.astype(F32)
    mid = r1.astype(BF16)
    lo = (r1 - mid.astype(F32)).astype(BF16)
    return hi, mid, lo


def _dot_exact01(x, b01):
    hi, mid, lo = _split3(x)
    return _dot(hi, b01) + _dot(mid, b01) + _dot(lo, b01)


def _mod_index(row0):
    return jnp.where(row0 < N_CTX, 0, 1 + (row0 - N_CTX) // DEC_SEQ)


def _mod_kernel(cond_ref, w_ref, b_ref, o_ref):
    x = cond_ref[...]
    s = x * jax.nn.sigmoid(x)
    o_ref[...] = _dot(s.astype(BF16), w_ref[...].astype(BF16)) + b_ref[...]


def adaln_mod(cond, ada_w, ada_b, tn=1024):
    depth, d, n6 = ada_w.shape
    out = pl.pallas_call(
        _mod_kernel,
        grid=(depth, n6 // tn),
        in_specs=[pl.BlockSpec((8, d), lambda l, j: (0, 0)),
                  pl.BlockSpec((None, d, tn), lambda l, j: (l, 0, j)),
                  pl.BlockSpec((None, 1, tn), lambda l, j: (l, 0, j))],
        out_specs=pl.BlockSpec((None, 8, tn), lambda l, j: (l, 0, j)),
        out_shape=jax.ShapeDtypeStruct((depth, 8, n6), F32),
        compiler_params=_cparams(("parallel", "parallel")),
        name="adaln_mod",
    )(cond, ada_w, ada_b.reshape(depth, 1, n6))
    return out.reshape(depth, 8, 6, d)


def _split_specs(tm, d, row_tile):
    nc = N_CTX // tm
    return (pl.BlockSpec((tm, d), lambda *g: (jnp.minimum(row_tile(*g), nc - 1), 0)),
            pl.BlockSpec((tm, d), lambda *g: (jnp.maximum(row_tile(*g) - nc, 0), 0)))


def _inproj_kernel(xc_ref, xl_ref, mod_ref, g_ref, w_ref, o_ref, *, tm):
    x = jnp.where(pl.program_id(1) * tm < N_CTX, xc_ref[...], xl_ref[...])
    ms = jnp.mean(x * x, axis=-1, keepdims=True)
    h = x * lax.rsqrt(ms + EPS) * g_ref[...]
    h = h * (1.0 + mod_ref[1:2, :]) + mod_ref[0:1, :]
    o_ref[...] = _dot(h.astype(BF16), w_ref[...])


def in_proj(x_ctx, x_lat, mod_l, g1, w_in, tm=256):
    d = x_ctx.shape[1]
    n = x_ctx.shape[0] + x_lat.shape[0]
    ncols = w_in.shape[1]
    tn = ncols
    xc_spec, xl_spec = _split_specs(tm, d, lambda j, i: i)
    return pl.pallas_call(
        functools.partial(_inproj_kernel, tm=tm),
        grid=(ncols // tn, n // tm),
        in_specs=[xc_spec, xl_spec,
                  pl.BlockSpec((None, 6, d), lambda j, i: (_mod_index(i * tm), 0, 0)),
                  pl.BlockSpec((1, d), lambda j, i: (0, 0)),
                  pl.BlockSpec((d, tn), lambda j, i: (0, j), pipeline_mode=pl.Buffered(1))],
        out_specs=pl.BlockSpec((tm, tn), lambda j, i: (i, j)),
        out_shape=jax.ShapeDtypeStruct((n, ncols), F32),
        compiler_params=_cparams(("parallel", "parallel")),
        name="in_proj",
    )(x_ctx, x_lat, mod_l, g1.reshape(1, d), w_in)


def _pool_kernel(u_ref, up_ref, un_ref, w_ref, sc_ref, o_ref, ext_ref, *, tm):
    i = pl.program_id(0)
    row0 = i * tm
    tseq = jnp.where(row0 < N_CTX, SEQ, DEC_SEQ)
    has_prev = (row0 & (tseq - 1)) != 0
    has_next = ((row0 + tm) & (tseq - 1)) != 0
    ext_ref[0:8, :] = jnp.where(has_prev, up_ref[...], 0.0)
    ext_ref[8:8 + tm, :] = u_ref[...]
    ext_ref[8 + tm:16 + tm, :] = jnp.where(has_next, un_ref[...], 0.0)
    pos = (row0 + lax.broadcasted_iota(jnp.int32, (tm, 1), 0)) & (tseq - 1)
    outs = []
    for gi, w in enumerate(POOL_WINDOWS):
        ls = slice(gi * POOL_GW, (gi + 1) * POOL_GW)
        acc = ext_ref[8 - w // 2:8 - w // 2 + tm, ls]
        for o in range(-w // 2 + 1, w // 2):
            acc = acc + ext_ref[8 + o:8 + o + tm, ls]
        lo = jnp.maximum(pos - w // 2, 0)
        hi = jnp.minimum(pos + w // 2, tseq)
        cnt = (hi - lo).astype(F32)
        pooled = acc / cnt - ext_ref[8:8 + tm, ls]
        outs.append(_dot(pooled.astype(BF16), w_ref[gi]))
    o_ref[...] = (jnp.concatenate(outs, axis=1) * sc_ref[...]).astype(o_ref.dtype)


def pool_mixer(cols, pool_w, pool_scale, tm=256):
    n = cols.shape[0]
    nb8 = n // 8
    return pl.pallas_call(
        functools.partial(_pool_kernel, tm=tm),
        grid=(n // tm,),
        in_specs=[pl.BlockSpec((tm, GROUP_W), lambda i: (i, CB_POOL)),
                  pl.BlockSpec((8, GROUP_W), lambda i: (jnp.maximum(i * (tm // 8) - 1, 0), CB_POOL)),
                  pl.BlockSpec((8, GROUP_W), lambda i: (jnp.minimum((i + 1) * (tm // 8), nb8 - 1), CB_POOL)),
                  pl.BlockSpec((4, POOL_GW, POOL_GW), lambda i: (0, 0, 0)),
                  pl.BlockSpec((1, GROUP_W), lambda i: (0, 0))],
        out_specs=pl.BlockSpec((tm, GROUP_W), lambda i: (i, 0)),
        out_shape=jax.ShapeDtypeStruct((n, GROUP_W), BF16),
        scratch_shapes=[pltpu.VMEM((tm + 16, GROUP_W), F32)],
        compiler_params=_cparams(("parallel",)),
        name="pool_mixer",
    )(cols, cols, cols, pool_w, pool_scale.reshape(1, GROUP_W))


def _block_ref_rows(x, c, m, rev):
    w = x.shape[1]
    if 2 * m >= 8:
        r = m if rev else m - 1
        xr = x.reshape(c // (2 * m), 2 * m, w)[:, r:r + 1, :]
        return jnp.broadcast_to(xr, (c // (2 * m), 2 * m, w)).reshape(c, w)
    x8 = x.reshape(c // 8, 8, w)
    sub = lax.broadcasted_iota(jnp.int32, (c // 8, 8, w), 1)
    out = None
    for kb in range(8 // (2 * m)):
        r = kb * 2 * m + (m if rev else m - 1)
        cand = jnp.broadcast_to(x8[:, r:r + 1, :], (c // 8, 8, w))
        out = cand if out is None else jnp.where(sub >= kb * 2 * m, cand, out)
    return out.reshape(c, w)


def _hgrn_chunk(q, fx, v, log_l, log1m_l, st_ref, rev, c):
    ls = jnp.minimum(fx, 0.0) - jnp.log(1.0 + jnp.exp(-jnp.abs(fx)))
    bb = log1m_l + ls
    logf = jnp.maximum(log_l, bb) + jnp.log(1.0 + jnp.exp(-jnp.abs(log_l - bb)))
    kk = 1.0 - jnp.exp(logf)
    row = lax.broadcasted_iota(jnp.int32, (c, c), 0)
    col = lax.broadcasted_iota(jnp.int32, (c, c), 1)
    causal = (row <= col) if rev else (row >= col)
    tri = jnp.where(causal, 1.0, 0.0).astype(BF16)
    hi, mid, lo = _split3(logf)
    b = (_dot(tri, hi) + _dot(tri, mid) + _dot(tri, lo)) * LOG2E
    b_last = b[0:1] if rev else b[c - 1:c]
    qt = (q * jnp.exp2(b)).astype(BF16)
    kd = (kk * jnp.exp2(b_last - b)).astype(BF16)
    dl = jnp.exp2(b_last)
    xor = row ^ col
    tbit = lax.broadcasted_iota(jnp.int32, (c, 1), 0)
    levels = []
    m = c // 2
    while m >= 1:
        wgt = jnp.exp2(-jnp.abs(b - _block_ref_rows(b, c, m, rev)))
        query_side = ((tbit & m) == 0) if rev else ((tbit & m) != 0)
        levels.append(((jnp.where(query_side, q, kk) * wgt).astype(BF16), (xor >> (m.bit_length() - 1)) == 1))
        m //= 2
    qb = q.astype(BF16)
    kb = kk.astype(BF16)
    vb = v.astype(BF16)
    outs = []
    for h in range(HG_HEADS):
        sl = slice(h * HG_DK, (h + 1) * HG_DK)
        att = jnp.where(xor == 0, _dot_nt(qb[:, sl], kb[:, sl]), 0.0)
        for z, msk in levels:
            att = att + jnp.where(msk, _dot_nt(z[:, sl], z[:, sl]), 0.0)
        att = jnp.where(causal, att, 0.0)
        intra = _dot(att.astype(BF16), vb[:, sl])
        st = st_ref[h]
        inter = _dot_nt(qt[:, sl], st.astype(BF16))
        st_ref[h] = st * dl[:, sl] + _dot_tn(vb[:, sl], kd[:, sl])
        outs.append(intra + inter)
    return jnp.concatenate(outs, axis=1)


def _hgrn_kernel(qf_ref, ff_ref, vf_ref, qb_ref, fb_ref, vb_ref, lp_ref, s0_ref, *rest, c, nchunks):
    of_ref, ob_ref, sfin_ref, st_ref = rest[-4:]
    ci = pl.program_id(1)

    @pl.when(ci == 0)
    def _():
        st_ref[...] = s0_ref[...]

    of_ref[...] = _hgrn_chunk(qf_ref[...], ff_ref[...], vf_ref[...], lp_ref[0, 0:1, :], lp_ref[0, 1:2, :],
                              st_ref.at[0], False, c)
    ob_ref[...] = _hgrn_chunk(qb_ref[...], fb_ref[...], vb_ref[...], lp_ref[1, 0:1, :], lp_ref[1, 1:2, :],
                              st_ref.at[1], True, c)

    @pl.when(ci == nchunks - 1)
    def _():
        for d in range(2):
            for h in range(HG_HEADS):
                sfin_ref[d, h] = st_ref[d, h].T


def hgrn_scan(cols, lp, s0_t, row_off, nseq, t, prev=None, c=HG_C):
    n = cols.shape[0]
    nchunks = t // c
    base = row_off // c
    extra = [] if prev is None else list(prev)
    any_spec = pl.BlockSpec(memory_space=pl.ANY)

    def fwd(cb):
        return pl.BlockSpec((c, GROUP_W), lambda s, i: (base + s * nchunks + i, cb))

    def bwd(cb):
        return pl.BlockSpec((c, GROUP_W), lambda s, i: (base + s * nchunks + nchunks - 1 - i, cb))

    st_spec = pl.BlockSpec((None, 2, HG_HEADS, HG_DK, HG_DK), lambda s, i: (s, 0, 0, 0, 0))
    return pl.pallas_call(
        functools.partial(_hgrn_kernel, c=c, nchunks=nchunks),
        grid=(nseq, nchunks),
        in_specs=[fwd(CB_HQ), fwd(CB_FF), fwd(CB_HI), bwd(CB_HQ), bwd(CB_FB), bwd(CB_HI),
                  pl.BlockSpec((2, 2, GROUP_W), lambda s, i: (0, 0, 0)), st_spec] + [any_spec] * len(extra),
        out_specs=[fwd(0), bwd(0), st_spec],
        out_shape=[jax.ShapeDtypeStruct((n, GROUP_W), F32),
                   jax.ShapeDtypeStruct((n, GROUP_W), F32),
                   jax.ShapeDtypeStruct((nseq, 2, HG_HEADS, HG_DK, HG_DK), F32)],
        input_output_aliases={8 + k: k for k in range(len(extra))},
        scratch_shapes=[pltpu.VMEM((2, HG_HEADS, HG_DK, HG_DK), F32)],
        compiler_params=_cparams(("parallel", "arbitrary")),
        name="hgrn_scan",
    )(cols, cols, cols, cols, cols, cols, lp, s0_t, *extra)


def _hgrn_finish_kernel(of_ref, ob_ref, g_ref, ng_ref, o_ref):
    o = of_ref[...] + ob_ref[...]
    g = g_ref[...]
    gate = g * jax.nn.sigmoid(g)
    outs = []
    for h in range(HG_HEADS):
        sl = slice(h * HG_DK, (h + 1) * HG_DK)
        oh = o[:, sl]
        ms = jnp.mean(oh * oh, axis=-1, keepdims=True)
        outs.append(oh * lax.rsqrt(ms + EPS) * ng_ref[...])
    o_ref[...] = (jnp.concatenate(outs, axis=1) * gate).astype(o_ref.dtype)


def hgrn_finish(o_f, o_b, cols, norm_g, tm=1024):
    n = o_f.shape[0]
    return pl.pallas_call(
        _hgrn_finish_kernel,
        grid=(n // tm,),
        in_specs=[pl.BlockSpec((tm, GROUP_W), lambda i: (i, 0)),
                  pl.BlockSpec((tm, GROUP_W), lambda i: (i, 0)),
                  pl.BlockSpec((tm, GROUP_W), lambda i: (i, CB_HG)),
                  pl.BlockSpec((1, HG_DK), lambda i: (0, 0))],
        out_specs=pl.BlockSpec((tm, GROUP_W), lambda i: (i, 0)),
        out_shape=jax.ShapeDtypeStruct((n, GROUP_W), BF16),
        compiler_params=_cparams(("parallel",)),
        name="hgrn_finish",
    )(o_f, o_b, cols, norm_g.reshape(1, HG_DK))


def _swap16(x):
    w = x.shape[1]
    lane = lax.broadcasted_iota(jnp.int32, x.shape, 1)
    return jnp.where((lane & 31) < 16, pltpu.roll(x, w - 16, 1), pltpu.roll(x, 16, 1))


def _qkprep_kernel(q_ref, k_ref, cos_ref, sin_ref, gq_ref, gk_ref, bq_ref, qo_ref, ko_ref):
    cos = cos_ref[...]
    sin = sin_ref[...]
    q = q_ref[...]
    msq = _dot_exact01(q * q, bq_ref[...]) * (1.0 / HEAD_DIM)
    qn = q * lax.rsqrt(msq + EPS) * gq_ref[...]
    cos4 = jnp.concatenate([cos] * 4, axis=1)
    sin4 = jnp.concatenate([sin] * 4, axis=1)
    qr = qn * cos4 + _swap16(qn) * sin4
    qo_ref[...] = (qr * (HEAD_DIM ** -0.5)).astype(qo_ref.dtype)
    k = k_ref[...]
    msk = _dot_exact01(k * k, bq_ref[0:128, 0:128]) * (1.0 / HEAD_DIM)
    kn = k * lax.rsqrt(msk + EPS) * gk_ref[...]
    ko_ref[...] = kn * cos + _swap16(kn) * sin


def qk_prep(cols, cos_t, sin_t, q_norm_g, k_norm_g, tm=512):
    n = cols.shape[0]
    head = np.arange(GROUP_W) // HEAD_DIM
    bones = jnp.asarray((head[:, None] == head[None, :]).astype(np.float32), BF16)
    return pl.pallas_call(
        _qkprep_kernel,
        grid=(n // tm,),
        in_specs=[pl.BlockSpec((tm, GROUP_W), lambda i: (i, CB_AQ)),
                  pl.BlockSpec((tm, 128), lambda i: (i, CB_AK)),
                  pl.BlockSpec((tm, 128), lambda i: (i, 0)),
                  pl.BlockSpec((tm, 128), lambda i: (i, 0)),
                  pl.BlockSpec((1, GROUP_W), lambda i: (0, 0)),
                  pl.BlockSpec((1, 128), lambda i: (0, 0)),
                  pl.BlockSpec((GROUP_W, GROUP_W), lambda i: (0, 0))],
        out_specs=[pl.BlockSpec((tm, GROUP_W), lambda i: (i, 0)),
                   pl.BlockSpec((tm, 128), lambda i: (i, 0))],
        out_shape=[jax.ShapeDtypeStruct((n, GROUP_W), BF16),
                   jax.ShapeDtypeStruct((n, 128), F32)],
        compiler_params=_cparams(("parallel",)),
        name="qk_prep",
    )(cols, cols, cos_t, sin_t, jnp.tile(q_norm_g, ATT_HEADS).reshape(1, GROUP_W),
      jnp.tile(k_norm_g, ATT_KV_HEADS).reshape(1, 128), bones)


def _stack_heads(q, kh, rows):
    return jnp.concatenate(
        [q[:, (kh * ATT_GROUP + g) * HEAD_DIM:(kh * ATT_GROUP + g + 1) * HEAD_DIM] for g in range(ATT_GROUP)], axis=0)


def _sink_col(sink_ref, kh, rows):
    return jnp.concatenate(
        [jnp.full((rows, 1), sink_ref[kh * ATT_GROUP + g], F32) for g in range(ATT_GROUP)], axis=0)


def _attn_ctx_kernel(sink_ref, q_ref, k_ref, v_ref, o_ref, *, t, spb):
    for sq in range(spb):
        rs = slice(sq * t, (sq + 1) * t)
        q = q_ref[rs, :]
        k = k_ref[rs, :].astype(BF16)
        v = v_ref[rs, :].astype(BF16)
        pieces = [None] * ATT_HEADS
        for kh in range(ATT_KV_HEADS):
            hs = slice(kh * HEAD_DIM, (kh + 1) * HEAD_DIM)
            s = _dot_nt(_stack_heads(q, kh, t), k[:, hs])
            sk = _sink_col(sink_ref, kh, t)
            mx = jnp.maximum(jnp.max(s, axis=1, keepdims=True), sk)
            p = jnp.exp(s - mx)
            den = jnp.sum(p, axis=1, keepdims=True) + jnp.exp(sk - mx)
            o = _dot(p.astype(BF16), v[:, hs]) / den
            for g in range(ATT_GROUP):
                pieces[kh * ATT_GROUP + g] = o[g * t:(g + 1) * t]
        o_ref[rs, :] = jnp.concatenate(pieces, axis=1).astype(o_ref.dtype)


def attend_ctx(qs, kn, cols, sink, nseq, t, spb=2):
    return pl.pallas_call(
        functools.partial(_attn_ctx_kernel, t=t, spb=spb),
        grid=(nseq // spb,),
        in_specs=[pl.BlockSpec(memory_space=pltpu.SMEM),
                  pl.BlockSpec((spb * t, GROUP_W), lambda s: (s, 0)),
                  pl.BlockSpec((spb * t, 128), lambda s: (s, 0)),
                  pl.BlockSpec((spb * t, 128), lambda s: (s, CB_AV))],
        out_specs=pl.BlockSpec((spb * t, GROUP_W), lambda s: (s, 0)),
        out_shape=jax.ShapeDtypeStruct((qs.shape[0], GROUP_W), BF16),
        compiler_params=_cparams(("parallel",)),
        name="attend_ctx",
    )(sink, qs, kn, cols)


def _attn_lat_kernel(sink_ref, q_ref, kp_ref, kc_ref, kn_ref, vp_ref, vc_ref, vn_ref, ck_ref, cv_ref, prev_ref,
                     o_ref, *, t, blk):
    del prev_ref
    i = pl.program_id(1)
    q = q_ref[...]
    kl = jnp.concatenate([kp_ref[...], kc_ref[...], kn_ref[...]], axis=0).astype(BF16)
    vl = jnp.concatenate([vp_ref[...], vc_ref[...], vn_ref[...]], axis=0).astype(BF16)
    ck = ck_ref[...].astype(BF16)
    cv = cv_ref[...].astype(BF16)
    rows = ATT_GROUP * blk
    qpos = i * blk + (lax.broadcasted_iota(jnp.int32, (rows, 3 * blk), 0) & (blk - 1))
    kpos = (i - 1) * blk + lax.broadcasted_iota(jnp.int32, (rows, 3 * blk), 1)
    valid = (kpos >= 0) & (kpos < t) & (jnp.abs(qpos - kpos) <= WINDOW)
    pieces = [None] * ATT_HEADS
    for kh in range(ATT_KV_HEADS):
        hs = slice(kh * HEAD_DIM, (kh + 1) * HEAD_DIM)
        qh = _stack_heads(q, kh, blk)
        s_loc = jnp.where(valid, _dot_nt(qh, kl[:, hs]), NEG_INF)
        s_ctx = _dot_nt(qh, ck[:, hs])
        sk = _sink_col(sink_ref, kh, blk)
        mx = jnp.maximum(jnp.maximum(jnp.max(s_loc, axis=1, keepdims=True),
                                     jnp.max(s_ctx, axis=1, keepdims=True)), sk)
        p_loc = jnp.exp(s_loc - mx)
        p_ctx = jnp.exp(s_ctx - mx)
        den = (jnp.sum(p_loc, axis=1, keepdims=True) + jnp.sum(p_ctx, axis=1, keepdims=True)
               + jnp.exp(sk - mx))
        o = (_dot(p_loc.astype(BF16), vl[:, hs]) + _dot(p_ctx.astype(BF16), cv[:, hs])) / den
        for g in range(ATT_GROUP):
            pieces[kh * ATT_GROUP + g] = o[g * blk:(g + 1) * blk]
    o_ref[...] = jnp.concatenate(pieces, axis=1).astype(o_ref.dtype)


def attend_lat(qs, kn, cols, cache_k, cache_v, sink, prev, row_off, nseq, t, blk=2 * WINDOW):
    nb = t // blk
    base = row_off // blk
    lctx = cache_k.shape[1]

    def nbr(cb, off):
        return pl.BlockSpec((blk, 128), lambda b, i: (base + b * nb + jnp.clip(i + off, 0, nb - 1), cb))

    return pl.pallas_call(
        functools.partial(_attn_lat_kernel, t=t, blk=blk),
        grid=(nseq, nb),
        in_specs=[pl.BlockSpec(memory_space=pltpu.SMEM),
                  pl.BlockSpec((blk, GROUP_W), lambda b, i: (base + b * nb + i, 0)),
                  nbr(0, -1), nbr(0, 0), nbr(0, 1),
                  nbr(CB_AV, -1), nbr(CB_AV, 0), nbr(CB_AV, 1),
                  pl.BlockSpec((None, lctx, 128), lambda b, i: (b, 0, 0)),
                  pl.BlockSpec((None, lctx, 128), lambda b, i: (b, 0, 0)),
                  pl.BlockSpec(memory_space=pl.ANY)],
        out_specs=pl.BlockSpec((blk, GROUP_W), lambda b, i: (base + b * nb + i, 0)),
        out_shape=jax.ShapeDtypeStruct(prev.shape, BF16),
        input_output_aliases={10: 0},
        compiler_params=_cparams(("parallel", "parallel")),
        name="attend_lat",
    )(sink, qs, kn, kn, kn, cols, cols, cols, cache_k, cache_v, prev)


def s5_operators(a_re, a_im, log_dt, b_re, b_im, c_re, c_im):
    L = S5_L
    dt = jnp.exp(log_dt.astype(F32))[..., None]
    are, aim = a_re.astype(F32), a_im.astype(F32)
    adr, adi = (are * dt)[..., None, :], (aim * dt)[..., None, :]

    def lam_pow(p):
        p = jnp.asarray(p, F32)[None, :, None, :, None]
        mag = jnp.exp(adr * p)
        return mag * jnp.cos(adi * p), mag * jnp.sin(adi * p)

    ar = np.arange(L)
    one = np.ones((2, 1))
    lre, lim = lam_pow(one)
    lre, lim = lre[..., 0, :], lim[..., 0, :]
    den = are * are + aim * aim
    fre = ((lre - 1.0) * are + lim * aim) / den
    fim = (lim * are - (lre - 1.0) * aim) / den
    bre = fre[..., None] * b_re - fim[..., None] * b_im
    bim = fre[..., None] * b_im + fim[..., None] * b_re
    bret, bimt = jnp.swapaxes(bre, -1, -2), jnp.swapaxes(bim, -1, -2)
    cre, cim = c_re.astype(F32)[..., None, :, :], c_im.astype(F32)[..., None, :, :]
    ein = functools.partial(jnp.einsum, precision=HIGHEST)

    def c_lam(p):
        pr, pi = lam_pow(p)
        pr, pi = pr[..., None, :], pi[..., None, :]
        return cre * pr - cim * pi, cre * pi + cim * pr

    cpr, cpi = c_lam(one * ar)
    ktau = ein('ldgtcn,ldgne->ldgtce', cpr, bre) - ein('ldgtcn,ldgne->ldgtce', cpi, bim)
    lag = ar[None, :] - ar[:, None]
    shift = np.stack([(lag[None] == ar[:, None, None]), (-lag[None] == ar[:, None, None])]).astype(np.float32)
    kmat = ein('dxst,ldgxce->ldgsetc', shift, ktau).reshape(ktau.shape[:3] + (L * S5_CH, L * S5_CH))
    pr, pi = lam_pow(np.stack([L - 1 - ar, ar]))
    pr, pi = pr[..., None, :], pi[..., None, :]
    wr = pr * bret[..., None, :, :] - pi * bimt[..., None, :, :]
    wi = pr * bimt[..., None, :, :] + pi * bret[..., None, :, :]
    wst = jnp.concatenate([wr, wi], axis=-1).reshape(ktau.shape[:3] + (L * S5_CH, 2 * S5_N))
    orr, oii = c_lam(np.stack([ar + 1, L - ar]))
    wout = jnp.concatenate([jnp.moveaxis(orr, -1, -3), -jnp.moveaxis(oii, -1, -3)], axis=-3)
    wout = wout.reshape(ktau.shape[:3] + (2 * S5_N, L * S5_CH))
    jre, jim = lam_pow(one * (L * 2.0 ** np.arange(9)))
    lam = jnp.stack([jnp.concatenate([jre, jre], axis=-1), jnp.concatenate([-jim, jim], axis=-1)], axis=-2)
    return kmat.astype(BF16), wst.astype(BF16), wout.astype(BF16), lam


S5_BUNDLE = 128 // S5_CH


def _s5_group(gl, u, km_ref, ws_ref, wo_ref, lam_ref, h0_ref, hfin_ref, seglen, nseq):
    r = nseq * seglen
    rowi = lax.broadcasted_iota(jnp.int32, (r, 1), 0)
    cl = rowi & (seglen - 1)

    def cmul(d, j, x):
        return lam_ref[d, gl, j, 0:1, :] * x + lam_ref[d, gl, j, 1:2, :] * pltpu.roll(x, S5_N, 1)

    y = None
    for d in range(2):
        edge = 0 if d == 0 else seglen - 1
        h0rows = jnp.zeros((r, 2 * S5_N), F32)
        x = _dot(u, ws_ref[d, gl])
        if h0_ref is not None:
            for s in range(nseq):
                h0rows = jnp.where(rowi == s * seglen + edge, h0_ref[gl, d, s:s + 1, :], h0rows)
            x = x + cmul(d, 0, h0rows)
        sh, j = 1, 0
        while sh < seglen:
            if d == 0:
                xs, ok = pltpu.roll(x, sh, 0), cl >= sh
            else:
                xs, ok = pltpu.roll(x, r - sh, 0), cl < seglen - sh
            x = x + jnp.where(ok, cmul(d, j, xs), 0.0)
            sh, j = sh * 2, j + 1
        if hfin_ref is not None:
            hfin_ref[gl, d] = x
        if d == 0:
            hin = jnp.where(cl >= 1, pltpu.roll(x, 1, 0), h0rows)
        else:
            hin = jnp.where(cl < seglen - 1, pltpu.roll(x, r - 1, 0), h0rows)
        yd = _dot(u, km_ref[d, gl]) + _dot(hin.astype(BF16), wo_ref[d, gl])
        y = yd if y is None else y + yd
    return y


def _s5_kernel(*refs, seglen, nseq, has_h0, has_prev, has_fin):
    u_ref, km_ref, ws_ref, wo_ref, lam_ref = refs[:5]
    pos = 5
    h0_ref = refs[pos] if has_h0 else None
    pos += has_h0 + has_prev
    y_ref = refs[pos]
    hfin_ref = refs[pos + 1] if has_fin else None
    ycat_ref, rin_ref, rout_ref = refs[-3:]
    lane_blk = lax.broadcasted_iota(jnp.int32, (1, 128), 1) // S5_CH
    nv = S5_L // S5_BUNDLE
    nb = S5_BUNDLE

    def merge(pick):
        acc = pick(0)
        for b in range(1, nb):
            acc = jnp.where(lane_blk == b, pick(b), acc)
        return acc

    for v in range(nv):
        for s in range(nb):
            m = merge(lambda g: u_ref[v * nb + (g + s) % nb])
            rin_ref[v, s] = pltpu.roll(m, s * S5_CH, 1) if s else m

    def body(gl, carry):
        halves = [merge(lambda k: rin_ref[v, (k - gl) & (nb - 1)]) for v in range(nv)]
        u = jnp.concatenate(halves, axis=1).astype(BF16)
        ycat_ref[gl] = _s5_group(gl, u, km_ref, ws_ref, wo_ref, lam_ref, h0_ref, hfin_ref, seglen, nseq)
        return carry

    lax.fori_loop(0, nb, body, 0)
    for v in range(nv):
        for s in range(nb):
            m = merge(lambda k: ycat_ref[(k + s) % nb, :, v * 128:(v + 1) * 128])
            rout_ref[s] = pltpu.roll(m, s * S5_CH, 1) if s else m
        for k in range(nb):
            y_ref[v * nb + k] = merge(lambda g: rout_ref[(g - k) % nb])


S5_TILE = 512


def _s5_perm():
    p = np.arange(S5_TILE)
    src = (p % (S5_TILE // S5_L)) * S5_L + p // (S5_TILE // S5_L)
    return (src[:, None] == np.arange(S5_TILE)[None, :]).astype(np.float32)


def _permute_rows(p, x, transposed):
    dot = _dot_tn if transposed else _dot
    hi, mid, lo = _split3(x)
    return dot(p, hi) + dot(p, mid) + dot(p, lo)


def _s5_gather_kernel(u_ref, p_ref, o_ref):
    o_ref[...] = _permute_rows(p_ref[...], u_ref[...], False).reshape(o_ref.shape)


def s5_gather(cols):
    n = cols.shape[0]
    cpt = S5_TILE // S5_L
    return pl.pallas_call(
        _s5_gather_kernel,
        grid=(n // S5_TILE,),
        in_specs=[pl.BlockSpec((S5_TILE, GROUP_W), lambda i: (i, CB_S5)),
                  pl.BlockSpec((S5_TILE, S5_TILE), lambda i: (0, 0))],
        out_specs=pl.BlockSpec((S5_L, cpt, GROUP_W), lambda i: (0, i, 0)),
        out_shape=jax.ShapeDtypeStruct((S5_L, n // S5_L, GROUP_W), F32),
        compiler_params=_cparams(("parallel",)),
        name="s5_gather",
    )(cols, jnp.asarray(_s5_perm(), BF16))


def s5_scan(ud, ops, layer, h0, prev, row_off, nseq, seglen):
    kmat, wst, wout, lam = ops
    nchunks = ud.shape[1]
    r = nseq * seglen
    rb = row_off // (r * S5_L)
    assert row_off % (r * S5_L) == 0
    lw = S5_L * S5_CH
    nb = S5_BUNDLE
    in_specs = [pl.BlockSpec((S5_L, r, 128), lambda j: (0, rb, j)),
                pl.BlockSpec((None, 2, nb, lw, lw), lambda j: (layer, 0, j, 0, 0)),
                pl.BlockSpec((None, 2, nb, lw, 2 * S5_N), lambda j: (layer, 0, j, 0, 0)),
                pl.BlockSpec((None, 2, nb, 2 * S5_N, lw), lambda j: (layer, 0, j, 0, 0)),
                pl.BlockSpec((None, 2, nb, 9, 2, 2 * S5_N), lambda j: (layer, 0, j, 0, 0, 0))]
    args = [ud, kmat, wst, wout, lam]
    if h0 is not None:
        in_specs.append(pl.BlockSpec((nb, 2, h0.shape[2], 2 * S5_N), lambda j: (j, 0, 0, 0)))
        args.append(h0)
    aliases = {}
    if prev is not None:
        aliases = {len(args): 0}
        in_specs.append(pl.BlockSpec(memory_space=pl.ANY))
        args.append(prev)
    out_specs = [pl.BlockSpec((S5_L, r, 128), lambda j: (0, rb, j))]
    out_shape = [jax.ShapeDtypeStruct((S5_L, nchunks, GROUP_W), F32)]
    has_fin = prev is None
    if has_fin:
        out_specs.append(pl.BlockSpec((nb, 2, r, 2 * S5_N), lambda j: (j, 0, 0, 0)))
        out_shape.append(jax.ShapeDtypeStruct((S5_GROUPS, 2, r, 2 * S5_N), F32))
    return pl.pallas_call(
        functools.partial(_s5_kernel, seglen=seglen, nseq=nseq, has_h0=h0 is not None,
                          has_prev=prev is not None, has_fin=has_fin),
        grid=(S5_GROUPS // nb,),
        in_specs=in_specs,
        out_specs=out_specs,
        out_shape=out_shape,
        input_output_aliases=aliases,
        scratch_shapes=[pltpu.VMEM((nb, r, lw), F32), pltpu.VMEM((S5_L // nb, nb, r, 128), F32),
                        pltpu.VMEM((nb, r, 128), F32)],
        compiler_params=_cparams(("parallel",)),
        name="s5_scan",
    )(*args)


def _s5_post_kernel(u_ref, y_ref, p_ref, d_ref, w_ref, o_ref):
    y = _permute_rows(p_ref[...], y_ref[...].reshape(S5_TILE, GROUP_W), True)
    out = d_ref[...] * u_ref[...] + y
    yy = jax.nn.gelu(out)
    zz = _dot(yy.astype(BF16), w_ref[...])
    o_ref[...] = (zz[:, :GROUP_W] * jax.nn.sigmoid(zz[:, GROUP_W:])).astype(o_ref.dtype)


def s5_post(cols, y, s5_d, w_glu):
    n = cols.shape[0]
    cpt = S5_TILE // S5_L
    return pl.pallas_call(
        _s5_post_kernel,
        grid=(n // S5_TILE,),
        in_specs=[pl.BlockSpec((S5_TILE, GROUP_W), lambda i: (i, CB_S5)),
                  pl.BlockSpec((S5_L, cpt, GROUP_W), lambda i: (0, i, 0)),
                  pl.BlockSpec((S5_TILE, S5_TILE), lambda i: (0, 0)),
                  pl.BlockSpec((1, GROUP_W), lambda i: (0, 0)),
                  pl.BlockSpec((GROUP_W, 2 * GROUP_W), lambda i: (0, 0))],
        out_specs=pl.BlockSpec((S5_TILE, GROUP_W), lambda i: (i, 0)),
        out_shape=jax.ShapeDtypeStruct((n, GROUP_W), BF16),
        compiler_params=_cparams(("parallel",)),
        name="s5_post",
    )(cols, y, jnp.asarray(_s5_perm(), BF16), s5_d.reshape(1, GROUP_W), w_glu)


def _outproj_kernel(p0_ref, p1_ref, p2_ref, p3_ref, w_ref, xc_ref, xl_ref, mod_ref, g_ref, x1_ref, h2_ref, *, tm):
    y = _dot(p0_ref[...], w_ref[0:GROUP_W, :])
    y = y + _dot(p1_ref[...], w_ref[GROUP_W:2 * GROUP_W, :])
    y = y + _dot(p2_ref[...], w_ref[2 * GROUP_W:3 * GROUP_W, :])
    y = y + _dot(p3_ref[...], w_ref[3 * GROUP_W:4 * GROUP_W, :])
    x = jnp.where(pl.program_id(0) * tm < N_CTX, xc_ref[...], xl_ref[...])
    x1 = x + mod_ref[2:3, :] * y
    x1_ref[...] = x1
    ms = jnp.mean(x1 * x1, axis=-1, keepdims=True)
    h = x1 * lax.rsqrt(ms + EPS) * g_ref[...]
    h2_ref[...] = (h * (1.0 + mod_ref[4:5, :]) + mod_ref[3:4, :]).astype(h2_ref.dtype)


def out_proj(parts, w_out, x_ctx, x_lat, mod_l, g2, tm=512):
    d = x_ctx.shape[1]
    n = x_ctx.shape[0] + x_lat.shape[0]
    part_spec = pl.BlockSpec((tm, GROUP_W), lambda i: (i, 0))
    xc_spec, xl_spec = _split_specs(tm, d, lambda i: i)
    return pl.pallas_call(
        functools.partial(_outproj_kernel, tm=tm),
        grid=(n // tm,),
        in_specs=[part_spec, part_spec, part_spec, part_spec,
                  pl.BlockSpec((4 * GROUP_W, d), lambda i: (0, 0)),
                  xc_spec, xl_spec,
                  pl.BlockSpec((None, 6, d), lambda i: (_mod_index(i * tm), 0, 0)),
                  pl.BlockSpec((1, d), lambda i: (0, 0))],
        out_specs=[pl.BlockSpec((tm, d), lambda i: (i, 0)),
                   pl.BlockSpec((tm, d), lambda i: (i, 0))],
        out_shape=[jax.ShapeDtypeStruct((n, d), F32),
                   jax.ShapeDtypeStruct((n, d), BF16)],
        compiler_params=_cparams(("parallel",)),
        name="out_proj",
    )(*parts, w_out, x_ctx, x_lat, mod_l, g2.reshape(1, d))


FFN_HALO = 16


def _ffn_kernel(h_ref, hp_ref, hn_ref, wa_ref, wb_ref, wd_ref, cw_ref, cb_ref, x1_ref, mod_ref, oc_ref, ol_ref,
                ext_ref, aext_ref, acc_ref, *, tm, nf):
    i = pl.program_id(0)
    j = pl.program_id(1)

    @pl.when(j == 0)
    def _():
        ext_ref[0:FFN_HALO, :] = hp_ref[...]
        ext_ref[FFN_HALO:FFN_HALO + tm, :] = h_ref[...]
        ext_ref[FFN_HALO + tm:2 * FFN_HALO + tm, :] = hn_ref[...]
        acc_ref[...] = jnp.zeros_like(acc_ref)

    aext_ref[...] = _dot(ext_ref[...], wa_ref[...])
    b = _dot(h_ref[...], wb_ref[...])
    row0 = i * tm
    tseq = jnp.where(row0 < N_CTX, SEQ, DEC_SEQ)
    pos = (row0 + lax.broadcasted_iota(jnp.int32, (tm, 1), 0)) & (tseq - 1)
    a_prev = jnp.where(pos == 0, 0.0, aext_ref[FFN_HALO - 1:FFN_HALO - 1 + tm, :])
    a_mid = aext_ref[FFN_HALO:FFN_HALO + tm, :]
    a_next = jnp.where(pos == tseq - 1, 0.0, aext_ref[FFN_HALO + 1:FFN_HALO + 1 + tm, :])
    a = a_prev * cw_ref[0:1, :] + a_mid * cw_ref[1:2, :] + a_next * cw_ref[2:3, :] + cb_ref[...]
    act = a * jax.nn.sigmoid(a) * b
    acc_ref[...] += _dot(act.astype(BF16), wd_ref[...])

    @pl.when((j == nf - 1) & (row0 < N_CTX))
    def _():
        oc_ref[...] = x1_ref[...] + mod_ref[5:6, :] * acc_ref[...]

    @pl.when((j == nf - 1) & (row0 >= N_CTX))
    def _():
        ol_ref[...] = x1_ref[...] + mod_ref[5:6, :] * acc_ref[...]


def conv_ffn(h2, x1, mod_l, w_up, conv_w, conv_b, w_down, tm=512, tf=512):
    n, d = x1.shape
    f = w_down.shape[0]
    nf = f // tf
    nhb = n // FFN_HALO
    oc_spec, ol_spec = _split_specs(tm, d, lambda i, j: i)
    return pl.pallas_call(
        functools.partial(_ffn_kernel, tm=tm, nf=nf),
        grid=(n // tm, nf),
        in_specs=[pl.BlockSpec((tm, d), lambda i, j: (i, 0)),
                  pl.BlockSpec((FFN_HALO, d), lambda i, j: (jnp.maximum(i * (tm // FFN_HALO) - 1, 0), 0)),
                  pl.BlockSpec((FFN_HALO, d), lambda i, j: (jnp.minimum((i + 1) * (tm // FFN_HALO), nhb - 1), 0)),
                  pl.BlockSpec((d, tf), lambda i, j: (0, j)),
                  pl.BlockSpec((d, tf), lambda i, j: (0, nf + j)),
                  pl.BlockSpec((tf, d), lambda i, j: (j, 0)),
                  pl.BlockSpec((3, tf), lambda i, j: (0, j)),
                  pl.BlockSpec((1, tf), lambda i, j: (0, j)),
                  pl.BlockSpec((tm, d), lambda i, j: (i, 0)),
                  pl.BlockSpec((None, 6, d), lambda i, j: (_mod_index(i * tm), 0, 0))],
        out_specs=[oc_spec, ol_spec],
        out_shape=[jax.ShapeDtypeStruct((N_CTX, d), F32), jax.ShapeDtypeStruct((n - N_CTX, d), F32)],
        scratch_shapes=[pltpu.VMEM((tm + 2 * FFN_HALO, d), BF16),
                        pltpu.VMEM((tm + 2 * FFN_HALO, tf), F32),
                        pltpu.VMEM((tm, d), F32)],
        compiler_params=_cparams(("arbitrary", "arbitrary")),
        name="conv_ffn",
    )(h2, h2, h2, w_up, w_up, w_down, conv_w, conv_b.reshape(1, f), x1, mod_l)


def _rope_tables():
    half = HEAD_DIM // 2
    nf = half // 2
    inv = ROPE_BASE ** (-jnp.arange(nf, dtype=F32) / nf)
    t = jnp.arange(DEC_SEQ)
    rows = (t // GRID_W).astype(F32)
    cols = (t % GRID_W).astype(F32)
    ang = jnp.concatenate([rows[:, None] * inv[None]] * 2 + [cols[:, None] * inv[None]] * 2, axis=1)
    sign = jnp.tile(jnp.concatenate([-jnp.ones(nf, F32), jnp.ones(nf, F32)]), 2)
    cos = jnp.tile(jnp.cos(ang), (DEC_BATCH, ATT_KV_HEADS))
    sin = jnp.tile(jnp.sin(ang) * sign, (DEC_BATCH, ATT_KV_HEADS))
    cos = jnp.concatenate([jnp.ones((N_CTX, 128), F32), cos], axis=0)
    sin = jnp.concatenate([jnp.zeros((N_CTX, 128), F32), sin], axis=0)
    return cos, sin


def kernel(x_prompt, x_sample, cache_k, cache_v, state_hgrn, state_s5_re, state_s5_im, c, c_ctx, norm1_g, norm2_g, ada_w, ada_b, w_in, w_out, pool_w, pool_scale, hg_lb_raw, hg_norm_g, q_norm_g, k_norm_g, att_sink, s5_a_re, s5_a_im, s5_log_dt, s5_b_re, s5_b_im, s5_c_re, s5_c_im, s5_d, s5_w_glu, ffn_w_up, ffn_conv_w, ffn_conv_b, ffn_w_down):
    d = D_MODEL
    x_ctx = x_prompt.reshape(N_CTX, d).astype(F32)
    x_lat = x_sample.reshape(N_LAT, d).astype(F32)

    cond = jnp.concatenate([c_ctx[None].astype(F32), c.astype(F32), jnp.zeros((8 - 1 - DEC_BATCH, d), F32)], axis=0)
    mod = adaln_mod(cond, ada_w, ada_b)

    lb_cum = jnp.cumsum(jax.nn.softmax(hg_lb_raw.astype(F32), axis=0), axis=0)
    hg_lb = lb_cum - lb_cum[:1]
    lp = jnp.stack([jnp.log(hg_lb), jnp.log1p(-hg_lb)], axis=2)

    cos_t, sin_t = _rope_tables()
    s0_ctx = jnp.zeros((BATCH, 2, HG_HEADS, HG_DK, HG_DK), F32)
    s5_ops = s5_operators(s5_a_re, s5_a_im, s5_log_dt, s5_b_re, s5_b_im, s5_c_re, s5_c_im)

    ks_, vs_, hs_, s5_ = [], [], [], []
    for l in range(DEPTH):
        w_in_l = w_in[l]
        w_in_p = jnp.concatenate([w_in_l[:, :3584], w_in_l[:, 3840:4352], w_in_l[:, 3584:3840]], axis=1).astype(BF16)
        cols = in_proj(x_ctx, x_lat, mod[l], norm1_g[l], w_in_p)

        y_pool = pool_mixer(cols, pool_w[l].astype(BF16), pool_scale[l])

        s0_lat = jnp.swapaxes(state_hgrn[:, l].astype(F32), -1, -2)
        o_f, o_b, sfin = hgrn_scan(cols, lp[l], s0_ctx, 0, BATCH, SEQ)
        o_f, o_b, _ = hgrn_scan(cols, lp[l], s0_lat, N_CTX, DEC_BATCH, DEC_SEQ, prev=(o_f, o_b))
        y_hg = hgrn_finish(o_f, o_b, cols, hg_norm_g[l])
        hs_.append(sfin)

        qs, kn = qk_prep(cols, cos_t, sin_t, q_norm_g[l], k_norm_g[l])
        sink = att_sink[l].astype(F32)
        y_att = attend_ctx(qs, kn, cols, sink, BATCH, SEQ)
        y_att = attend_lat(qs, kn, cols, cache_k[:, l].reshape(DEC_BATCH, PAST_LEN, 128).astype(F32),
                           cache_v[:, l].reshape(DEC_BATCH, PAST_LEN, 128).astype(F32), sink, y_att,
                           N_CTX, DEC_BATCH, DEC_SEQ)
        ks_.append(kn[:N_CTX].reshape(BATCH, SEQ, ATT_KV_HEADS, HEAD_DIM))
        vs_.append(cols[:N_CTX, 4224:4352].reshape(BATCH, SEQ, ATT_KV_HEADS, HEAD_DIM))

        h0_lat = jnp.concatenate([state_s5_re[:, l], state_s5_im[:, l]], axis=-1).astype(F32)
        h0_lat = jnp.pad(jnp.transpose(h0_lat, (2, 1, 0, 3)), ((0, 0), (0, 0), (0, 8 - DEC_BATCH), (0, 0)))
        ud = s5_gather(cols)
        ys, hf_c = s5_scan(ud, s5_ops, l, None, None, 0, BATCH, SEQ // S5_L)
        ys, = s5_scan(ud, s5_ops, l, h0_lat, ys, N_CTX, DEC_BATCH, DEC_SEQ // S5_L)
        y_s5 = s5_post(cols, ys, s5_d[l], s5_w_glu[l].astype(BF16))
        hf = hf_c.reshape(S5_GROUPS, 2, BATCH, SEQ // S5_L, 2 * S5_N)
        fin = jnp.stack([hf[:, 0, :, -1], hf[:, 1, :, 0]], axis=1)
        s5_.append(jnp.transpose(fin, (2, 1, 0, 3)))

        x1, h2 = out_proj((y_pool, y_hg, y_att, y_s5), w_out[l].astype(BF16), x_ctx, x_lat, mod[l], norm2_g[l])
        x_ctx, x_lat = conv_ffn(h2, x1, mod[l], ffn_w_up[l].astype(BF16), ffn_conv_w[l].astype(F32),
                                ffn_conv_b[l].astype(F32), ffn_w_down[l].astype(BF16))

    y_prompt = x_ctx.reshape(BATCH, SEQ, d).astype(x_prompt.dtype)
    y_sample = x_lat.reshape(DEC_BATCH, DEC_SEQ, d).astype(x_sample.dtype)
    s5_all = jnp.stack(s5_, axis=1)
    return (y_prompt, y_sample, jnp.stack(ks_, axis=1), jnp.stack(vs_, axis=1), jnp.stack(hs_, axis=1),
            s5_all[..., :S5_N], s5_all[..., S5_N:])
```

```python
import functools

import jax
import jax.numpy as jnp
import numpy as np
from jax import lax
from jax.experimental import pallas as pl
from jax.experimental.pallas import tpu as pltpu

F32 = jnp.float32
BF16 = jnp.bfloat16
HIGHEST = lax.Precision.HIGHEST
LOG2E = 1.4426950408889634

D_MODEL = 2048
BATCH = 32
SEQ = 256
DEPTH = 2
DEC_BATCH = 2
DEC_SEQ = 4096
PAST_LEN = 256
GRID_W = 64
GROUP_W = 512
POOL_WINDOWS = (2, 4, 8, 16)
POOL_GW = 128
HG_HEADS = 4
HG_DK = 128
ATT_HEADS = 8
ATT_KV_HEADS = 2
ATT_GROUP = 4
HEAD_DIM = 64
WINDOW = 128
BLOCK = 128
ROPE_BASE = 10000.0
S5_CH = 16
S5_GROUPS = 32
S5_N = 64
FFN_DIM = 5632
EPS = 1e-6
NEG_INF = -1e30

N_CTX = BATCH * SEQ
N_LAT = DEC_BATCH * DEC_SEQ
N_TOK = N_CTX + N_LAT
IN_COLS = 4352

CB_POOL, CB_HQ, CB_FF, CB_FB, CB_HI, CB_HG, CB_AQ, CB_S5 = range(8)
CB_AK, CB_AV = 32, 33

S5_L = 16
HG_C = 128
VMEM_LIMIT = 56 * 1024 * 1024


def _cparams(sem):
    return pltpu.CompilerParams(dimension_semantics=sem, vmem_limit_bytes=VMEM_LIMIT)


def _dot(a, b):
    return jnp.dot(a, b, preferred_element_type=F32)


def _dot_nt(a, b):
    return lax.dot_general(a, b, (((1,), (1,)), ((), ())), preferred_element_type=F32)


def _dot_tn(a, b):
    return lax.dot_general(a, b, (((0,), (0,)), ((), ())), preferred_element_type=F32)


def _split3(x):
    hi = x.astype(BF16)
    r1 = x - hi.astype(F32)
    mid = r1.astype(BF16)
    lo = (r1 - mid.astype(F32)).astype(BF16)
    return hi, mid, lo


def _dot_exact01(x, b01):
    hi, mid, lo = _split3(x)
    return _dot(hi, b01) + _dot(mid, b01) + _dot(lo, b01)


def _mod_index(row0):
    return jnp.where(row0 < N_CTX, 0, 1 + (row0 - N_CTX) // DEC_SEQ)


def _mod_kernel(cond_ref, w_ref, b_ref, o_ref):
    x = cond_ref[...]
    s = x * jax.nn.sigmoid(x)
    o_ref[...] = _dot(s.astype(BF16), w_ref[...].astype(BF16)) + b_ref[...]


def adaln_mod(cond, ada_w, ada_b, tn=1024):
    depth, d, n6 = ada_w.shape
    out = pl.pallas_call(
        _mod_kernel,
        grid=(depth, n6 // tn),
        in_specs=[pl.BlockSpec((8, d), lambda l, j: (0, 0)),
                  pl.BlockSpec((None, d, tn), lambda l, j: (l, 0, j)),
                  pl.BlockSpec((None, 1, tn), lambda l, j: (l, 0, j))],
        out_specs=pl.BlockSpec((None, 8, tn), lambda l, j: (l, 0, j)),
        out_shape=jax.ShapeDtypeStruct((depth, 8, n6), F32),
        compiler_params=_cparams(("parallel", "parallel")),
        name="adaln_mod",
    )(cond, ada_w, ada_b.reshape(depth, 1, n6))
    return out.reshape(depth, 8, 6, d)


def _split_specs(tm, d, row_tile):
    nc = N_CTX // tm
    return (pl.BlockSpec((tm, d), lambda *g: (jnp.minimum(row_tile(*g), nc - 1), 0)),
            pl.BlockSpec((tm, d), lambda *g: (jnp.maximum(row_tile(*g) - nc, 0), 0)))


def _inproj_kernel(xc_ref, xl_ref, mod_ref, g_ref, w_ref, o_ref, *, tm):
    x = jnp.where(pl.program_id(1) * tm < N_CTX, xc_ref[...], xl_ref[...])
    ms = jnp.mean(x * x, axis=-1, keepdims=True)
    h = x * lax.rsqrt(ms + EPS) * g_ref[...]
    h = h * (1.0 + mod_ref[1:2, :]) + mod_ref[0:1, :]
    o_ref[...] = _dot(h.astype(BF16), w_ref[...])


def in_proj(x_ctx, x_lat, mod_l, g1, w_in, layer, tm=256):
    d = x_ctx.shape[1]
    n = x_ctx.shape[0] + x_lat.shape[0]
    ncols = w_in.shape[2]
    tn = ncols
    xc_spec, xl_spec = _split_specs(tm, d, lambda j, i: i)
    return pl.pallas_call(
        functools.partial(_inproj_kernel, tm=tm),
        grid=(ncols // tn, n // tm),
        in_specs=[xc_spec, xl_spec,
                  pl.BlockSpec((None, 6, d), lambda j, i: (_mod_index(i * tm), 0, 0)),
                  pl.BlockSpec((1, d), lambda j, i: (0, 0)),
                  pl.BlockSpec((None, d, tn), lambda j, i: (layer, 0, j), pipeline_mode=pl.Buffered(1))],
        out_specs=pl.BlockSpec((tm, tn), lambda j, i: (i, j)),
        out_shape=jax.ShapeDtypeStruct((n, ncols), F32),
        compiler_params=_cparams(("parallel", "parallel")),
        name="in_proj",
    )(x_ctx, x_lat, mod_l, g1.reshape(1, d), w_in)


def _pool_kernel(u_ref, up_ref, un_ref, w_ref, sc_ref, o_ref, ext_ref, *, tm):
    i = pl.program_id(0)
    row0 = i * tm
    tseq = jnp.where(row0 < N_CTX, SEQ, DEC_SEQ)
    ext_ref[0:8, :] = up_ref[...]
    ext_ref[8:8 + tm, :] = u_ref[...]
    ext_ref[8 + tm:16 + tm, :] = un_ref[...]
    pos = (row0 + lax.broadcasted_iota(jnp.int32, (tm, 1), 0)) & (tseq - 1)
    outs = []
    for gi, w in enumerate(POOL_WINDOWS):
        ls = slice(gi * POOL_GW, (gi + 1) * POOL_GW)
        acc = None
        for o in range(-w // 2, w // 2):
            inside = (pos + o >= 0) & (pos + o < tseq)
            term = jnp.where(inside, ext_ref[8 + o:8 + o + tm, ls], 0.0)
            acc = term if acc is None else acc + term
        lo = jnp.maximum(pos - w // 2, 0)
        hi = jnp.minimum(pos + w // 2, tseq)
        cnt = (hi - lo).astype(F32)
        pooled = acc / cnt - ext_ref[8:8 + tm, ls]
        outs.append(_dot(pooled.astype(BF16), w_ref[gi]))
    o_ref[...] = (jnp.concatenate(outs, axis=1) * sc_ref[...]).astype(o_ref.dtype)


def pool_mixer(cols, pool_w, pool_scale, tm=1024):
    n = cols.shape[0]
    nb8 = n // 8
    return pl.pallas_call(
        functools.partial(_pool_kernel, tm=tm),
        grid=(n // tm,),
        in_specs=[pl.BlockSpec((tm, GROUP_W), lambda i: (i, CB_POOL)),
                  pl.BlockSpec((8, GROUP_W), lambda i: (jnp.maximum(i * (tm // 8) - 1, 0), CB_POOL)),
                  pl.BlockSpec((8, GROUP_W), lambda i: (jnp.minimum((i + 1) * (tm // 8), nb8 - 1), CB_POOL)),
                  pl.BlockSpec((4, POOL_GW, POOL_GW), lambda i: (0, 0, 0)),
                  pl.BlockSpec((1, GROUP_W), lambda i: (0, 0))],
        out_specs=pl.BlockSpec((tm, GROUP_W), lambda i: (i, 0)),
        out_shape=jax.ShapeDtypeStruct((n, GROUP_W), BF16),
        scratch_shapes=[pltpu.VMEM((tm + 16, GROUP_W), F32)],
        compiler_params=_cparams(("parallel",)),
        name="pool_mixer",
    )(cols, cols, cols, pool_w, pool_scale.reshape(1, GROUP_W))


def _block_ref_rows(x, c, m, rev):
    w = x.shape[1]
    if 2 * m >= 8:
        r = m if rev else m - 1
        xr = x.reshape(c // (2 * m), 2 * m, w)[:, r:r + 1, :]
        return jnp.broadcast_to(xr, (c // (2 * m), 2 * m, w)).reshape(c, w)
    x8 = x.reshape(c // 8, 8, w)
    sub = lax.broadcasted_iota(jnp.int32, (c // 8, 8, w), 1)
    out = None
    for kb in range(8 // (2 * m)):
        r = kb * 2 * m + (m if rev else m - 1)
        cand = jnp.broadcast_to(x8[:, r:r + 1, :], (c // 8, 8, w))
        out = cand if out is None else jnp.where(sub >= kb * 2 * m, cand, out)
    return out.reshape(c, w)


def _hgrn_chunk(q, fx, v, log_l, log1m_l, st_ref, rev, c):
    ls = jnp.minimum(fx, 0.0) - jnp.log(1.0 + jnp.exp(-jnp.abs(fx)))
    bb = log1m_l + ls
    logf = jnp.maximum(log_l, bb) + jnp.log(1.0 + jnp.exp(-jnp.abs(log_l - bb)))
    kk = 1.0 - jnp.exp(logf)
    row = lax.broadcasted_iota(jnp.int32, (c, c), 0)
    col = lax.broadcasted_iota(jnp.int32, (c, c), 1)
    causal = (row <= col) if rev else (row >= col)
    tri = jnp.where(causal, 1.0, 0.0).astype(BF16)
    hi, mid, lo = _split3(logf)
    b = (_dot(tri, hi) + _dot(tri, mid) + _dot(tri, lo)) * LOG2E
    b_last = b[0:1] if rev else b[c - 1:c]
    qt = (q * jnp.exp2(b)).astype(BF16)
    kd = (kk * jnp.exp2(b_last - b)).astype(BF16)
    dl = jnp.exp2(b_last)
    xor = row ^ col
    tbit = lax.broadcasted_iota(jnp.int32, (c, 1), 0)
    levels = []
    m = c // 2
    while m >= 1:
        wgt = jnp.exp2(-jnp.abs(b - _block_ref_rows(b, c, m, rev)))
        query_side = ((tbit & m) == 0) if rev else ((tbit & m) != 0)
        levels.append(((jnp.where(query_side, q, kk) * wgt).astype(BF16), (xor >> (m.bit_length() - 1)) == 1))
        m //= 2
    qb = q.astype(BF16)
    kb = kk.astype(BF16)
    vb = v.astype(BF16)
    outs = []
    for h in range(HG_HEADS):
        sl = slice(h * HG_DK, (h + 1) * HG_DK)
        att = jnp.where(xor == 0, _dot_nt(qb[:, sl], kb[:, sl]), 0.0)
        for z, msk in levels:
            att = att + jnp.where(msk, _dot_nt(z[:, sl], z[:, sl]), 0.0)
        att = jnp.where(causal, att, 0.0)
        intra = _dot(att.astype(BF16), vb[:, sl])
        st = st_ref[h]
        inter = _dot_nt(qt[:, sl], st.astype(BF16))
        st_ref[h] = st * dl[:, sl] + _dot_tn(vb[:, sl], kd[:, sl])
        outs.append(intra + inter)
    return jnp.concatenate(outs, axis=1)


def _hgrn_kernel(qf_ref, ff_ref, vf_ref, qb_ref, fb_ref, vb_ref, lp_ref, s0_ref, *rest, c, nchunks):
    of_ref, ob_ref, sfin_ref, st_ref = rest[-4:]
    ci = pl.program_id(1)

    @pl.when(ci == 0)
    def _():
        st_ref[...] = s0_ref[...]

    of_ref[...] = _hgrn_chunk(qf_ref[...], ff_ref[...], vf_ref[...], lp_ref[0, 0:1, :], lp_ref[0, 1:2, :],
                              st_ref.at[0], False, c)
    ob_ref[...] = _hgrn_chunk(qb_ref[...], fb_ref[...], vb_ref[...], lp_ref[1, 0:1, :], lp_ref[1, 1:2, :],
                              st_ref.at[1], True, c)

    @pl.when(ci == nchunks - 1)
    def _():
        for d in range(2):
            for h in range(HG_HEADS):
                sfin_ref[d, h] = st_ref[d, h].T


def hgrn_scan(cols, lp, s0_t, row_off, nseq, t, prev=None, c=HG_C):
    n = cols.shape[0]
    nchunks = t // c
    base = row_off // c
    extra = [] if prev is None else list(prev)
    any_spec = pl.BlockSpec(memory_space=pl.ANY)

    def fwd(cb):
        return pl.BlockSpec((c, GROUP_W), lambda s, i: (base + s * nchunks + i, cb))

    def bwd(cb):
        return pl.BlockSpec((c, GROUP_W), lambda s, i: (base + s * nchunks + nchunks - 1 - i, cb))

    st_spec = pl.BlockSpec((None, 2, HG_HEADS, HG_DK, HG_DK), lambda s, i: (s, 0, 0, 0, 0))
    return pl.pallas_call(
        functools.partial(_hgrn_kernel, c=c, nchunks=nchunks),
        grid=(nseq, nchunks),
        in_specs=[fwd(CB_HQ), fwd(CB_FF), fwd(CB_HI), bwd(CB_HQ), bwd(CB_FB), bwd(CB_HI),
                  pl.BlockSpec((2, 2, GROUP_W), lambda s, i: (0, 0, 0)), st_spec] + [any_spec] * len(extra),
        out_specs=[fwd(0), bwd(0), st_spec],
        out_shape=[jax.ShapeDtypeStruct((n, GROUP_W), F32),
                   jax.ShapeDtypeStruct((n, GROUP_W), F32),
                   jax.ShapeDtypeStruct((nseq, 2, HG_HEADS, HG_DK, HG_DK), F32)],
        input_output_aliases={8 + k: k for k in range(len(extra))},
        scratch_shapes=[pltpu.VMEM((2, HG_HEADS, HG_DK, HG_DK), F32)],
        compiler_params=_cparams(("parallel", "arbitrary")),
        name="hgrn_scan",
    )(cols, cols, cols, cols, cols, cols, lp, s0_t, *extra)


def _hgrn_finish_kernel(of_ref, ob_ref, g_ref, ng_ref, o_ref):
    o = of_ref[...] + ob_ref[...]
    g = g_ref[...]
    gate = g * jax.nn.sigmoid(g)
    outs = []
    for h in range(HG_HEADS):
        sl = slice(h * HG_DK, (h + 1) * HG_DK)
        oh = o[:, sl]
        ms = jnp.mean(oh * oh, axis=-1, keepdims=True)
        outs.append(oh * lax.rsqrt(ms + EPS) * ng_ref[...])
    o_ref[...] = (jnp.concatenate(outs, axis=1) * gate).astype(o_ref.dtype)


def hgrn_finish(o_f, o_b, cols, norm_g, tm=1024):
    n = o_f.shape[0]
    return pl.pallas_call(
        _hgrn_finish_kernel,
        grid=(n // tm,),
        in_specs=[pl.BlockSpec((tm, GROUP_W), lambda i: (i, 0)),
                  pl.BlockSpec((tm, GROUP_W), lambda i: (i, 0)),
                  pl.BlockSpec((tm, GROUP_W), lambda i: (i, CB_HG)),
                  pl.BlockSpec((1, HG_DK), lambda i: (0, 0))],
        out_specs=pl.BlockSpec((tm, GROUP_W), lambda i: (i, 0)),
        out_shape=jax.ShapeDtypeStruct((n, GROUP_W), BF16),
        compiler_params=_cparams(("parallel",)),
        name="hgrn_finish",
    )(o_f, o_b, cols, norm_g.reshape(1, HG_DK))


def _swap16(x):
    w = x.shape[1]
    lane = lax.broadcasted_iota(jnp.int32, x.shape, 1)
    return jnp.where((lane & 31) < 16, pltpu.roll(x, w - 16, 1), pltpu.roll(x, 16, 1))


def _qkprep_kernel(q_ref, k_ref, cos_ref, sin_ref, gq_ref, gk_ref, bq_ref, qo_ref, ko_ref):
    cos = cos_ref[...]
    sin = sin_ref[...]
    q = q_ref[...]
    msq = _dot_exact01(q * q, bq_ref[...]) * (1.0 / HEAD_DIM)
    qn = q * lax.rsqrt(msq + EPS) * gq_ref[...]
    cos4 = jnp.concatenate([cos] * 4, axis=1)
    sin4 = jnp.concatenate([sin] * 4, axis=1)
    qr = qn * cos4 + _swap16(qn) * sin4
    qo_ref[...] = (qr * (HEAD_DIM ** -0.5)).astype(qo_ref.dtype)
    k = k_ref[...]
    msk = _dot_exact01(k * k, bq_ref[0:128, 0:128]) * (1.0 / HEAD_DIM)
    kn = k * lax.rsqrt(msk + EPS) * gk_ref[...]
    ko_ref[...] = kn * cos + _swap16(kn) * sin


def qk_prep(cols, cos_t, sin_t, q_norm_g, k_norm_g, tm=512):
    n = cols.shape[0]
    head = np.arange(GROUP_W) // HEAD_DIM
    bones = jnp.asarray((head[:, None] == head[None, :]).astype(np.float32), BF16)
    return pl.pallas_call(
        _qkprep_kernel,
        grid=(n // tm,),
        in_specs=[pl.BlockSpec((tm, GROUP_W), lambda i: (i, CB_AQ)),
                  pl.BlockSpec((tm, 128), lambda i: (i, CB_AK)),
                  pl.BlockSpec((tm, 128), lambda i: (i, 0)),
                  pl.BlockSpec((tm, 128), lambda i: (i, 0)),
                  pl.BlockSpec((1, GROUP_W), lambda i: (0, 0)),
                  pl.BlockSpec((1, 128), lambda i: (0, 0)),
                  pl.BlockSpec((GROUP_W, GROUP_W), lambda i: (0, 0))],
        out_specs=[pl.BlockSpec((tm, GROUP_W), lambda i: (i, 0)),
                   pl.BlockSpec((tm, 128), lambda i: (i, 0))],
        out_shape=[jax.ShapeDtypeStruct((n, GROUP_W), BF16),
                   jax.ShapeDtypeStruct((n, 128), F32)],
        compiler_params=_cparams(("parallel",)),
        name="qk_prep",
    )(cols, cols, cos_t, sin_t, jnp.tile(q_norm_g, ATT_HEADS).reshape(1, GROUP_W),
      jnp.tile(k_norm_g, ATT_KV_HEADS).reshape(1, 128), bones)


def _attn_ctx_kernel(sink_ref, q_ref, k_ref, v_ref, o_ref, *, t, spb):
    for sq in range(spb):
        rs = slice(sq * t, (sq + 1) * t)
        q = q_ref[rs, :]
        k = k_ref[rs, :].astype(BF16)
        v = v_ref[rs, :].astype(BF16)
        pieces = [None] * ATT_HEADS
        for kh in range(ATT_KV_HEADS):
            hs = slice(kh * HEAD_DIM, (kh + 1) * HEAD_DIM)
            for g in range(ATT_GROUP):
                hq = kh * ATT_GROUP + g
                s = _dot_nt(q[:, hq * HEAD_DIM:(hq + 1) * HEAD_DIM], k[:, hs])
                sk = sink_ref[hq]
                mx = jnp.maximum(jnp.max(s, axis=1, keepdims=True), sk)
                p = jnp.exp(s - mx)
                den = jnp.sum(p, axis=1, keepdims=True) + jnp.exp(sk - mx)
                pieces[hq] = _dot(p.astype(BF16), v[:, hs]) / den
        o_ref[rs, :] = jnp.concatenate(pieces, axis=1).astype(o_ref.dtype)


def attend_ctx(qs, kn, cols, sink, nseq, t, spb=2):
    return pl.pallas_call(
        functools.partial(_attn_ctx_kernel, t=t, spb=spb),
        grid=(nseq // spb,),
        in_specs=[pl.BlockSpec(memory_space=pltpu.SMEM),
                  pl.BlockSpec((spb * t, GROUP_W), lambda s: (s, 0)),
                  pl.BlockSpec((spb * t, 128), lambda s: (s, 0)),
                  pl.BlockSpec((spb * t, 128), lambda s: (s, CB_AV))],
        out_specs=pl.BlockSpec((spb * t, GROUP_W), lambda s: (s, 0)),
        out_shape=jax.ShapeDtypeStruct((qs.shape[0], GROUP_W), BF16),
        compiler_params=_cparams(("parallel",)),
        name="attend_ctx",
    )(sink, qs, kn, cols)


def _attn_lat_kernel(sink_ref, q_ref, *refs, t, blk, nk):
    k_refs, v_refs = refs[:nk], refs[nk:2 * nk]
    ck_ref, cv_ref, _, o_ref = refs[2 * nk:]
    i = pl.program_id(1)
    q = q_ref[...]
    kl = jnp.concatenate([r[...] for r in k_refs], axis=0).astype(BF16)
    vl = jnp.concatenate([r[...] for r in v_refs], axis=0).astype(BF16)
    ck = ck_ref[...].astype(BF16)
    cv = cv_ref[...].astype(BF16)
    r_idx = lax.broadcasted_iota(jnp.int32, (blk, nk * WINDOW), 0)
    c_idx = lax.broadcasted_iota(jnp.int32, (blk, nk * WINDOW), 1)
    kpos = i * blk - WINDOW + lax.broadcasted_iota(jnp.int32, (1, nk * WINDOW), 1)
    col_bias = jnp.where(kpos >= 0, jnp.where(kpos < t, 0.0, NEG_INF), NEG_INF)
    bias = jnp.where(r_idx <= c_idx, jnp.where(r_idx >= c_idx - 2 * WINDOW, col_bias, NEG_INF), NEG_INF)
    pieces = [None] * ATT_HEADS
    for kh in range(ATT_KV_HEADS):
        hs = slice(kh * HEAD_DIM, (kh + 1) * HEAD_DIM)
        for g in range(ATT_GROUP):
            hq = kh * ATT_GROUP + g
            qh = q[:, hq * HEAD_DIM:(hq + 1) * HEAD_DIM]
            s_loc = _dot_nt(qh, kl[:, hs]) + bias
            s_ctx = _dot_nt(qh, ck[:, hs])
            sk = sink_ref[hq]
            mx = jnp.maximum(jnp.maximum(jnp.max(s_loc, axis=1, keepdims=True),
                                         jnp.max(s_ctx, axis=1, keepdims=True)), sk)
            p_loc = jnp.exp(s_loc - mx)
            p_ctx = jnp.exp(s_ctx - mx)
            den = (jnp.sum(p_loc, axis=1, keepdims=True) + jnp.sum(p_ctx, axis=1, keepdims=True)
                   + jnp.exp(sk - mx))
            pieces[hq] = (_dot(p_loc.astype(BF16), vl[:, hs]) + _dot(p_ctx.astype(BF16), cv[:, hs])) / den
    o_ref[...] = jnp.concatenate(pieces, axis=1).astype(o_ref.dtype)


def attend_lat(qs, kn, cols, cache_k, cache_v, sink, prev, row_off, nseq, t, blk=2 * WINDOW):
    nb = t // blk
    base = row_off // blk
    lctx = cache_k.shape[1]
    kpb = blk // WINDOW
    nk = kpb + 2
    nkb = t // WINDOW
    kbase = row_off // WINDOW

    def nbr(cb, j):
        return pl.BlockSpec((WINDOW, 128),
                            lambda b, i: (kbase + b * nkb + jnp.clip(i * kpb - 1 + j, 0, nkb - 1), cb))

    return pl.pallas_call(
        functools.partial(_attn_lat_kernel, t=t, blk=blk, nk=nk),
        grid=(nseq, nb),
        in_specs=[pl.BlockSpec(memory_space=pltpu.SMEM),
                  pl.BlockSpec((blk, GROUP_W), lambda b, i: (base + b * nb + i, 0))]
                 + [nbr(0, j) for j in range(nk)] + [nbr(CB_AV, j) for j in range(nk)]
                 + [pl.BlockSpec((None, lctx, 128), lambda b, i: (b, 0, 0)),
                    pl.BlockSpec((None, lctx, 128), lambda b, i: (b, 0, 0)),
                    pl.BlockSpec(memory_space=pl.ANY)],
        out_specs=pl.BlockSpec((blk, GROUP_W), lambda b, i: (base + b * nb + i, 0)),
        out_shape=jax.ShapeDtypeStruct(prev.shape, BF16),
        input_output_aliases={4 + 2 * nk: 0},
        compiler_params=_cparams(("parallel", "parallel")),
        name="attend_lat",
    )(sink, qs, *([kn] * nk), *([cols] * nk), cache_k, cache_v, prev)


def s5_operators(a_re, a_im, log_dt, b_re, b_im, c_re, c_im):
    L = S5_L
    dt = jnp.exp(log_dt.astype(F32))[..., None]
    are, aim = a_re.astype(F32), a_im.astype(F32)
    adr, adi = (are * dt)[..., None, :], (aim * dt)[..., None, :]

    def lam_pow(p):
        p = jnp.asarray(p, F32)[None, :, None, :, None]
        mag = jnp.exp(adr * p)
        return mag * jnp.cos(adi * p), mag * jnp.sin(adi * p)

    ar = np.arange(L)
    one = np.ones((2, 1))
    lre, lim = lam_pow(one)
    lre, lim = lre[..., 0, :], lim[..., 0, :]
    den = are * are + aim * aim
    fre = ((lre - 1.0) * are + lim * aim) / den
    fim = (lim * are - (lre - 1.0) * aim) / den
    bre = fre[..., None] * b_re - fim[..., None] * b_im
    bim = fre[..., None] * b_im + fim[..., None] * b_re
    bret, bimt = jnp.swapaxes(bre, -1, -2), jnp.swapaxes(bim, -1, -2)
    cre, cim = c_re.astype(F32)[..., None, :, :], c_im.astype(F32)[..., None, :, :]
    ein = functools.partial(jnp.einsum, precision=HIGHEST)

    def c_lam(p):
        pr, pi = lam_pow(p)
        pr, pi = pr[..., None, :], pi[..., None, :]
        return cre * pr - cim * pi, cre * pi + cim * pr

    cpr, cpi = c_lam(one * ar)
    ktau = ein('ldgtcn,ldgne->ldgtce', cpr, bre) - ein('ldgtcn,ldgne->ldgtce', cpi, bim)
    lw = L * S5_CH
    rowf = jnp.transpose(ktau[:, 0], (0, 1, 4, 2, 3)).reshape(ktau.shape[0], -1, S5_CH, lw)
    rowb = jnp.transpose(jnp.flip(ktau[:, 1], axis=2), (0, 1, 4, 2, 3)).reshape(ktau.shape[0], -1, S5_CH, lw)
    zeros = jnp.zeros_like(rowf)
    padf = jnp.concatenate([zeros, rowf], axis=-1)
    padb = jnp.concatenate([rowb, zeros], axis=-1)
    kf = jnp.stack([padf[..., lw - S5_CH * s:2 * lw - S5_CH * s] for s in range(L)], axis=2)
    kb = jnp.stack([padb[..., S5_CH * (L - 1 - s):S5_CH * (L - 1 - s) + lw] for s in range(L)], axis=2)
    kmat = jnp.stack([kf, kb], axis=1).reshape(ktau.shape[:3] + (lw, lw))
    pr, pi = lam_pow(np.stack([L - 1 - ar, ar]))
    pr, pi = pr[..., None, :], pi[..., None, :]
    wr = pr * bret[..., None, :, :] - pi * bimt[..., None, :, :]
    wi = pr * bimt[..., None, :, :] + pi * bret[..., None, :, :]
    wst = jnp.concatenate([wr, wi], axis=-1).reshape(ktau.shape[:3] + (L * S5_CH, 2 * S5_N))
    orr, oii = c_lam(np.stack([ar + 1, L - ar]))
    wout = jnp.concatenate([jnp.moveaxis(orr, -1, -3), -jnp.moveaxis(oii, -1, -3)], axis=-3)
    wout = wout.reshape(ktau.shape[:3] + (2 * S5_N, L * S5_CH))
    jre, jim = lam_pow(one * (L * 2.0 ** np.arange(9)))
    lam = jnp.stack([jnp.concatenate([jre, jre], axis=-1), jnp.concatenate([-jim, jim], axis=-1)], axis=-2)
    return kmat.astype(BF16), wst.astype(BF16), wout.astype(BF16), lam


S5_BUNDLE = 128 // S5_CH


def _s5_group(gl, u, km_ref, ws_ref, wo_ref, lam_ref, h0_ref, hfin_ref, seglen, nseq):
    r = nseq * seglen
    rowi = lax.broadcasted_iota(jnp.int32, (r, 1), 0)
    cl = rowi & (seglen - 1)

    def cmul(d, j, x):
        return lam_ref[d, gl, j, 0:1, :] * x + lam_ref[d, gl, j, 1:2, :] * pltpu.roll(x, S5_N, 1)

    y = None
    for d in range(2):
        edge = 0 if d == 0 else seglen - 1
        h0rows = jnp.zeros((r, 2 * S5_N), F32)
        x = _dot(u, ws_ref[d, gl])
        if h0_ref is not None:
            for s in range(nseq):
                h0rows = jnp.where(rowi == s * seglen + edge, h0_ref[gl, d, s:s + 1, :], h0rows)
            x = x + cmul(d, 0, h0rows)
        sh, j = 1, 0
        while sh < seglen:
            if d == 0:
                xs, ok = pltpu.roll(x, sh, 0), cl >= sh
            else:
                xs, ok = pltpu.roll(x, r - sh, 0), cl < seglen - sh
            x = x + jnp.where(ok, cmul(d, j, xs), 0.0)
            sh, j = sh * 2, j + 1
        if hfin_ref is not None:
            hfin_ref[gl, d] = x
        if d == 0:
            hin = jnp.where(cl >= 1, pltpu.roll(x, 1, 0), h0rows)
        else:
            hin = jnp.where(cl < seglen - 1, pltpu.roll(x, r - 1, 0), h0rows)
        yd = _dot(u, km_ref[d, gl]) + _dot(hin.astype(BF16), wo_ref[d, gl])
        y = yd if y is None else y + yd
    return y


def _s5_kernel(*refs, seglen, nseq, has_h0, has_prev, has_fin):
    u_ref, km_ref, ws_ref, wo_ref, lam_ref = refs[:5]
    pos = 5
    h0_ref = refs[pos] if has_h0 else None
    pos += has_h0 + has_prev
    y_ref = refs[pos]
    hfin_ref = refs[pos + 1] if has_fin else None
    ycat_ref, rin_ref, rout_ref = refs[-3:]
    lane_blk = lax.broadcasted_iota(jnp.int32, (1, 128), 1) // S5_CH
    nv = S5_L // S5_BUNDLE
    nb = S5_BUNDLE

    def merge(pick):
        acc = pick(0)
        for b in range(1, nb):
            acc = jnp.where(lane_blk == b, pick(b), acc)
        return acc

    for v in range(nv):
        for s in range(nb):
            m = merge(lambda g: u_ref[v * nb + (g + s) % nb])
            rin_ref[v, s] = pltpu.roll(m, s * S5_CH, 1) if s else m

    def body(gl, carry):
        halves = [merge(lambda k: rin_ref[v, (k - gl) & (nb - 1)]) for v in range(nv)]
        u = jnp.concatenate(halves, axis=1).astype(BF16)
        ycat_ref[gl] = _s5_group(gl, u, km_ref, ws_ref, wo_ref, lam_ref, h0_ref, hfin_ref, seglen, nseq)
        return carry

    lax.fori_loop(0, nb, body, 0)
    for v in range(nv):
        for s in range(nb):
            m = merge(lambda k: ycat_ref[(k + s) % nb, :, v * 128:(v + 1) * 128])
            rout_ref[s] = pltpu.roll(m, s * S5_CH, 1) if s else m
        for k in range(nb):
            y_ref[v * nb + k] = merge(lambda g: rout_ref[(g - k) % nb])


S5_TILE = 512


def _s5_perm():
    p = np.arange(S5_TILE)
    src = (p % (S5_TILE // S5_L)) * S5_L + p // (S5_TILE // S5_L)
    return (src[:, None] == np.arange(S5_TILE)[None, :]).astype(np.float32)


def _permute_rows(p, x, transposed):
    dot = _dot_tn if transposed else _dot
    hi, mid, lo = _split3(x)
    return dot(p, hi) + dot(p, mid) + dot(p, lo)


def _s5_gather_kernel(u_ref, p_ref, o_ref):
    o_ref[...] = _permute_rows(p_ref[...], u_ref[...], False).reshape(o_ref.shape)


def s5_gather(cols):
    n = cols.shape[0]
    cpt = S5_TILE // S5_L
    return pl.pallas_call(
        _s5_gather_kernel,
        grid=(n // S5_TILE,),
        in_specs=[pl.BlockSpec((S5_TILE, GROUP_W), lambda i: (i, CB_S5)),
                  pl.BlockSpec((S5_TILE, S5_TILE), lambda i: (0, 0))],
        out_specs=pl.BlockSpec((S5_L, cpt, GROUP_W), lambda i: (0, i, 0)),
        out_shape=jax.ShapeDtypeStruct((S5_L, n // S5_L, GROUP_W), F32),
        compiler_params=_cparams(("parallel",)),
        name="s5_gather",
    )(cols, jnp.asarray(_s5_perm(), BF16))


def s5_scan(ud, ops, layer, h0, prev, row_off, nseq, seglen):
    kmat, wst, wout, lam = ops
    nchunks = ud.shape[1]
    r = nseq * seglen
    rb = row_off // (r * S5_L)
    assert row_off % (r * S5_L) == 0
    lw = S5_L * S5_CH
    nb = S5_BUNDLE
    in_specs = [pl.BlockSpec((S5_L, r, 128), lambda j: (0, rb, j)),
                pl.BlockSpec((None, 2, nb, lw, lw), lambda j: (layer, 0, j, 0, 0)),
                pl.BlockSpec((None, 2, nb, lw, 2 * S5_N), lambda j: (layer, 0, j, 0, 0)),
                pl.BlockSpec((None, 2, nb, 2 * S5_N, lw), lambda j: (layer, 0, j, 0, 0)),
                pl.BlockSpec((None, 2, nb, 9, 2, 2 * S5_N), lambda j: (layer, 0, j, 0, 0, 0))]
    args = [ud, kmat, wst, wout, lam]
    if h0 is not None:
        in_specs.append(pl.BlockSpec((nb, 2, h0.shape[2], 2 * S5_N), lambda j: (j, 0, 0, 0)))
        args.append(h0)
    aliases = {}
    if prev is not None:
        aliases = {len(args): 0}
        in_specs.append(pl.BlockSpec(memory_space=pl.ANY))
        args.append(prev)
    out_specs = [pl.BlockSpec((S5_L, r, 128), lambda j: (0, rb, j))]
    out_shape = [jax.ShapeDtypeStruct((S5_L, nchunks, GROUP_W), F32)]
    has_fin = prev is None
    if has_fin:
        out_specs.append(pl.BlockSpec((nb, 2, r, 2 * S5_N), lambda j: (j, 0, 0, 0)))
        out_shape.append(jax.ShapeDtypeStruct((S5_GROUPS, 2, r, 2 * S5_N), F32))
    return pl.pallas_call(
        functools.partial(_s5_kernel, seglen=seglen, nseq=nseq, has_h0=h0 is not None,
                          has_prev=prev is not None, has_fin=has_fin),
        grid=(S5_GROUPS // nb,),
        in_specs=in_specs,
        out_specs=out_specs,
        out_shape=out_shape,
        input_output_aliases=aliases,
        scratch_shapes=[pltpu.VMEM((nb, r, lw), F32), pltpu.VMEM((S5_L // nb, nb, r, 128), F32),
                        pltpu.VMEM((nb, r, 128), F32)],
        compiler_params=_cparams(("parallel",)),
        name="s5_scan",
    )(*args)


def _s5_post_kernel(u_ref, y_ref, p_ref, d_ref, w_ref, o_ref):
    y = _permute_rows(p_ref[...], y_ref[...].reshape(S5_TILE, GROUP_W), True)
    out = d_ref[...] * u_ref[...] + y
    yy = jax.nn.gelu(out)
    zz = _dot(yy.astype(BF16), w_ref[...])
    o_ref[...] = (zz[:, :GROUP_W] * jax.nn.sigmoid(zz[:, GROUP_W:])).astype(o_ref.dtype)


def s5_post(cols, y, s5_d, w_glu):
    n = cols.shape[0]
    cpt = S5_TILE // S5_L
    return pl.pallas_call(
        _s5_post_kernel,
        grid=(n // S5_TILE,),
        in_specs=[pl.BlockSpec((S5_TILE, GROUP_W), lambda i: (i, CB_S5)),
                  pl.BlockSpec((S5_L, cpt, GROUP_W), lambda i: (0, i, 0)),
                  pl.BlockSpec((S5_TILE, S5_TILE), lambda i: (0, 0)),
                  pl.BlockSpec((1, GROUP_W), lambda i: (0, 0)),
                  pl.BlockSpec((GROUP_W, 2 * GROUP_W), lambda i: (0, 0))],
        out_specs=pl.BlockSpec((S5_TILE, GROUP_W), lambda i: (i, 0)),
        out_shape=jax.ShapeDtypeStruct((n, GROUP_W), BF16),
        compiler_params=_cparams(("parallel",)),
        name="s5_post",
    )(cols, y, jnp.asarray(_s5_perm(), BF16), s5_d.reshape(1, GROUP_W), w_glu)


def _outproj_kernel(p0_ref, p1_ref, p2_ref, p3_ref, w_ref, xc_ref, xl_ref, mod_ref, g_ref, x1_ref, h2_ref, *, tm):
    y = _dot(p0_ref[...], w_ref[0:GROUP_W, :])
    y = y + _dot(p1_ref[...], w_ref[GROUP_W:2 * GROUP_W, :])
    y = y + _dot(p2_ref[...], w_ref[2 * GROUP_W:3 * GROUP_W, :])
    y = y + _dot(p3_ref[...], w_ref[3 * GROUP_W:4 * GROUP_W, :])
    x = jnp.where(pl.program_id(0) * tm < N_CTX, xc_ref[...], xl_ref[...])
    x1 = x + mod_ref[2:3, :] * y
    x1_ref[...] = x1
    ms = jnp.mean(x1 * x1, axis=-1, keepdims=True)
    h = x1 * lax.rsqrt(ms + EPS) * g_ref[...]
    h2_ref[...] = (h * (1.0 + mod_ref[4:5, :]) + mod_ref[3:4, :]).astype(h2_ref.dtype)


def out_proj(parts, w_out, layer, x_ctx, x_lat, mod_l, g2, tm=512):
    d = x_ctx.shape[1]
    n = x_ctx.shape[0] + x_lat.shape[0]
    part_spec = pl.BlockSpec((tm, GROUP_W), lambda i: (i, 0))
    xc_spec, xl_spec = _split_specs(tm, d, lambda i: i)
    return pl.pallas_call(
        functools.partial(_outproj_kernel, tm=tm),
        grid=(n // tm,),
        in_specs=[part_spec, part_spec, part_spec, part_spec,
                  pl.BlockSpec((None, 4 * GROUP_W, d), lambda i: (layer, 0, 0)),
                  xc_spec, xl_spec,
                  pl.BlockSpec((None, 6, d), lambda i: (_mod_index(i * tm), 0, 0)),
                  pl.BlockSpec((1, d), lambda i: (0, 0))],
        out_specs=[pl.BlockSpec((tm, d), lambda i: (i, 0)),
                   pl.BlockSpec((tm, d), lambda i: (i, 0))],
        out_shape=[jax.ShapeDtypeStruct((n, d), F32),
                   jax.ShapeDtypeStruct((n, d), BF16)],
        compiler_params=_cparams(("parallel",)),
        name="out_proj",
    )(*parts, w_out, x_ctx, x_lat, mod_l, g2.reshape(1, d))


FFN_HALO = 16


def _ffn_kernel(h_ref, hp_ref, hn_ref, wa_ref, wb_ref, wd_ref, cw_ref, cb_ref, x1_ref, mod_ref, oc_ref, ol_ref,
                ext_ref, aext_ref, acc_ref, *, tm, nf):
    i = pl.program_id(0)
    j = pl.program_id(1)

    @pl.when(j == 0)
    def _():
        ext_ref[0:FFN_HALO, :] = hp_ref[...]
        ext_ref[FFN_HALO:FFN_HALO + tm, :] = h_ref[...]
        ext_ref[FFN_HALO + tm:2 * FFN_HALO + tm, :] = hn_ref[...]
        acc_ref[...] = jnp.zeros_like(acc_ref)

    aext_ref[...] = _dot(ext_ref[...], wa_ref[...])
    b = _dot(h_ref[...], wb_ref[...])
    row0 = i * tm
    tseq = jnp.where(row0 < N_CTX, SEQ, DEC_SEQ)
    pos = (row0 + lax.broadcasted_iota(jnp.int32, (tm, 1), 0)) & (tseq - 1)
    a_prev = jnp.where(pos == 0, 0.0, aext_ref[FFN_HALO - 1:FFN_HALO - 1 + tm, :])
    a_mid = aext_ref[FFN_HALO:FFN_HALO + tm, :]
    a_next = jnp.where(pos == tseq - 1, 0.0, aext_ref[FFN_HALO + 1:FFN_HALO + 1 + tm, :])
    a = a_prev * cw_ref[0:1, :] + a_mid * cw_ref[1:2, :] + a_next * cw_ref[2:3, :] + cb_ref[...]
    act = a * jax.nn.sigmoid(a) * b
    acc_ref[...] += _dot(act.astype(BF16), wd_ref[...])

    @pl.when((j == nf - 1) & (row0 < N_CTX))
    def _():
        oc_ref[...] = x1_ref[...] + mod_ref[5:6, :] * acc_ref[...]

    @pl.when((j == nf - 1) & (row0 >= N_CTX))
    def _():
        ol_ref[...] = x1_ref[...] + mod_ref[5:6, :] * acc_ref[...]


def conv_ffn(h2, x1, mod_l, w_up, conv_w, conv_b, w_down, layer, tm=512, tf=512):
    n, d = x1.shape
    f = w_down.shape[1]
    nf = f // tf
    nhb = n // FFN_HALO
    oc_spec, ol_spec = _split_specs(tm, d, lambda i, j: i)
    return pl.pallas_call(
        functools.partial(_ffn_kernel, tm=tm, nf=nf),
        grid=(n // tm, nf),
        in_specs=[pl.BlockSpec((tm, d), lambda i, j: (i, 0)),
                  pl.BlockSpec((FFN_HALO, d), lambda i, j: (jnp.maximum(i * (tm // FFN_HALO) - 1, 0), 0)),
                  pl.BlockSpec((FFN_HALO, d), lambda i, j: (jnp.minimum((i + 1) * (tm // FFN_HALO), nhb - 1), 0)),
                  pl.BlockSpec((None, d, tf), lambda i, j: (layer, 0, j)),
                  pl.BlockSpec((None, d, tf), lambda i, j: (layer, 0, nf + j)),
                  pl.BlockSpec((None, tf, d), lambda i, j: (layer, j, 0)),
                  pl.BlockSpec((3, tf), lambda i, j: (0, j)),
                  pl.BlockSpec((1, tf), lambda i, j: (0, j)),
                  pl.BlockSpec((tm, d), lambda i, j: (i, 0)),
                  pl.BlockSpec((None, 6, d), lambda i, j: (_mod_index(i * tm), 0, 0))],
        out_specs=[oc_spec, ol_spec],
        out_shape=[jax.ShapeDtypeStruct((N_CTX, d), F32), jax.ShapeDtypeStruct((n - N_CTX, d), F32)],
        scratch_shapes=[pltpu.VMEM((tm + 2 * FFN_HALO, d), BF16),
                        pltpu.VMEM((tm + 2 * FFN_HALO, tf), F32),
                        pltpu.VMEM((tm, d), F32)],
        compiler_params=_cparams(("arbitrary", "arbitrary")),
        name="conv_ffn",
    )(h2, h2, h2, w_up, w_up, w_down, conv_w, conv_b.reshape(1, f), x1, mod_l)


def _rope_tables():
    half = HEAD_DIM // 2
    nf = half // 2
    inv = ROPE_BASE ** (-jnp.arange(nf, dtype=F32) / nf)
    t = jnp.arange(DEC_SEQ)
    rows = (t // GRID_W).astype(F32)
    cols = (t % GRID_W).astype(F32)
    ang = jnp.concatenate([rows[:, None] * inv[None]] * 2 + [cols[:, None] * inv[None]] * 2, axis=1)
    sign = jnp.tile(jnp.concatenate([-jnp.ones(nf, F32), jnp.ones(nf, F32)]), 2)
    cos = jnp.tile(jnp.cos(ang), (DEC_BATCH, ATT_KV_HEADS))
    sin = jnp.tile(jnp.sin(ang) * sign, (DEC_BATCH, ATT_KV_HEADS))
    cos = jnp.concatenate([jnp.ones((N_CTX, 128), F32), cos], axis=0)
    sin = jnp.concatenate([jnp.zeros((N_CTX, 128), F32), sin], axis=0)
    return cos, sin


def kernel(x_prompt, x_sample, cache_k, cache_v, state_hgrn, state_s5_re, state_s5_im, c, c_ctx, norm1_g, norm2_g, ada_w, ada_b, w_in, w_out, pool_w, pool_scale, hg_lb_raw, hg_norm_g, q_norm_g, k_norm_g, att_sink, s5_a_re, s5_a_im, s5_log_dt, s5_b_re, s5_b_im, s5_c_re, s5_c_im, s5_d, s5_w_glu, ffn_w_up, ffn_conv_w, ffn_conv_b, ffn_w_down):
    d = D_MODEL
    x_ctx = x_prompt.reshape(N_CTX, d).astype(F32)
    x_lat = x_sample.reshape(N_LAT, d).astype(F32)

    cond = jnp.concatenate([c_ctx[None].astype(F32), c.astype(F32), jnp.zeros((8 - 1 - DEC_BATCH, d), F32)], axis=0)
    mod = adaln_mod(cond, ada_w, ada_b)

    lb_cum = jnp.cumsum(jax.nn.softmax(hg_lb_raw.astype(F32), axis=0), axis=0)
    hg_lb = lb_cum - lb_cum[:1]
    lp = jnp.stack([jnp.log(hg_lb), jnp.log1p(-hg_lb)], axis=2)

    cos_t, sin_t = _rope_tables()
    s0_ctx = jnp.zeros((BATCH, 2, HG_HEADS, HG_DK, HG_DK), F32)
    s5_ops = s5_operators(s5_a_re, s5_a_im, s5_log_dt, s5_b_re, s5_b_im, s5_c_re, s5_c_im)

    w_in_p = jnp.concatenate([w_in[..., :3584], w_in[..., 3840:4352], w_in[..., 3584:3840]], axis=-1).astype(BF16)
    w_out_b = w_out.astype(BF16)
    w_up_b = ffn_w_up.astype(BF16)
    w_down_b = ffn_w_down.astype(BF16)

    ks_, vs_, hs_, s5_ = [], [], [], []
    for l in range(DEPTH):
        cols = in_proj(x_ctx, x_lat, mod[l], norm1_g[l], w_in_p, l)

        y_pool = pool_mixer(cols, pool_w[l].astype(BF16), pool_scale[l])

        s0_lat = jnp.swapaxes(state_hgrn[:, l].astype(F32), -1, -2)
        o_f, o_b, sfin = hgrn_scan(cols, lp[l], s0_ctx, 0, BATCH, SEQ)
        o_f, o_b, _ = hgrn_scan(cols, lp[l], s0_lat, N_CTX, DEC_BATCH, DEC_SEQ, prev=(o_f, o_b))
        y_hg = hgrn_finish(o_f, o_b, cols, hg_norm_g[l])
        hs_.append(sfin)

        qs, kn = qk_prep(cols, cos_t, sin_t, q_norm_g[l], k_norm_g[l])
        sink = att_sink[l].astype(F32)
        y_att = attend_ctx(qs, kn, cols, sink, BATCH, SEQ)
        y_att = attend_lat(qs, kn, cols, cache_k[:, l].reshape(DEC_BATCH, PAST_LEN, 128).astype(F32),
                           cache_v[:, l].reshape(DEC_BATCH, PAST_LEN, 128).astype(F32), sink, y_att,
                           N_CTX, DEC_BATCH, DEC_SEQ)
        ks_.append(kn[:N_CTX].reshape(BATCH, SEQ, ATT_KV_HEADS, HEAD_DIM))
        vs_.append(cols[:N_CTX, 4224:4352].reshape(BATCH, SEQ, ATT_KV_HEADS, HEAD_DIM))

        h0_lat = jnp.concatenate([state_s5_re[:, l], state_s5_im[:, l]], axis=-1).astype(F32)
        h0_lat = jnp.pad(jnp.transpose(h0_lat, (2, 1, 0, 3)), ((0, 0), (0, 0), (0, 8 - DEC_BATCH), (0, 0)))
        ud = s5_gather(cols)
        ys, hf_c = s5_scan(ud, s5_ops, l, None, None, 0, BATCH, SEQ // S5_L)
        ys, = s5_scan(ud, s5_ops, l, h0_lat, ys, N_CTX, DEC_BATCH, DEC_SEQ // S5_L)
        y_s5 = s5_post(cols, ys, s5_d[l], s5_w_glu[l].astype(BF16))
        hf = hf_c.reshape(S5_GROUPS, 2, BATCH, SEQ // S5_L, 2 * S5_N)
        fin = jnp.stack([hf[:, 0, :, -1], hf[:, 1, :, 0]], axis=1)
        s5_.append(jnp.transpose(fin, (2, 1, 0, 3)))

        x1, h2 = out_proj((y_pool, y_hg, y_att, y_s5), w_out_b, l, x_ctx, x_lat, mod[l], norm2_g[l])
        x_ctx, x_lat = conv_ffn(h2, x1, mod[l], w_up_b, ffn_conv_w[l].astype(F32), ffn_conv_b[l].astype(F32),
                                w_down_b, l)

    y_prompt = x_ctx.reshape(BATCH, SEQ, d).astype(x_prompt.dtype)
    y_sample = x_lat.reshape(DEC_BATCH, DEC_SEQ, d).astype(x_sample.dtype)
    s5_all = jnp.stack(s5_, axis=1)
    return (y_prompt, y_sample, jnp.stack(ks_, axis=1), jnp.stack(vs_, axis=1), jnp.stack(hs_, axis=1),
            s5_all[..., :S5_N], s5_all[..., S5_N:])
```

```python
import functools

import jax
import jax.numpy as jnp
import numpy as np
from jax import lax
from jax.experimental import pallas as pl
from jax.experimental.pallas import tpu as pltpu

F32 = jnp.float32
BF16 = jnp.bfloat16
HIGHEST = lax.Precision.HIGHEST
LOG2E = 1.4426950408889634

D_MODEL = 2048
BATCH = 32
SEQ = 256
DEPTH = 2
DEC_BATCH = 2
DEC_SEQ = 4096
PAST_LEN = 256
GRID_W = 64
GROUP_W = 512
POOL_WINDOWS = (2, 4, 8, 16)
POOL_GW = 128
HG_HEADS = 4
HG_DK = 128
ATT_HEADS = 8
ATT_KV_HEADS = 2
ATT_GROUP = 4
HEAD_DIM = 64
WINDOW = 128
BLOCK = 128
ROPE_BASE = 10000.0
S5_CH = 16
S5_GROUPS = 32
S5_N = 64
FFN_DIM = 5632
EPS = 1e-6
NEG_INF = -1e30

N_CTX = BATCH * SEQ
N_LAT = DEC_BATCH * DEC_SEQ
N_TOK = N_CTX + N_LAT
IN_COLS = 4352

CB_POOL, CB_HQ, CB_FF, CB_FB, CB_HI, CB_HG, CB_AQ, CB_S5 = range(8)
CB_AK, CB_AV = 32, 33

S5_L = 16
HG_C = 128
VMEM_LIMIT = 56 * 1024 * 1024


def _cparams(sem):
    return pltpu.CompilerParams(dimension_semantics=sem, vmem_limit_bytes=VMEM_LIMIT)


def _dot(a, b):
    return jnp.dot(a, b, preferred_element_type=F32)


def _dot_nt(a, b):
    return lax.dot_general(a, b, (((1,), (1,)), ((), ())), preferred_element_type=F32)


def _dot_tn(a, b):
    return lax.dot_general(a, b, (((0,), (0,)), ((), ())), preferred_element_type=F32)


def _split3(x):
    hi = x.astype(BF16)
    r1 = x - hi.astype(F32)
    mid = r1.astype(BF16)
    lo = (r1 - mid.astype(F32)).astype(BF16)
    return hi, mid, lo


def _dot_exact01(x, b01):
    hi, mid, lo = _split3(x)
    return _dot(hi, b01) + _dot(mid, b01) + _dot(lo, b01)


def _mod_index(row0):
    return jnp.where(row0 < N_CTX, 0, 1 + (row0 - N_CTX) // DEC_SEQ)


def _mod_kernel(cond_ref, w_ref, b_ref, o_ref):
    x = cond_ref[...]
    s = x * jax.nn.sigmoid(x)
    o_ref[...] = _dot(s.astype(BF16), w_ref[...].astype(BF16)) + b_ref[...]


def adaln_mod(cond, ada_w, ada_b, tn=1024):
    depth, d, n6 = ada_w.shape
    out = pl.pallas_call(
        _mod_kernel,
        grid=(depth, n6 // tn),
        in_specs=[pl.BlockSpec((8, d), lambda l, j: (0, 0)),
                  pl.BlockSpec((None, d, tn), lambda l, j: (l, 0, j)),
                  pl.BlockSpec((None, 1, tn), lambda l, j: (l, 0, j))],
        out_specs=pl.BlockSpec((None, 8, tn), lambda l, j: (l, 0, j)),
        out_shape=jax.ShapeDtypeStruct((depth, 8, n6), F32),
        compiler_params=_cparams(("parallel", "parallel")),
        name="adaln_mod",
    )(cond, ada_w, ada_b.reshape(depth, 1, n6))
    return out.reshape(depth, 8, 6, d)


def _split_specs(tm, d, row_tile):
    nc = N_CTX // tm
    return (pl.BlockSpec((tm, d), lambda *g: (jnp.minimum(row_tile(*g), nc - 1), 0)),
            pl.BlockSpec((tm, d), lambda *g: (jnp.maximum(row_tile(*g) - nc, 0), 0)))


def _inproj_kernel(xc_ref, xl_ref, mod_ref, g_ref, w_ref, o_ref, *, tm):
    x = jnp.where(pl.program_id(1) * tm < N_CTX, xc_ref[...], xl_ref[...])
    ms = jnp.mean(x * x, axis=-1, keepdims=True)
    h = x * lax.rsqrt(ms + EPS) * g_ref[...]
    h = h * (1.0 + mod_ref[1:2, :]) + mod_ref[0:1, :]
    o_ref[...] = _dot(h.astype(BF16), w_ref[...])


def in_proj(x_ctx, x_lat, mod_l, g1, w_in, layer, tm=256):
    d = x_ctx.shape[1]
    n = x_ctx.shape[0] + x_lat.shape[0]
    ncols = w_in.shape[2]
    tn = ncols
    xc_spec, xl_spec = _split_specs(tm, d, lambda j, i: i)
    return pl.pallas_call(
        functools.partial(_inproj_kernel, tm=tm),
        grid=(ncols // tn, n // tm),
        in_specs=[xc_spec, xl_spec,
                  pl.BlockSpec((None, 6, d), lambda j, i: (_mod_index(i * tm), 0, 0)),
                  pl.BlockSpec((1, d), lambda j, i: (0, 0)),
                  pl.BlockSpec((None, d, tn), lambda j, i: (layer, 0, j), pipeline_mode=pl.Buffered(1))],
        out_specs=pl.BlockSpec((tm, tn), lambda j, i: (i, j)),
        out_shape=jax.ShapeDtypeStruct((n, ncols), F32),
        compiler_params=_cparams(("parallel", "parallel")),
        name="in_proj",
    )(x_ctx, x_lat, mod_l, g1.reshape(1, d), w_in)


def _pool_kernel(u_ref, up_ref, un_ref, w_ref, sc_ref, o_ref, ext_ref, *, tm):
    i = pl.program_id(0)
    row0 = i * tm
    tseq = jnp.where(row0 < N_CTX, SEQ, DEC_SEQ)
    ext_ref[0:8, :] = up_ref[...]
    ext_ref[8:8 + tm, :] = u_ref[...]
    ext_ref[8 + tm:16 + tm, :] = un_ref[...]
    pos = (row0 + lax.broadcasted_iota(jnp.int32, (tm, 1), 0)) & (tseq - 1)
    outs = []
    for gi, w in enumerate(POOL_WINDOWS):
        ls = slice(gi * POOL_GW, (gi + 1) * POOL_GW)
        acc = None
        for o in range(-w // 2, w // 2):
            inside = lax.bitcast_convert_type(pos + o, jnp.uint32) < tseq.astype(jnp.uint32)
            term = jnp.where(inside, ext_ref[8 + o:8 + o + tm, ls], 0.0)
            acc = term if acc is None else acc + term
        lo = jnp.maximum(pos - w // 2, 0)
        hi = jnp.minimum(pos + w // 2, tseq)
        cnt = (hi - lo).astype(F32)
        pooled = acc / cnt - ext_ref[8:8 + tm, ls]
        outs.append(_dot(pooled.astype(BF16), w_ref[gi]))
    o_ref[...] = (jnp.concatenate(outs, axis=1) * sc_ref[...]).astype(o_ref.dtype)


def pool_mixer(cols, pool_w, pool_scale, tm=1024):
    n = cols.shape[0]
    nb8 = n // 8
    return pl.pallas_call(
        functools.partial(_pool_kernel, tm=tm),
        grid=(n // tm,),
        in_specs=[pl.BlockSpec((tm, GROUP_W), lambda i: (i, CB_POOL)),
                  pl.BlockSpec((8, GROUP_W), lambda i: (jnp.maximum(i * (tm // 8) - 1, 0), CB_POOL)),
                  pl.BlockSpec((8, GROUP_W), lambda i: (jnp.minimum((i + 1) * (tm // 8), nb8 - 1), CB_POOL)),
                  pl.BlockSpec((4, POOL_GW, POOL_GW), lambda i: (0, 0, 0)),
                  pl.BlockSpec((1, GROUP_W), lambda i: (0, 0))],
        out_specs=pl.BlockSpec((tm, GROUP_W), lambda i: (i, 0)),
        out_shape=jax.ShapeDtypeStruct((n, GROUP_W), BF16),
        scratch_shapes=[pltpu.VMEM((tm + 16, GROUP_W), F32)],
        compiler_params=_cparams(("parallel",)),
        name="pool_mixer",
    )(cols, cols, cols, pool_w, pool_scale.reshape(1, GROUP_W))


def _block_ref_rows(x, c, m, rev):
    w = x.shape[1]
    if 2 * m >= 8:
        r = m if rev else m - 1
        xr = x.reshape(c // (2 * m), 2 * m, w)[:, r:r + 1, :]
        return jnp.broadcast_to(xr, (c // (2 * m), 2 * m, w)).reshape(c, w)
    x8 = x.reshape(c // 8, 8, w)
    sub = lax.broadcasted_iota(jnp.int32, (c // 8, 8, w), 1)
    out = None
    for kb in range(8 // (2 * m)):
        r = kb * 2 * m + (m if rev else m - 1)
        cand = jnp.broadcast_to(x8[:, r:r + 1, :], (c // 8, 8, w))
        out = cand if out is None else jnp.where(sub >= kb * 2 * m, cand, out)
    return out.reshape(c, w)


def _hgrn_chunk(q, fx, v, log_l, log1m_l, st_ref, rev, c):
    ls = jnp.minimum(fx, 0.0) - jnp.log(1.0 + jnp.exp(-jnp.abs(fx)))
    bb = log1m_l + ls
    logf = jnp.maximum(log_l, bb) + jnp.log(1.0 + jnp.exp(-jnp.abs(log_l - bb)))
    kk = 1.0 - jnp.exp(logf)
    row = lax.broadcasted_iota(jnp.int32, (c, c), 0)
    col = lax.broadcasted_iota(jnp.int32, (c, c), 1)
    causal = (row <= col) if rev else (row >= col)
    tri = jnp.where(causal, 1.0, 0.0).astype(BF16)
    hi, mid, lo = _split3(logf)
    b = (_dot(tri, hi) + _dot(tri, mid) + _dot(tri, lo)) * LOG2E
    b_last = b[0:1] if rev else b[c - 1:c]
    qt = (q * jnp.exp2(b)).astype(BF16)
    kd = (kk * jnp.exp2(b_last - b)).astype(BF16)
    dl = jnp.exp2(b_last)
    xor = row ^ col
    tbit = lax.broadcasted_iota(jnp.int32, (c, 1), 0)
    levels = []
    m = c // 2
    while m >= 1:
        query_side = ((tbit & m) == 0) if rev else ((tbit & m) != 0)
        wgt = jnp.exp2((b - _block_ref_rows(b, c, m, rev)) * jnp.where(query_side, 1.0, -1.0))
        levels.append(((jnp.where(query_side, q, kk) * wgt).astype(BF16), (xor >> (m.bit_length() - 1)) == 1))
        m //= 2
    qb = q.astype(BF16)
    kb = kk.astype(BF16)
    vb = v.astype(BF16)
    outs = []
    for h in range(HG_HEADS):
        sl = slice(h * HG_DK, (h + 1) * HG_DK)
        att = _dot_nt(qb[:, sl], kb[:, sl])
        for z, msk in levels:
            att = jnp.where(msk, _dot_nt(z[:, sl], z[:, sl]), att)
        att = jnp.where(causal, att, 0.0)
        intra = _dot(att.astype(BF16), vb[:, sl])
        st = st_ref[h]
        inter = _dot_nt(qt[:, sl], st.astype(BF16))
        st_ref[h] = st * dl[:, sl] + _dot_tn(vb[:, sl], kd[:, sl])
        outs.append(intra + inter)
    return jnp.concatenate(outs, axis=1)


def _hgrn_kernel(qf_ref, ff_ref, vf_ref, qb_ref, fb_ref, vb_ref, lp_ref, s0_ref, *rest, c, nchunks):
    of_ref, ob_ref, sfin_ref, st_ref = rest[-4:]
    ci = pl.program_id(1)

    @pl.when(ci == 0)
    def _():
        st_ref[...] = s0_ref[...]

    of_ref[...] = _hgrn_chunk(qf_ref[...], ff_ref[...], vf_ref[...], lp_ref[0, 0:1, :], lp_ref[0, 1:2, :],
                              st_ref.at[0], False, c)
    ob_ref[...] = _hgrn_chunk(qb_ref[...], fb_ref[...], vb_ref[...], lp_ref[1, 0:1, :], lp_ref[1, 1:2, :],
                              st_ref.at[1], True, c)

    @pl.when(ci == nchunks - 1)
    def _():
        for d in range(2):
            for h in range(HG_HEADS):
                sfin_ref[d, h] = st_ref[d, h].T


def hgrn_scan(cols, lp, s0_t, row_off, nseq, t, prev=None, c=HG_C):
    n = cols.shape[0]
    nchunks = t // c
    base = row_off // c
    extra = [] if prev is None else list(prev)
    any_spec = pl.BlockSpec(memory_space=pl.ANY)

    def fwd(cb):
        return pl.BlockSpec((c, GROUP_W), lambda s, i: (base + s * nchunks + i, cb))

    def bwd(cb):
        return pl.BlockSpec((c, GROUP_W), lambda s, i: (base + s * nchunks + nchunks - 1 - i, cb))

    st_spec = pl.BlockSpec((None, 2, HG_HEADS, HG_DK, HG_DK), lambda s, i: (s, 0, 0, 0, 0))
    return pl.pallas_call(
        functools.partial(_hgrn_kernel, c=c, nchunks=nchunks),
        grid=(nseq, nchunks),
        in_specs=[fwd(CB_HQ), fwd(CB_FF), fwd(CB_HI), bwd(CB_HQ), bwd(CB_FB), bwd(CB_HI),
                  pl.BlockSpec((2, 2, GROUP_W), lambda s, i: (0, 0, 0)), st_spec] + [any_spec] * len(extra),
        out_specs=[fwd(0), bwd(0), st_spec],
        out_shape=[jax.ShapeDtypeStruct((n, GROUP_W), F32),
                   jax.ShapeDtypeStruct((n, GROUP_W), F32),
                   jax.ShapeDtypeStruct((nseq, 2, HG_HEADS, HG_DK, HG_DK), F32)],
        input_output_aliases={8 + k: k for k in range(len(extra))},
        scratch_shapes=[pltpu.VMEM((2, HG_HEADS, HG_DK, HG_DK), F32)],
        compiler_params=_cparams(("parallel", "arbitrary")),
        name="hgrn_scan",
    )(cols, cols, cols, cols, cols, cols, lp, s0_t, *extra)


def _hgrn_finish_kernel(of_ref, ob_ref, g_ref, ng_ref, o_ref):
    o = of_ref[...] + ob_ref[...]
    g = g_ref[...]
    gate = g * jax.nn.sigmoid(g)
    outs = []
    for h in range(HG_HEADS):
        sl = slice(h * HG_DK, (h + 1) * HG_DK)
        oh = o[:, sl]
        ms = jnp.mean(oh * oh, axis=-1, keepdims=True)
        outs.append(oh * lax.rsqrt(ms + EPS) * ng_ref[...])
    o_ref[...] = (jnp.concatenate(outs, axis=1) * gate).astype(o_ref.dtype)


def hgrn_finish(o_f, o_b, cols, norm_g, tm=1024):
    n = o_f.shape[0]
    return pl.pallas_call(
        _hgrn_finish_kernel,
        grid=(n // tm,),
        in_specs=[pl.BlockSpec((tm, GROUP_W), lambda i: (i, 0)),
                  pl.BlockSpec((tm, GROUP_W), lambda i: (i, 0)),
                  pl.BlockSpec((tm, GROUP_W), lambda i: (i, CB_HG)),
                  pl.BlockSpec((1, HG_DK), lambda i: (0, 0))],
        out_specs=pl.BlockSpec((tm, GROUP_W), lambda i: (i, 0)),
        out_shape=jax.ShapeDtypeStruct((n, GROUP_W), BF16),
        compiler_params=_cparams(("parallel",)),
        name="hgrn_finish",
    )(o_f, o_b, cols, norm_g.reshape(1, HG_DK))


def _swap16(x):
    w = x.shape[1]
    lane = lax.broadcasted_iota(jnp.int32, x.shape, 1)
    return jnp.where((lane & 31) < 16, pltpu.roll(x, w - 16, 1), pltpu.roll(x, 16, 1))


def _qkprep_kernel(q_ref, k_ref, cos_ref, sin_ref, gq_ref, gk_ref, bq_ref, qo_ref, ko_ref):
    cos = cos_ref[...]
    sin = sin_ref[...]
    q = q_ref[...]
    msq = _dot_exact01(q * q, bq_ref[...]) * (1.0 / HEAD_DIM)
    qn = q * lax.rsqrt(msq + EPS) * gq_ref[...]
    cos4 = jnp.concatenate([cos] * 4, axis=1)
    sin4 = jnp.concatenate([sin] * 4, axis=1)
    qr = qn * cos4 + _swap16(qn) * sin4
    qo_ref[...] = (qr * (HEAD_DIM ** -0.5)).astype(qo_ref.dtype)
    k = k_ref[...]
    msk = _dot_exact01(k * k, bq_ref[0:128, 0:128]) * (1.0 / HEAD_DIM)
    kn = k * lax.rsqrt(msk + EPS) * gk_ref[...]
    ko_ref[...] = kn * cos + _swap16(kn) * sin


def qk_prep(cols, cos_t, sin_t, q_norm_g, k_norm_g, tm=512):
    n = cols.shape[0]
    head = np.arange(GROUP_W) // HEAD_DIM
    bones = jnp.asarray((head[:, None] == head[None, :]).astype(np.float32), BF16)
    return pl.pallas_call(
        _qkprep_kernel,
        grid=(n // tm,),
        in_specs=[pl.BlockSpec((tm, GROUP_W), lambda i: (i, CB_AQ)),
                  pl.BlockSpec((tm, 128), lambda i: (i, CB_AK)),
                  pl.BlockSpec((tm, 128), lambda i: (i, 0)),
                  pl.BlockSpec((tm, 128), lambda i: (i, 0)),
                  pl.BlockSpec((1, GROUP_W), lambda i: (0, 0)),
                  pl.BlockSpec((1, 128), lambda i: (0, 0)),
                  pl.BlockSpec((GROUP_W, GROUP_W), lambda i: (0, 0))],
        out_specs=[pl.BlockSpec((tm, GROUP_W), lambda i: (i, 0)),
                   pl.BlockSpec((tm, 128), lambda i: (i, 0))],
        out_shape=[jax.ShapeDtypeStruct((n, GROUP_W), BF16),
                   jax.ShapeDtypeStruct((n, 128), F32)],
        compiler_params=_cparams(("parallel",)),
        name="qk_prep",
    )(cols, cols, cos_t, sin_t, jnp.tile(q_norm_g, ATT_HEADS).reshape(1, GROUP_W),
      jnp.tile(k_norm_g, ATT_KV_HEADS).reshape(1, 128), bones)


def _attn_ctx_kernel(sink_ref, q_ref, k_ref, v_ref, o_ref, *, t, spb):
    for sq in range(spb):
        rs = slice(sq * t, (sq + 1) * t)
        q = q_ref[rs, :]
        k = k_ref[rs, :].astype(BF16)
        v = v_ref[rs, :].astype(BF16)
        pieces = [None] * ATT_HEADS
        for kh in range(ATT_KV_HEADS):
            hs = slice(kh * HEAD_DIM, (kh + 1) * HEAD_DIM)
            for g in range(ATT_GROUP):
                hq = kh * ATT_GROUP + g
                s = _dot_nt(q[:, hq * HEAD_DIM:(hq + 1) * HEAD_DIM], k[:, hs])
                sk = sink_ref[hq]
                mx = jnp.maximum(jnp.max(s, axis=1, keepdims=True), sk)
                p = jnp.exp(s - mx)
                den = jnp.sum(p, axis=1, keepdims=True) + jnp.exp(sk - mx)
                pieces[hq] = _dot(p.astype(BF16), v[:, hs]) / den
        o_ref[rs, :] = jnp.concatenate(pieces, axis=1).astype(o_ref.dtype)


def attend_ctx(qs, kn, cols, sink, nseq, t, spb=2):
    return pl.pallas_call(
        functools.partial(_attn_ctx_kernel, t=t, spb=spb),
        grid=(nseq // spb,),
        in_specs=[pl.BlockSpec(memory_space=pltpu.SMEM),
                  pl.BlockSpec((spb * t, GROUP_W), lambda s: (s, 0)),
                  pl.BlockSpec((spb * t, 128), lambda s: (s, 0)),
                  pl.BlockSpec((spb * t, 128), lambda s: (s, CB_AV))],
        out_specs=pl.BlockSpec((spb * t, GROUP_W), lambda s: (s, 0)),
        out_shape=jax.ShapeDtypeStruct((qs.shape[0], GROUP_W), BF16),
        compiler_params=_cparams(("parallel",)),
        name="attend_ctx",
    )(sink, qs, kn, cols)


def _attn_lat_kernel(sink_ref, q_ref, *refs, t, blk, nk):
    k_refs, v_refs = refs[:nk], refs[nk:2 * nk]
    ck_ref, cv_ref, _, o_ref = refs[2 * nk:]
    i = pl.program_id(1)
    q = q_ref[...]
    kl = jnp.concatenate([r[...] for r in k_refs], axis=0).astype(BF16)
    vl = jnp.concatenate([r[...] for r in v_refs], axis=0).astype(BF16)
    ck = ck_ref[...].astype(BF16)
    cv = cv_ref[...].astype(BF16)
    r_idx = lax.broadcasted_iota(jnp.int32, (blk, nk * WINDOW), 0)
    c_idx = lax.broadcasted_iota(jnp.int32, (blk, nk * WINDOW), 1)
    kpos = i * blk - WINDOW + lax.broadcasted_iota(jnp.int32, (1, nk * WINDOW), 1)
    col_bias = jnp.where(kpos >= 0, jnp.where(kpos < t, 0.0, NEG_INF), NEG_INF)
    bias = jnp.where(r_idx <= c_idx, jnp.where(r_idx >= c_idx - 2 * WINDOW, col_bias, NEG_INF), NEG_INF)
    pieces = [None] * ATT_HEADS
    for kh in range(ATT_KV_HEADS):
        hs = slice(kh * HEAD_DIM, (kh + 1) * HEAD_DIM)
        for g in range(ATT_GROUP):
            hq = kh * ATT_GROUP + g
            qh = q[:, hq * HEAD_DIM:(hq + 1) * HEAD_DIM]
            s_loc = _dot_nt(qh, kl[:, hs]) + bias
            s_ctx = _dot_nt(qh, ck[:, hs])
            sk = sink_ref[hq]
            mx = jnp.maximum(jnp.maximum(jnp.max(s_loc, axis=1, keepdims=True),
                                         jnp.max(s_ctx, axis=1, keepdims=True)), sk)
            p_loc = jnp.exp(s_loc - mx)
            p_ctx = jnp.exp(s_ctx - mx)
            den = (jnp.sum(p_loc, axis=1, keepdims=True) + jnp.sum(p_ctx, axis=1, keepdims=True)
                   + jnp.exp(sk - mx))
            pieces[hq] = (_dot(p_loc.astype(BF16), vl[:, hs]) + _dot(p_ctx.astype(BF16), cv[:, hs])) / den
    o_ref[...] = jnp.concatenate(pieces, axis=1).astype(o_ref.dtype)


def attend_lat(qs, kn, cols, cache_k, cache_v, sink, prev, row_off, nseq, t, blk=2 * WINDOW):
    nb = t // blk
    base = row_off // blk
    lctx = cache_k.shape[1]
    kpb = blk // WINDOW
    nk = kpb + 2
    nkb = t // WINDOW
    kbase = row_off // WINDOW

    def nbr(cb, j):
        return pl.BlockSpec((WINDOW, 128),
                            lambda b, i: (kbase + b * nkb + jnp.clip(i * kpb - 1 + j, 0, nkb - 1), cb))

    return pl.pallas_call(
        functools.partial(_attn_lat_kernel, t=t, blk=blk, nk=nk),
        grid=(nseq, nb),
        in_specs=[pl.BlockSpec(memory_space=pltpu.SMEM),
                  pl.BlockSpec((blk, GROUP_W), lambda b, i: (base + b * nb + i, 0))]
                 + [nbr(0, j) for j in range(nk)] + [nbr(CB_AV, j) for j in range(nk)]
                 + [pl.BlockSpec((None, lctx, 128), lambda b, i: (b, 0, 0)),
                    pl.BlockSpec((None, lctx, 128), lambda b, i: (b, 0, 0)),
                    pl.BlockSpec(memory_space=pl.ANY)],
        out_specs=pl.BlockSpec((blk, GROUP_W), lambda b, i: (base + b * nb + i, 0)),
        out_shape=jax.ShapeDtypeStruct(prev.shape, BF16),
        input_output_aliases={4 + 2 * nk: 0},
        compiler_params=_cparams(("parallel", "parallel")),
        name="attend_lat",
    )(sink, qs, *([kn] * nk), *([cols] * nk), cache_k, cache_v, prev)


def s5_operators(a_re, a_im, log_dt, b_re, b_im, c_re, c_im):
    L = S5_L
    dt = jnp.exp(log_dt.astype(F32))[..., None]
    are, aim = a_re.astype(F32), a_im.astype(F32)
    adr, adi = (are * dt)[..., None, :], (aim * dt)[..., None, :]

    def lam_pow(p):
        p = jnp.asarray(p, F32)[None, :, None, :, None]
        mag = jnp.exp(adr * p)
        return mag * jnp.cos(adi * p), mag * jnp.sin(adi * p)

    ar = np.arange(L)
    one = np.ones((2, 1))
    lre, lim = lam_pow(one)
    lre, lim = lre[..., 0, :], lim[..., 0, :]
    den = are * are + aim * aim
    fre = ((lre - 1.0) * are + lim * aim) / den
    fim = (lim * are - (lre - 1.0) * aim) / den
    bre = fre[..., None] * b_re - fim[..., None] * b_im
    bim = fre[..., None] * b_im + fim[..., None] * b_re
    bret, bimt = jnp.swapaxes(bre, -1, -2), jnp.swapaxes(bim, -1, -2)
    cre, cim = c_re.astype(F32)[..., None, :, :], c_im.astype(F32)[..., None, :, :]
    ein = functools.partial(jnp.einsum, precision=HIGHEST)

    def c_lam(p):
        pr, pi = lam_pow(p)
        pr, pi = pr[..., None, :], pi[..., None, :]
        return cre * pr - cim * pi, cre * pi + cim * pr

    cpr, cpi = c_lam(one * ar)
    ktau = ein('ldgtcn,ldgne->ldgtce', cpr, bre) - ein('ldgtcn,ldgne->ldgtce', cpi, bim)
    lw = L * S5_CH
    rowf = jnp.transpose(ktau[:, 0], (0, 1, 4, 2, 3)).reshape(ktau.shape[0], -1, S5_CH, lw)
    rowb = jnp.transpose(jnp.flip(ktau[:, 1], axis=2), (0, 1, 4, 2, 3)).reshape(ktau.shape[0], -1, S5_CH, lw)
    zeros = jnp.zeros_like(rowf)
    padf = jnp.concatenate([zeros, rowf], axis=-1)
    padb = jnp.concatenate([rowb, zeros], axis=-1)
    kf = jnp.stack([padf[..., lw - S5_CH * s:2 * lw - S5_CH * s] for s in range(L)], axis=2)
    kb = jnp.stack([padb[..., S5_CH * (L - 1 - s):S5_CH * (L - 1 - s) + lw] for s in range(L)], axis=2)
    kmat = jnp.stack([kf, kb], axis=1).reshape(ktau.shape[:3] + (lw, lw))
    pr, pi = lam_pow(np.stack([L - 1 - ar, ar]))
    pr, pi = pr[..., None, :], pi[..., None, :]
    wr = pr * bret[..., None, :, :] - pi * bimt[..., None, :, :]
    wi = pr * bimt[..., None, :, :] + pi * bret[..., None, :, :]
    wst = jnp.concatenate([wr, wi], axis=-1).reshape(ktau.shape[:3] + (L * S5_CH, 2 * S5_N))
    orr, oii = c_lam(np.stack([ar + 1, L - ar]))
    wout = jnp.concatenate([jnp.moveaxis(orr, -1, -3), -jnp.moveaxis(oii, -1, -3)], axis=-3)
    wout = wout.reshape(ktau.shape[:3] + (2 * S5_N, L * S5_CH))
    jre, jim = lam_pow(one * (L * 2.0 ** np.arange(9)))
    lam = jnp.stack([jnp.concatenate([jre, jre], axis=-1), jnp.concatenate([-jim, jim], axis=-1)], axis=-2)
    return kmat.astype(BF16), wst.astype(BF16), wout.astype(BF16), lam


S5_BUNDLE = 128 // S5_CH


def _s5_group(gl, u, km_ref, ws_ref, wo_ref, lam_ref, h0_ref, hfin_ref, seglen, nseq):
    r = nseq * seglen
    rowi = lax.broadcasted_iota(jnp.int32, (r, 1), 0)
    cl = rowi & (seglen - 1)

    def cmul(d, j, x):
        return lam_ref[d, gl, j, 0:1, :] * x + lam_ref[d, gl, j, 1:2, :] * pltpu.roll(x, S5_N, 1)

    y = None
    for d in range(2):
        edge = 0 if d == 0 else seglen - 1
        h0rows = jnp.zeros((r, 2 * S5_N), F32)
        x = _dot(u, ws_ref[d, gl])
        if h0_ref is not None:
            for s in range(nseq):
                h0rows = jnp.where(rowi == s * seglen + edge, h0_ref[gl, d, s:s + 1, :], h0rows)
            x = x + cmul(d, 0, h0rows)
        sh, j = 1, 0
        while sh < seglen:
            if d == 0:
                xs, ok = pltpu.roll(x, sh, 0), cl >= sh
            else:
                xs, ok = pltpu.roll(x, r - sh, 0), cl < seglen - sh
            x = x + jnp.where(ok, cmul(d, j, xs), 0.0)
            sh, j = sh * 2, j + 1
        if hfin_ref is not None:
            hfin_ref[gl, d] = x
        if d == 0:
            hin = jnp.where(cl >= 1, pltpu.roll(x, 1, 0), h0rows)
        else:
            hin = jnp.where(cl < seglen - 1, pltpu.roll(x, r - 1, 0), h0rows)
        yd = _dot(u, km_ref[d, gl]) + _dot(hin.astype(BF16), wo_ref[d, gl])
        y = yd if y is None else y + yd
    return y


def _s5_kernel(*refs, seglen, nseq, has_h0, has_prev, has_fin):
    u_ref, km_ref, ws_ref, wo_ref, lam_ref = refs[:5]
    pos = 5
    h0_ref = refs[pos] if has_h0 else None
    pos += has_h0 + has_prev
    y_ref = refs[pos]
    hfin_ref = refs[pos + 1] if has_fin else None
    ycat_ref, rin_ref, rout_ref = refs[-3:]
    lane_blk = lax.broadcasted_iota(jnp.int32, (1, 128), 1) // S5_CH
    nv = S5_L // S5_BUNDLE
    nb = S5_BUNDLE

    def merge(pick):
        acc = pick(0)
        for b in range(1, nb):
            acc = jnp.where(lane_blk == b, pick(b), acc)
        return acc

    for v in range(nv):
        for s in range(nb):
            m = merge(lambda g: u_ref[v * nb + (g + s) % nb])
            rin_ref[v, s] = pltpu.roll(m, s * S5_CH, 1) if s else m

    def body(gl, carry):
        halves = [merge(lambda k: rin_ref[v, (k - gl) & (nb - 1)]) for v in range(nv)]
        u = jnp.concatenate(halves, axis=1).astype(BF16)
        ycat_ref[gl] = _s5_group(gl, u, km_ref, ws_ref, wo_ref, lam_ref, h0_ref, hfin_ref, seglen, nseq)
        return carry

    lax.fori_loop(0, nb, body, 0)
    for v in range(nv):
        for s in range(nb):
            m = merge(lambda k: ycat_ref[(k + s) % nb, :, v * 128:(v + 1) * 128])
            rout_ref[s] = pltpu.roll(m, s * S5_CH, 1) if s else m
        for k in range(nb):
            y_ref[v * nb + k] = merge(lambda g: rout_ref[(g - k) % nb])


S5_TILE = 512


def _s5_perm():
    p = np.arange(S5_TILE)
    src = (p % (S5_TILE // S5_L)) * S5_L + p // (S5_TILE // S5_L)
    return (src[:, None] == np.arange(S5_TILE)[None, :]).astype(np.float32)


def _s5_gather_kernel(u_ref, p_ref, o_ref):
    p = p_ref[...]
    hi, mid, lo = _split3(u_ref[...])
    o_ref[...] = (_dot(p, hi) + _dot(p, mid) + _dot(p, lo)).reshape(o_ref.shape)


def s5_gather(cols):
    n = cols.shape[0]
    cpt = S5_TILE // S5_L
    return pl.pallas_call(
        _s5_gather_kernel,
        grid=(n // S5_TILE,),
        in_specs=[pl.BlockSpec((S5_TILE, GROUP_W), lambda i: (i, CB_S5)),
                  pl.BlockSpec((S5_TILE, S5_TILE), lambda i: (0, 0))],
        out_specs=pl.BlockSpec((S5_L, cpt, GROUP_W), lambda i: (0, i, 0)),
        out_shape=jax.ShapeDtypeStruct((S5_L, n // S5_L, GROUP_W), F32),
        compiler_params=_cparams(("parallel",)),
        name="s5_gather",
    )(cols, jnp.asarray(_s5_perm(), BF16))


def s5_scan(ud, ops, layer, h0, prev, row_off, nseq, seglen):
    kmat, wst, wout, lam = ops
    nchunks = ud.shape[1]
    r = nseq * seglen
    rb = row_off // (r * S5_L)
    assert row_off % (r * S5_L) == 0
    lw = S5_L * S5_CH
    nb = S5_BUNDLE
    in_specs = [pl.BlockSpec((S5_L, r, 128), lambda j: (0, rb, j)),
                pl.BlockSpec((None, 2, nb, lw, lw), lambda j: (layer, 0, j, 0, 0)),
                pl.BlockSpec((None, 2, nb, lw, 2 * S5_N), lambda j: (layer, 0, j, 0, 0)),
                pl.BlockSpec((None, 2, nb, 2 * S5_N, lw), lambda j: (layer, 0, j, 0, 0)),
                pl.BlockSpec((None, 2, nb, 9, 2, 2 * S5_N), lambda j: (layer, 0, j, 0, 0, 0))]
    args = [ud, kmat, wst, wout, lam]
    if h0 is not None:
        in_specs.append(pl.BlockSpec((nb, 2, h0.shape[2], 2 * S5_N), lambda j: (j, 0, 0, 0)))
        args.append(h0)
    aliases = {}
    if prev is not None:
        aliases = {len(args): 0}
        in_specs.append(pl.BlockSpec(memory_space=pl.ANY))
        args.append(prev)
    out_specs = [pl.BlockSpec((S5_L, r, 128), lambda j: (0, rb, j))]
    out_shape = [jax.ShapeDtypeStruct((S5_L, nchunks, GROUP_W), F32)]
    has_fin = prev is None
    if has_fin:
        out_specs.append(pl.BlockSpec((nb, 2, r, 2 * S5_N), lambda j: (j, 0, 0, 0)))
        out_shape.append(jax.ShapeDtypeStruct((S5_GROUPS, 2, r, 2 * S5_N), F32))
    return pl.pallas_call(
        functools.partial(_s5_kernel, seglen=seglen, nseq=nseq, has_h0=h0 is not None,
                          has_prev=prev is not None, has_fin=has_fin),
        grid=(S5_GROUPS // nb,),
        in_specs=in_specs,
        out_specs=out_specs,
        out_shape=out_shape,
        input_output_aliases=aliases,
        scratch_shapes=[pltpu.VMEM((nb, r, lw), F32), pltpu.VMEM((S5_L // nb, nb, r, 128), F32),
                        pltpu.VMEM((nb, r, 128), F32)],
        compiler_params=_cparams(("parallel",)),
        name="s5_scan",
    )(*args)


def _s5_post_kernel(u_ref, y_ref, p_ref, d_ref, w_ref, o_ref):
    out = (d_ref[...] * u_ref[...] + y_ref[...]).reshape(S5_TILE, GROUP_W)
    yy = jax.nn.gelu(out)
    zz = _dot(yy.astype(BF16), w_ref[...])
    res = (zz[:, :GROUP_W] * jax.nn.sigmoid(zz[:, GROUP_W:])).astype(BF16)
    o_ref[...] = _dot_tn(p_ref[...], res).astype(o_ref.dtype)


def s5_post(ud, y, s5_d, w_glu):
    n = ud.shape[1] * S5_L
    cpt = S5_TILE // S5_L
    return pl.pallas_call(
        _s5_post_kernel,
        grid=(n // S5_TILE,),
        in_specs=[pl.BlockSpec((S5_L, cpt, GROUP_W), lambda i: (0, i, 0)),
                  pl.BlockSpec((S5_L, cpt, GROUP_W), lambda i: (0, i, 0)),
                  pl.BlockSpec((S5_TILE, S5_TILE), lambda i: (0, 0)),
                  pl.BlockSpec((1, GROUP_W), lambda i: (0, 0)),
                  pl.BlockSpec((GROUP_W, 2 * GROUP_W), lambda i: (0, 0))],
        out_specs=pl.BlockSpec((S5_TILE, GROUP_W), lambda i: (i, 0)),
        out_shape=jax.ShapeDtypeStruct((n, GROUP_W), BF16),
        compiler_params=_cparams(("parallel",)),
        name="s5_post",
    )(ud, y, jnp.asarray(_s5_perm(), BF16), s5_d.reshape(1, GROUP_W), w_glu)


def _outproj_kernel(p0_ref, p1_ref, p2_ref, p3_ref, w_ref, xc_ref, xl_ref, mod_ref, g_ref, x1_ref, h2_ref, *, tm):
    y = _dot(p0_ref[...], w_ref[0:GROUP_W, :])
    y = y + _dot(p1_ref[...], w_ref[GROUP_W:2 * GROUP_W, :])
    y = y + _dot(p2_ref[...], w_ref[2 * GROUP_W:3 * GROUP_W, :])
    y = y + _dot(p3_ref[...], w_ref[3 * GROUP_W:4 * GROUP_W, :])
    x = jnp.where(pl.program_id(0) * tm < N_CTX, xc_ref[...], xl_ref[...])
    x1 = x + mod_ref[2:3, :] * y
    x1_ref[...] = x1
    ms = jnp.mean(x1 * x1, axis=-1, keepdims=True)
    h = x1 * lax.rsqrt(ms + EPS) * g_ref[...]
    h2_ref[...] = (h * (1.0 + mod_ref[4:5, :]) + mod_ref[3:4, :]).astype(h2_ref.dtype)


def out_proj(parts, w_out, layer, x_ctx, x_lat, mod_l, g2, tm=512):
    d = x_ctx.shape[1]
    n = x_ctx.shape[0] + x_lat.shape[0]
    part_spec = pl.BlockSpec((tm, GROUP_W), lambda i: (i, 0))
    xc_spec, xl_spec = _split_specs(tm, d, lambda i: i)
    return pl.pallas_call(
        functools.partial(_outproj_kernel, tm=tm),
        grid=(n // tm,),
        in_specs=[part_spec, part_spec, part_spec, part_spec,
                  pl.BlockSpec((None, 4 * GROUP_W, d), lambda i: (layer, 0, 0)),
                  xc_spec, xl_spec,
                  pl.BlockSpec((None, 6, d), lambda i: (_mod_index(i * tm), 0, 0)),
                  pl.BlockSpec((1, d), lambda i: (0, 0))],
        out_specs=[pl.BlockSpec((tm, d), lambda i: (i, 0)),
                   pl.BlockSpec((tm, d), lambda i: (i, 0))],
        out_shape=[jax.ShapeDtypeStruct((n, d), F32),
                   jax.ShapeDtypeStruct((n, d), BF16)],
        compiler_params=_cparams(("parallel",)),
        name="out_proj",
    )(*parts, w_out, x_ctx, x_lat, mod_l, g2.reshape(1, d))


FFN_HALO = 16


def _ffn_kernel(h_ref, hp_ref, hn_ref, wa_ref, wb_ref, wd_ref, cw_ref, cb_ref, x1_ref, mod_ref, oc_ref, ol_ref,
                ext_ref, aext_ref, acc_ref, *, tm, nf):
    i = pl.program_id(0)
    j = pl.program_id(1)

    @pl.when(j == 0)
    def _():
        ext_ref[0:FFN_HALO, :] = hp_ref[...]
        ext_ref[FFN_HALO:FFN_HALO + tm, :] = h_ref[...]
        ext_ref[FFN_HALO + tm:2 * FFN_HALO + tm, :] = hn_ref[...]
        acc_ref[...] = jnp.zeros_like(acc_ref)

    aext_ref[...] = _dot(ext_ref[...], wa_ref[...])
    b = _dot(h_ref[...], wb_ref[...])
    row0 = i * tm
    tseq = jnp.where(row0 < N_CTX, SEQ, DEC_SEQ)
    pos = (row0 + lax.broadcasted_iota(jnp.int32, (tm, 1), 0)) & (tseq - 1)
    a_prev = jnp.where(pos == 0, 0.0, aext_ref[FFN_HALO - 1:FFN_HALO - 1 + tm, :])
    a_mid = aext_ref[FFN_HALO:FFN_HALO + tm, :]
    a_next = jnp.where(pos == tseq - 1, 0.0, aext_ref[FFN_HALO + 1:FFN_HALO + 1 + tm, :])
    a = a_prev * cw_ref[0:1, :] + a_mid * cw_ref[1:2, :] + a_next * cw_ref[2:3, :] + cb_ref[...]
    act = a * jax.nn.sigmoid(a) * b
    acc_ref[...] += _dot(act.astype(BF16), wd_ref[...])

    @pl.when((j == nf - 1) & (row0 < N_CTX))
    def _():
        oc_ref[...] = x1_ref[...] + mod_ref[5:6, :] * acc_ref[...]

    @pl.when((j == nf - 1) & (row0 >= N_CTX))
    def _():
        ol_ref[...] = x1_ref[...] + mod_ref[5:6, :] * acc_ref[...]


def conv_ffn(h2, x1, mod_l, w_up, conv_w, conv_b, w_down, layer, tm=512, tf=512):
    n, d = x1.shape
    f = w_down.shape[1]
    nf = f // tf
    nhb = n // FFN_HALO
    oc_spec, ol_spec = _split_specs(tm, d, lambda i, j: i)
    return pl.pallas_call(
        functools.partial(_ffn_kernel, tm=tm, nf=nf),
        grid=(n // tm, nf),
        in_specs=[pl.BlockSpec((tm, d), lambda i, j: (i, 0)),
                  pl.BlockSpec((FFN_HALO, d), lambda i, j: (jnp.maximum(i * (tm // FFN_HALO) - 1, 0), 0)),
                  pl.BlockSpec((FFN_HALO, d), lambda i, j: (jnp.minimum((i + 1) * (tm // FFN_HALO), nhb - 1), 0)),
                  pl.BlockSpec((None, d, tf), lambda i, j: (layer, 0, j)),
                  pl.BlockSpec((None, d, tf), lambda i, j: (layer, 0, nf + j)),
                  pl.BlockSpec((None, tf, d), lambda i, j: (layer, j, 0)),
                  pl.BlockSpec((3, tf), lambda i, j: (0, j)),
                  pl.BlockSpec((1, tf), lambda i, j: (0, j)),
                  pl.BlockSpec((tm, d), lambda i, j: (i, 0)),
                  pl.BlockSpec((None, 6, d), lambda i, j: (_mod_index(i * tm), 0, 0))],
        out_specs=[oc_spec, ol_spec],
        out_shape=[jax.ShapeDtypeStruct((N_CTX, d), F32), jax.ShapeDtypeStruct((n - N_CTX, d), F32)],
        scratch_shapes=[pltpu.VMEM((tm + 2 * FFN_HALO, d), BF16),
                        pltpu.VMEM((tm + 2 * FFN_HALO, tf), F32),
                        pltpu.VMEM((tm, d), F32)],
        compiler_params=_cparams(("arbitrary", "arbitrary")),
        name="conv_ffn",
    )(h2, h2, h2, w_up, w_up, w_down, conv_w, conv_b.reshape(1, f), x1, mod_l)


def _rope_tables():
    half = HEAD_DIM // 2
    nf = half // 2
    inv = ROPE_BASE ** (-jnp.arange(nf, dtype=F32) / nf)
    t = jnp.arange(DEC_SEQ)
    rows = (t // GRID_W).astype(F32)
    cols = (t % GRID_W).astype(F32)
    ang = jnp.concatenate([rows[:, None] * inv[None]] * 2 + [cols[:, None] * inv[None]] * 2, axis=1)
    sign = jnp.tile(jnp.concatenate([-jnp.ones(nf, F32), jnp.ones(nf, F32)]), 2)
    cos = jnp.tile(jnp.cos(ang), (DEC_BATCH, ATT_KV_HEADS))
    sin = jnp.tile(jnp.sin(ang) * sign, (DEC_BATCH, ATT_KV_HEADS))
    cos = jnp.concatenate([jnp.ones((N_CTX, 128), F32), cos], axis=0)
    sin = jnp.concatenate([jnp.zeros((N_CTX, 128), F32), sin], axis=0)
    return cos, sin


def kernel(x_prompt, x_sample, cache_k, cache_v, state_hgrn, state_s5_re, state_s5_im, c, c_ctx, norm1_g, norm2_g, ada_w, ada_b, w_in, w_out, pool_w, pool_scale, hg_lb_raw, hg_norm_g, q_norm_g, k_norm_g, att_sink, s5_a_re, s5_a_im, s5_log_dt, s5_b_re, s5_b_im, s5_c_re, s5_c_im, s5_d, s5_w_glu, ffn_w_up, ffn_conv_w, ffn_conv_b, ffn_w_down):
    d = D_MODEL
    x_ctx = x_prompt.reshape(N_CTX, d).astype(F32)
    x_lat = x_sample.reshape(N_LAT, d).astype(F32)

    cond = jnp.concatenate([c_ctx[None].astype(F32), c.astype(F32), jnp.zeros((8 - 1 - DEC_BATCH, d), F32)], axis=0)
    mod = adaln_mod(cond, ada_w, ada_b)

    lb_cum = jnp.cumsum(jax.nn.softmax(hg_lb_raw.astype(F32), axis=0), axis=0)
    hg_lb = lb_cum - lb_cum[:1]
    lp = jnp.stack([jnp.log(hg_lb), jnp.log1p(-hg_lb)], axis=2)

    cos_t, sin_t = _rope_tables()
    s0_ctx = jnp.zeros((BATCH, 2, HG_HEADS, HG_DK, HG_DK), F32)
    s5_ops = s5_operators(s5_a_re, s5_a_im, s5_log_dt, s5_b_re, s5_b_im, s5_c_re, s5_c_im)

    w_in_p = jnp.concatenate([w_in[..., :3584], w_in[..., 3840:4352], w_in[..., 3584:3840]], axis=-1).astype(BF16)
    w_out_b = w_out.astype(BF16)
    w_up_b = ffn_w_up.astype(BF16)
    w_down_b = ffn_w_down.astype(BF16)

    ks_, vs_, hs_, s5_ = [], [], [], []
    for l in range(DEPTH):
        cols = in_proj(x_ctx, x_lat, mod[l], norm1_g[l], w_in_p, l)

        y_pool = pool_mixer(cols, pool_w[l].astype(BF16), pool_scale[l])

        s0_lat = jnp.swapaxes(state_hgrn[:, l].astype(F32), -1, -2)
        o_f, o_b, sfin = hgrn_scan(cols, lp[l], s0_ctx, 0, BATCH, SEQ)
        o_f, o_b, _ = hgrn_scan(cols, lp[l], s0_lat, N_CTX, DEC_BATCH, DEC_SEQ, prev=(o_f, o_b))
        y_hg = hgrn_finish(o_f, o_b, cols, hg_norm_g[l])
        hs_.append(sfin)

        qs, kn = qk_prep(cols, cos_t, sin_t, q_norm_g[l], k_norm_g[l])
        sink = att_sink[l].astype(F32)
        y_att = attend_ctx(qs, kn, cols, sink, BATCH, SEQ)
        y_att = attend_lat(qs, kn, cols, cache_k[:, l].reshape(DEC_BATCH, PAST_LEN, 128).astype(F32),
                           cache_v[:, l].reshape(DEC_BATCH, PAST_LEN, 128).astype(F32), sink, y_att,
                           N_CTX, DEC_BATCH, DEC_SEQ)
        ks_.append(kn[:N_CTX].reshape(BATCH, SEQ, ATT_KV_HEADS, HEAD_DIM))
        vs_.append(cols[:N_CTX, 4224:4352].reshape(BATCH, SEQ, ATT_KV_HEADS, HEAD_DIM))

        h0_lat = jnp.concatenate([state_s5_re[:, l], state_s5_im[:, l]], axis=-1).astype(F32)
        h0_lat = jnp.pad(jnp.transpose(h0_lat, (2, 1, 0, 3)), ((0, 0), (0, 0), (0, 8 - DEC_BATCH), (0, 0)))
        ud = s5_gather(cols)
        ys, hf_c = s5_scan(ud, s5_ops, l, None, None, 0, BATCH, SEQ // S5_L)
        ys, = s5_scan(ud, s5_ops, l, h0_lat, ys, N_CTX, DEC_BATCH, DEC_SEQ // S5_L)
        y_s5 = s5_post(ud, ys, s5_d[l], s5_w_glu[l].astype(BF16))
        hf = hf_c.reshape(S5_GROUPS, 2, BATCH, SEQ // S5_L, 2 * S5_N)
        fin = jnp.stack([hf[:, 0, :, -1], hf[:, 1, :, 0]], axis=1)
        s5_.append(jnp.transpose(fin, (2, 1, 0, 3)))

        x1, h2 = out_proj((y_pool, y_hg, y_att, y_s5), w_out_b, l, x_ctx, x_lat, mod[l], norm2_g[l])
        x_ctx, x_lat = conv_ffn(h2, x1, mod[l], w_up_b, ffn_conv_w[l].astype(F32), ffn_conv_b[l].astype(F32),
                                w_down_b, l)

    y_prompt = x_ctx.reshape(BATCH, SEQ, d).astype(x_prompt.dtype)
    y_sample = x_lat.reshape(DEC_BATCH, DEC_SEQ, d).astype(x_sample.dtype)
    s5_all = jnp.stack(s5_, axis=1)
    return (y_prompt, y_sample, jnp.stack(ks_, axis=1), jnp.stack(vs_, axis=1), jnp.stack(hs_, axis=1),
            s5_all[..., :S5_N], s5_all[..., S5_N:])
```

```python
import functools

import jax
import jax.numpy as jnp
import numpy as np
from jax import lax
from jax.experimental import pallas as pl
from jax.experimental.pallas import tpu as pltpu

F32 = jnp.float32
BF16 = jnp.bfloat16
HIGHEST = lax.Precision.HIGHEST
LOG2E = 1.4426950408889634

D_MODEL = 2048
BATCH = 32
SEQ = 256
DEPTH = 2
DEC_BATCH = 2
DEC_SEQ = 4096
PAST_LEN = 256
GRID_W = 64
GROUP_W = 512
POOL_WINDOWS = (2, 4, 8, 16)
POOL_GW = 128
HG_HEADS = 4
HG_DK = 128
ATT_HEADS = 8
ATT_KV_HEADS = 2
ATT_GROUP = 4
HEAD_DIM = 64
WINDOW = 128
BLOCK = 128
ROPE_BASE = 10000.0
S5_CH = 16
S5_GROUPS = 32
S5_N = 64
FFN_DIM = 5632
EPS = 1e-6
NEG_INF = -1e30

N_CTX = BATCH * SEQ
N_LAT = DEC_BATCH * DEC_SEQ
N_TOK = N_CTX + N_LAT
IN_COLS = 4352

CB_POOL, CB_HQ, CB_FF, CB_FB, CB_HI, CB_HG, CB_AQ, CB_S5 = range(8)
CB_AK, CB_AV = 32, 33

S5_L = 16
HG_C = 128
VMEM_LIMIT = 56 * 1024 * 1024


def _cparams(sem):
    return pltpu.CompilerParams(dimension_semantics=sem, vmem_limit_bytes=VMEM_LIMIT)


def _dot(a, b):
    return jnp.dot(a, b, preferred_element_type=F32)


def _dot_nt(a, b):
    return lax.dot_general(a, b, (((1,), (1,)), ((), ())), preferred_element_type=F32)


def _dot_tn(a, b):
    return lax.dot_general(a, b, (((0,), (0,)), ((), ())), preferred_element_type=F32)


def _split3(x):
    hi = x.astype(BF16)
    r1 = x - hi.astype(F32)
    mid = r1.astype(BF16)
    lo = (r1 - mid.astype(F32)).astype(BF16)
    return hi, mid, lo


def _dot_exact01(x, b01):
    hi, mid, lo = _split3(x)
    return _dot(hi, b01) + _dot(mid, b01) + _dot(lo, b01)


def _mod_index(row0):
    return jnp.where(row0 < N_CTX, 0, 1 + (row0 - N_CTX) // DEC_SEQ)


def _mod_kernel(cond_ref, w_ref, b_ref, o_ref):
    x = cond_ref[...]
    s = x * jax.nn.sigmoid(x)
    o_ref[...] = _dot(s.astype(BF16), w_ref[...].astype(BF16)) + b_ref[...]


def adaln_mod(cond, ada_w, ada_b, tn=1024):
    depth, d, n6 = ada_w.shape
    out = pl.pallas_call(
        _mod_kernel,
        grid=(depth, n6 // tn),
        in_specs=[pl.BlockSpec((8, d), lambda l, j: (0, 0)),
                  pl.BlockSpec((None, d, tn), lambda l, j: (l, 0, j)),
                  pl.BlockSpec((None, 1, tn), lambda l, j: (l, 0, j))],
        out_specs=pl.BlockSpec((None, 8, tn), lambda l, j: (l, 0, j)),
        out_shape=jax.ShapeDtypeStruct((depth, 8, n6), F32),
        compiler_params=_cparams(("parallel", "parallel")),
        name="adaln_mod",
    )(cond, ada_w, ada_b.reshape(depth, 1, n6))
    return out.reshape(depth, 8, 6, d)


def _split_specs(tm, d, row_tile):
    nc = N_CTX // tm
    return (pl.BlockSpec((tm, d), lambda *g: (jnp.minimum(row_tile(*g), nc - 1), 0)),
            pl.BlockSpec((tm, d), lambda *g: (jnp.maximum(row_tile(*g) - nc, 0), 0)))


def _inproj_kernel(xc_ref, xl_ref, mod_ref, g_ref, w_ref, o_ref, *, tm):
    x = jnp.where(pl.program_id(1) * tm < N_CTX, xc_ref[...], xl_ref[...])
    ms = jnp.mean(x * x, axis=-1, keepdims=True)
    h = x * lax.rsqrt(ms + EPS) * g_ref[...]
    h = h * (1.0 + mod_ref[1:2, :]) + mod_ref[0:1, :]
    o_ref[...] = _dot(h.astype(BF16), w_ref[...])


def in_proj(x_ctx, x_lat, mod_l, g1, w_in, layer, tm=256):
    d = x_ctx.shape[1]
    n = x_ctx.shape[0] + x_lat.shape[0]
    ncols = w_in.shape[2]
    tn = ncols
    xc_spec, xl_spec = _split_specs(tm, d, lambda j, i: i)
    return pl.pallas_call(
        functools.partial(_inproj_kernel, tm=tm),
        grid=(ncols // tn, n // tm),
        in_specs=[xc_spec, xl_spec,
                  pl.BlockSpec((None, 6, d), lambda j, i: (_mod_index(i * tm), 0, 0)),
                  pl.BlockSpec((1, d), lambda j, i: (0, 0)),
                  pl.BlockSpec((None, d, tn), lambda j, i: (layer, 0, j), pipeline_mode=pl.Buffered(1))],
        out_specs=pl.BlockSpec((tm, tn), lambda j, i: (i, j)),
        out_shape=jax.ShapeDtypeStruct((n, ncols), F32),
        compiler_params=_cparams(("parallel", "parallel")),
        name="in_proj",
    )(x_ctx, x_lat, mod_l, g1.reshape(1, d), w_in)


def _pool_kernel(u_ref, up_ref, un_ref, w_ref, sc_ref, o_ref, ext_ref, *, tm):
    i = pl.program_id(0)
    row0 = i * tm
    tseq = jnp.where(row0 < N_CTX, SEQ, DEC_SEQ)
    ext_ref[0:8, :] = up_ref[...]
    ext_ref[8:8 + tm, :] = u_ref[...]
    ext_ref[8 + tm:16 + tm, :] = un_ref[...]
    pos = (row0 + lax.broadcasted_iota(jnp.int32, (tm, 1), 0)) & (tseq - 1)
    outs = []
    for gi, w in enumerate(POOL_WINDOWS):
        ls = slice(gi * POOL_GW, (gi + 1) * POOL_GW)
        acc = None
        for o in range(-w // 2, w // 2):
            inside = lax.bitcast_convert_type(pos + o, jnp.uint32) < tseq.astype(jnp.uint32)
            term = jnp.where(inside, ext_ref[8 + o:8 + o + tm, ls], 0.0)
            acc = term if acc is None else acc + term
        lo = jnp.maximum(pos - w // 2, 0)
        hi = jnp.minimum(pos + w // 2, tseq)
        cnt = (hi - lo).astype(F32)
        pooled = acc / cnt - ext_ref[8:8 + tm, ls]
        outs.append(_dot(pooled.astype(BF16), w_ref[gi]))
    o_ref[...] = (jnp.concatenate(outs, axis=1) * sc_ref[...]).astype(o_ref.dtype)


def pool_mixer(cols, pool_w, pool_scale, tm=1024):
    n = cols.shape[0]
    nb8 = n // 8
    return pl.pallas_call(
        functools.partial(_pool_kernel, tm=tm),
        grid=(n // tm,),
        in_specs=[pl.BlockSpec((tm, GROUP_W), lambda i: (i, CB_POOL)),
                  pl.BlockSpec((8, GROUP_W), lambda i: (jnp.maximum(i * (tm // 8) - 1, 0), CB_POOL)),
                  pl.BlockSpec((8, GROUP_W), lambda i: (jnp.minimum((i + 1) * (tm // 8), nb8 - 1), CB_POOL)),
                  pl.BlockSpec((4, POOL_GW, POOL_GW), lambda i: (0, 0, 0)),
                  pl.BlockSpec((1, GROUP_W), lambda i: (0, 0))],
        out_specs=pl.BlockSpec((tm, GROUP_W), lambda i: (i, 0)),
        out_shape=jax.ShapeDtypeStruct((n, GROUP_W), BF16),
        scratch_shapes=[pltpu.VMEM((tm + 16, GROUP_W), F32)],
        compiler_params=_cparams(("parallel",)),
        name="pool_mixer",
    )(cols, cols, cols, pool_w, pool_scale.reshape(1, GROUP_W))


def _block_ref_rows(x, c, m, rev):
    w = x.shape[1]
    if 2 * m >= 8:
        r = m if rev else m - 1
        xr = x.reshape(c // (2 * m), 2 * m, w)[:, r:r + 1, :]
        return jnp.broadcast_to(xr, (c // (2 * m), 2 * m, w)).reshape(c, w)
    x8 = x.reshape(c // 8, 8, w)
    sub = lax.broadcasted_iota(jnp.int32, (c // 8, 8, w), 1)
    out = None
    for kb in range(8 // (2 * m)):
        r = kb * 2 * m + (m if rev else m - 1)
        cand = jnp.broadcast_to(x8[:, r:r + 1, :], (c // 8, 8, w))
        out = cand if out is None else jnp.where(sub >= kb * 2 * m, cand, out)
    return out.reshape(c, w)


def _hgrn_chunk(q, fx, v, log_l, log1m_l, st_ref, rev, c):
    ls = jnp.minimum(fx, 0.0) - jnp.log(1.0 + jnp.exp(-jnp.abs(fx)))
    bb = log1m_l + ls
    logf = jnp.maximum(log_l, bb) + jnp.log(1.0 + jnp.exp(-jnp.abs(log_l - bb)))
    kk = 1.0 - jnp.exp(logf)
    row = lax.broadcasted_iota(jnp.int32, (c, c), 0)
    col = lax.broadcasted_iota(jnp.int32, (c, c), 1)
    causal = (row <= col) if rev else (row >= col)
    tri = jnp.where(causal, 1.0, 0.0).astype(BF16)
    hi, mid, lo = _split3(logf)
    b = (_dot(tri, hi) + _dot(tri, mid) + _dot(tri, lo)) * LOG2E
    b_last = b[0:1] if rev else b[c - 1:c]
    qt = (q * jnp.exp2(b)).astype(BF16)
    kd = (kk * jnp.exp2(b_last - b)).astype(BF16)
    dl = jnp.exp2(b_last)
    xor = row ^ col
    tbit = lax.broadcasted_iota(jnp.int32, (c, 1), 0)
    levels = []
    m = c // 2
    while m >= 1:
        query_side = ((tbit & m) == 0) if rev else ((tbit & m) != 0)
        wgt = jnp.exp2((b - _block_ref_rows(b, c, m, rev)) * jnp.where(query_side, 1.0, -1.0))
        levels.append(((jnp.where(query_side, q, kk) * wgt).astype(BF16), (xor >> (m.bit_length() - 1)) == 1))
        m //= 2
    qb = q.astype(BF16)
    kb = kk.astype(BF16)
    vb = v.astype(BF16)
    outs = []
    for h in range(HG_HEADS):
        sl = slice(h * HG_DK, (h + 1) * HG_DK)
        att = _dot_nt(qb[:, sl], kb[:, sl])
        for z, msk in levels:
            att = jnp.where(msk, _dot_nt(z[:, sl], z[:, sl]), att)
        att = jnp.where(causal, att, 0.0)
        intra = _dot(att.astype(BF16), vb[:, sl])
        st = st_ref[h]
        inter = _dot_nt(qt[:, sl], st.astype(BF16))
        st_ref[h] = st * dl[:, sl] + _dot_tn(vb[:, sl], kd[:, sl])
        outs.append(intra + inter)
    return jnp.concatenate(outs, axis=1)


def _hgrn_kernel(qf_ref, ff_ref, vf_ref, qb_ref, fb_ref, vb_ref, lp_ref, s0_ref, *rest, c, nchunks):
    of_ref, ob_ref, sfin_ref, st_ref = rest[-4:]
    ci = pl.program_id(1)

    @pl.when(ci == 0)
    def _():
        st_ref[...] = s0_ref[...]

    of_ref[...] = _hgrn_chunk(qf_ref[...], ff_ref[...], vf_ref[...], lp_ref[0, 0:1, :], lp_ref[0, 1:2, :],
                              st_ref.at[0], False, c)
    ob_ref[...] = _hgrn_chunk(qb_ref[...], fb_ref[...], vb_ref[...], lp_ref[1, 0:1, :], lp_ref[1, 1:2, :],
                              st_ref.at[1], True, c)

    @pl.when(ci == nchunks - 1)
    def _():
        for d in range(2):
            for h in range(HG_HEADS):
                sfin_ref[d, h] = st_ref[d, h].T


def hgrn_scan(cols, lp, s0_t, row_off, nseq, t, prev=None, c=HG_C):
    n = cols.shape[0]
    nchunks = t // c
    base = row_off // c
    extra = [] if prev is None else list(prev)
    any_spec = pl.BlockSpec(memory_space=pl.ANY)

    def fwd(cb):
        return pl.BlockSpec((c, GROUP_W), lambda s, i: (base + s * nchunks + i, cb))

    def bwd(cb):
        return pl.BlockSpec((c, GROUP_W), lambda s, i: (base + s * nchunks + nchunks - 1 - i, cb))

    st_spec = pl.BlockSpec((None, 2, HG_HEADS, HG_DK, HG_DK), lambda s, i: (s, 0, 0, 0, 0))
    return pl.pallas_call(
        functools.partial(_hgrn_kernel, c=c, nchunks=nchunks),
        grid=(nseq, nchunks),
        in_specs=[fwd(CB_HQ), fwd(CB_FF), fwd(CB_HI), bwd(CB_HQ), bwd(CB_FB), bwd(CB_HI),
                  pl.BlockSpec((2, 2, GROUP_W), lambda s, i: (0, 0, 0)), st_spec] + [any_spec] * len(extra),
        out_specs=[fwd(0), bwd(0), st_spec],
        out_shape=[jax.ShapeDtypeStruct((n, GROUP_W), F32),
                   jax.ShapeDtypeStruct((n, GROUP_W), F32),
                   jax.ShapeDtypeStruct((nseq, 2, HG_HEADS, HG_DK, HG_DK), F32)],
        input_output_aliases={8 + k: k for k in range(len(extra))},
        scratch_shapes=[pltpu.VMEM((2, HG_HEADS, HG_DK, HG_DK), F32)],
        compiler_params=_cparams(("parallel", "arbitrary")),
        name="hgrn_scan",
    )(cols, cols, cols, cols, cols, cols, lp, s0_t, *extra)


def _hgrn_finish(o_f, o_b, g, norm_g):
    o = o_f + o_b
    gate = g * jax.nn.sigmoid(g)
    outs = []
    for h in range(HG_HEADS):
        oh = o[:, h * HG_DK:(h + 1) * HG_DK]
        ms = jnp.mean(oh * oh, axis=-1, keepdims=True)
        outs.append(oh * lax.rsqrt(ms + EPS) * norm_g)
    return (jnp.concatenate(outs, axis=1) * gate).astype(BF16)


def _swap16(x):
    w = x.shape[1]
    lane = lax.broadcasted_iota(jnp.int32, x.shape, 1)
    return jnp.where((lane & 31) < 16, pltpu.roll(x, w - 16, 1), pltpu.roll(x, 16, 1))


def _qkprep_kernel(q_ref, k_ref, cos_ref, sin_ref, gq_ref, gk_ref, bq_ref, qo_ref, ko_ref):
    cos = cos_ref[...]
    sin = sin_ref[...]
    q = q_ref[...]
    msq = _dot_exact01(q * q, bq_ref[...]) * (1.0 / HEAD_DIM)
    qn = q * lax.rsqrt(msq + EPS) * gq_ref[...]
    cos4 = jnp.concatenate([cos] * 4, axis=1)
    sin4 = jnp.concatenate([sin] * 4, axis=1)
    qr = qn * cos4 + _swap16(qn) * sin4
    qo_ref[...] = (qr * (HEAD_DIM ** -0.5)).astype(qo_ref.dtype)
    k = k_ref[...]
    msk = _dot_exact01(k * k, bq_ref[0:128, 0:128]) * (1.0 / HEAD_DIM)
    kn = k * lax.rsqrt(msk + EPS) * gk_ref[...]
    ko_ref[...] = kn * cos + _swap16(kn) * sin


def qk_prep(cols, cos_t, sin_t, q_norm_g, k_norm_g, tm=512):
    n = cols.shape[0]
    head = np.arange(GROUP_W) // HEAD_DIM
    bones = jnp.asarray((head[:, None] == head[None, :]).astype(np.float32), BF16)
    return pl.pallas_call(
        _qkprep_kernel,
        grid=(n // tm,),
        in_specs=[pl.BlockSpec((tm, GROUP_W), lambda i: (i, CB_AQ)),
                  pl.BlockSpec((tm, 128), lambda i: (i, CB_AK)),
                  pl.BlockSpec((tm, 128), lambda i: (i, 0)),
                  pl.BlockSpec((tm, 128), lambda i: (i, 0)),
                  pl.BlockSpec((1, GROUP_W), lambda i: (0, 0)),
                  pl.BlockSpec((1, 128), lambda i: (0, 0)),
                  pl.BlockSpec((GROUP_W, GROUP_W), lambda i: (0, 0))],
        out_specs=[pl.BlockSpec((tm, GROUP_W), lambda i: (i, 0)),
                   pl.BlockSpec((tm, 128), lambda i: (i, 0))],
        out_shape=[jax.ShapeDtypeStruct((n, GROUP_W), BF16),
                   jax.ShapeDtypeStruct((n, 128), F32)],
        compiler_params=_cparams(("parallel",)),
        name="qk_prep",
    )(cols, cols, cos_t, sin_t, jnp.tile(q_norm_g, ATT_HEADS).reshape(1, GROUP_W),
      jnp.tile(k_norm_g, ATT_KV_HEADS).reshape(1, 128), bones)


def _attn_ctx_kernel(sink_ref, q_ref, k_ref, v_ref, o_ref, *, t, spb):
    for sq in range(spb):
        rs = slice(sq * t, (sq + 1) * t)
        q = q_ref[rs, :]
        k = k_ref[rs, :].astype(BF16)
        v = v_ref[rs, :].astype(BF16)
        pieces = [None] * ATT_HEADS
        for kh in range(ATT_KV_HEADS):
            hs = slice(kh * HEAD_DIM, (kh + 1) * HEAD_DIM)
            for g in range(ATT_GROUP):
                hq = kh * ATT_GROUP + g
                s = _dot_nt(q[:, hq * HEAD_DIM:(hq + 1) * HEAD_DIM], k[:, hs])
                sk = sink_ref[hq]
                mx = jnp.maximum(jnp.max(s, axis=1, keepdims=True), sk)
                p = jnp.exp(s - mx)
                den = jnp.sum(p, axis=1, keepdims=True) + jnp.exp(sk - mx)
                pieces[hq] = _dot(p.astype(BF16), v[:, hs]) / den
        o_ref[rs, :] = jnp.concatenate(pieces, axis=1).astype(o_ref.dtype)


def attend_ctx(qs, kn, cols, sink, nseq, t, spb=2):
    return pl.pallas_call(
        functools.partial(_attn_ctx_kernel, t=t, spb=spb),
        grid=(nseq // spb,),
        in_specs=[pl.BlockSpec(memory_space=pltpu.SMEM),
                  pl.BlockSpec((spb * t, GROUP_W), lambda s: (s, 0)),
                  pl.BlockSpec((spb * t, 128), lambda s: (s, 0)),
                  pl.BlockSpec((spb * t, 128), lambda s: (s, CB_AV))],
        out_specs=pl.BlockSpec((spb * t, GROUP_W), lambda s: (s, 0)),
        out_shape=jax.ShapeDtypeStruct((qs.shape[0], GROUP_W), BF16),
        compiler_params=_cparams(("parallel",)),
        name="attend_ctx",
    )(sink, qs, kn, cols)


def _attn_lat_kernel(sink_ref, q_ref, *refs, t, blk, nk):
    k_refs, v_refs = refs[:nk], refs[nk:2 * nk]
    ck_ref, cv_ref, _, o_ref = refs[2 * nk:]
    i = pl.program_id(1)
    q = q_ref[...]
    kl = jnp.concatenate([r[...] for r in k_refs], axis=0).astype(BF16)
    vl = jnp.concatenate([r[...] for r in v_refs], axis=0).astype(BF16)
    ck = ck_ref[...].astype(BF16)
    cv = cv_ref[...].astype(BF16)
    r_idx = lax.broadcasted_iota(jnp.int32, (blk, nk * WINDOW), 0)
    c_idx = lax.broadcasted_iota(jnp.int32, (blk, nk * WINDOW), 1)
    kpos = i * blk - WINDOW + lax.broadcasted_iota(jnp.int32, (1, nk * WINDOW), 1)
    col_bias = jnp.where(kpos >= 0, jnp.where(kpos < t, 0.0, NEG_INF), NEG_INF)
    bias = jnp.where(r_idx <= c_idx, jnp.where(r_idx >= c_idx - 2 * WINDOW, col_bias, NEG_INF), NEG_INF)
    pieces = [None] * ATT_HEADS
    for kh in range(ATT_KV_HEADS):
        hs = slice(kh * HEAD_DIM, (kh + 1) * HEAD_DIM)
        for g in range(ATT_GROUP):
            hq = kh * ATT_GROUP + g
            qh = q[:, hq * HEAD_DIM:(hq + 1) * HEAD_DIM]
            s_loc = _dot_nt(qh, kl[:, hs]) + bias
            s_ctx = _dot_nt(qh, ck[:, hs])
            sk = sink_ref[hq]
            mx = jnp.maximum(jnp.maximum(jnp.max(s_loc, axis=1, keepdims=True),
                                         jnp.max(s_ctx, axis=1, keepdims=True)), sk)
            p_loc = jnp.exp(s_loc - mx)
            p_ctx = jnp.exp(s_ctx - mx)
            den = (jnp.sum(p_loc, axis=1, keepdims=True) + jnp.sum(p_ctx, axis=1, keepdims=True)
                   + jnp.exp(sk - mx))
            pieces[hq] = (_dot(p_loc.astype(BF16), vl[:, hs]) + _dot(p_ctx.astype(BF16), cv[:, hs])) / den
    o_ref[...] = jnp.concatenate(pieces, axis=1).astype(o_ref.dtype)


def attend_lat(qs, kn, cols, cache_k, cache_v, sink, prev, row_off, nseq, t, blk=2 * WINDOW):
    nb = t // blk
    base = row_off // blk
    lctx = cache_k.shape[1]
    kpb = blk // WINDOW
    nk = kpb + 2
    nkb = t // WINDOW
    kbase = row_off // WINDOW

    def nbr(cb, j):
        return pl.BlockSpec((WINDOW, 128),
                            lambda b, i: (kbase + b * nkb + jnp.clip(i * kpb - 1 + j, 0, nkb - 1), cb))

    return pl.pallas_call(
        functools.partial(_attn_lat_kernel, t=t, blk=blk, nk=nk),
        grid=(nseq, nb),
        in_specs=[pl.BlockSpec(memory_space=pltpu.SMEM),
                  pl.BlockSpec((blk, GROUP_W), lambda b, i: (base + b * nb + i, 0))]
                 + [nbr(0, j) for j in range(nk)] + [nbr(CB_AV, j) for j in range(nk)]
                 + [pl.BlockSpec((None, lctx, 128), lambda b, i: (b, 0, 0)),
                    pl.BlockSpec((None, lctx, 128), lambda b, i: (b, 0, 0)),
                    pl.BlockSpec(memory_space=pl.ANY)],
        out_specs=pl.BlockSpec((blk, GROUP_W), lambda b, i: (base + b * nb + i, 0)),
        out_shape=jax.ShapeDtypeStruct(prev.shape, BF16),
        input_output_aliases={4 + 2 * nk: 0},
        compiler_params=_cparams(("parallel", "parallel")),
        name="attend_lat",
    )(sink, qs, *([kn] * nk), *([cols] * nk), cache_k, cache_v, prev)


def s5_operators(a_re, a_im, log_dt, b_re, b_im, c_re, c_im):
    L = S5_L
    dt = jnp.exp(log_dt.astype(F32))[..., None]
    are, aim = a_re.astype(F32), a_im.astype(F32)
    adr, adi = (are * dt)[..., None, :], (aim * dt)[..., None, :]

    def lam_pow(p):
        p = jnp.asarray(p, F32)[None, :, None, :, None]
        mag = jnp.exp(adr * p)
        return mag * jnp.cos(adi * p), mag * jnp.sin(adi * p)

    ar = np.arange(L)
    one = np.ones((2, 1))
    lre, lim = lam_pow(one)
    lre, lim = lre[..., 0, :], lim[..., 0, :]
    den = are * are + aim * aim
    fre = ((lre - 1.0) * are + lim * aim) / den
    fim = (lim * are - (lre - 1.0) * aim) / den
    bre = fre[..., None] * b_re - fim[..., None] * b_im
    bim = fre[..., None] * b_im + fim[..., None] * b_re
    bret, bimt = jnp.swapaxes(bre, -1, -2), jnp.swapaxes(bim, -1, -2)
    cre, cim = c_re.astype(F32)[..., None, :, :], c_im.astype(F32)[..., None, :, :]
    ein = functools.partial(jnp.einsum, precision=HIGHEST)

    def c_lam(p):
        pr, pi = lam_pow(p)
        pr, pi = pr[..., None, :], pi[..., None, :]
        return cre * pr - cim * pi, cre * pi + cim * pr

    cpr, cpi = c_lam(one * ar)
    ktau = ein('ldgtcn,ldgne->ldgtce', cpr, bre) - ein('ldgtcn,ldgne->ldgtce', cpi, bim)
    lw = L * S5_CH
    rowf = jnp.transpose(ktau[:, 0], (0, 1, 4, 2, 3)).reshape(ktau.shape[0], -1, S5_CH, lw)
    rowb = jnp.transpose(jnp.flip(ktau[:, 1], axis=2), (0, 1, 4, 2, 3)).reshape(ktau.shape[0], -1, S5_CH, lw)
    zeros = jnp.zeros_like(rowf)
    padf = jnp.concatenate([zeros, rowf], axis=-1)
    padb = jnp.concatenate([rowb, zeros], axis=-1)
    kf = jnp.stack([padf[..., lw - S5_CH * s:2 * lw - S5_CH * s] for s in range(L)], axis=2)
    kb = jnp.stack([padb[..., S5_CH * (L - 1 - s):S5_CH * (L - 1 - s) + lw] for s in range(L)], axis=2)
    kmat = jnp.stack([kf, kb], axis=1).reshape(ktau.shape[:3] + (lw, lw))
    pr, pi = lam_pow(np.stack([L - 1 - ar, ar]))
    pr, pi = pr[..., None, :], pi[..., None, :]
    wr = pr * bret[..., None, :, :] - pi * bimt[..., None, :, :]
    wi = pr * bimt[..., None, :, :] + pi * bret[..., None, :, :]
    wst = jnp.concatenate([wr, wi], axis=-1).reshape(ktau.shape[:3] + (L * S5_CH, 2 * S5_N))
    orr, oii = c_lam(np.stack([ar + 1, L - ar]))
    wout = jnp.concatenate([jnp.moveaxis(orr, -1, -3), -jnp.moveaxis(oii, -1, -3)], axis=-3)
    wout = wout.reshape(ktau.shape[:3] + (2 * S5_N, L * S5_CH))
    jre, jim = lam_pow(one * (L * 2.0 ** np.arange(9)))
    lam = jnp.stack([jnp.concatenate([jre, jre], axis=-1), jnp.concatenate([-jim, jim], axis=-1)], axis=-2)
    return kmat.astype(BF16), wst.astype(BF16), wout.astype(BF16), lam


S5_BUNDLE = 128 // S5_CH


def _s5_group(gl, u, km_ref, ws_ref, wo_ref, lam_ref, h0_ref, hfin_ref, seglen, nseq):
    r = nseq * seglen
    rowi = lax.broadcasted_iota(jnp.int32, (r, 1), 0)
    cl = rowi & (seglen - 1)

    def cmul(d, j, x):
        return lam_ref[d, gl, j, 0:1, :] * x + lam_ref[d, gl, j, 1:2, :] * pltpu.roll(x, S5_N, 1)

    y = None
    for d in range(2):
        edge = 0 if d == 0 else seglen - 1
        h0rows = jnp.zeros((r, 2 * S5_N), F32)
        x = _dot(u, ws_ref[d, gl])
        if h0_ref is not None:
            for s in range(nseq):
                h0rows = jnp.where(rowi == s * seglen + edge, h0_ref[gl, d, s:s + 1, :], h0rows)
            x = x + cmul(d, 0, h0rows)
        sh, j = 1, 0
        while sh < seglen:
            if d == 0:
                xs, ok = pltpu.roll(x, sh, 0), cl >= sh
            else:
                xs, ok = pltpu.roll(x, r - sh, 0), cl < seglen - sh
            x = x + jnp.where(ok, cmul(d, j, xs), 0.0)
            sh, j = sh * 2, j + 1
        if hfin_ref is not None:
            hfin_ref[gl, d] = x
        if d == 0:
            hin = jnp.where(cl >= 1, pltpu.roll(x, 1, 0), h0rows)
        else:
            hin = jnp.where(cl < seglen - 1, pltpu.roll(x, r - 1, 0), h0rows)
        yd = _dot(u, km_ref[d, gl]) + _dot(hin.astype(BF16), wo_ref[d, gl])
        y = yd if y is None else y + yd
    return y


def _s5_kernel(*refs, seglen, nseq, has_h0, has_prev, has_fin):
    u_ref, km_ref, ws_ref, wo_ref, lam_ref = refs[:5]
    pos = 5
    h0_ref = refs[pos] if has_h0 else None
    pos += has_h0 + has_prev
    y_ref = refs[pos]
    hfin_ref = refs[pos + 1] if has_fin else None
    ycat_ref, rin_ref, rout_ref = refs[-3:]
    lane_blk = lax.broadcasted_iota(jnp.int32, (1, 128), 1) // S5_CH
    nv = S5_L // S5_BUNDLE
    nb = S5_BUNDLE

    def merge(pick):
        acc = pick(0)
        for b in range(1, nb):
            acc = jnp.where(lane_blk == b, pick(b), acc)
        return acc

    for v in range(nv):
        for s in range(nb):
            m = merge(lambda g: u_ref[v * nb + (g + s) % nb])
            rin_ref[v, s] = pltpu.roll(m, s * S5_CH, 1) if s else m

    def body(gl, carry):
        halves = [merge(lambda k: rin_ref[v, (k - gl) & (nb - 1)]) for v in range(nv)]
        u = jnp.concatenate(halves, axis=1).astype(BF16)
        ycat_ref[gl] = _s5_group(gl, u, km_ref, ws_ref, wo_ref, lam_ref, h0_ref, hfin_ref, seglen, nseq)
        return carry

    lax.fori_loop(0, nb, body, 0)
    for v in range(nv):
        for s in range(nb):
            m = merge(lambda k: ycat_ref[(k + s) % nb, :, v * 128:(v + 1) * 128])
            rout_ref[s] = pltpu.roll(m, s * S5_CH, 1) if s else m
        for k in range(nb):
            y_ref[v * nb + k] = merge(lambda g: rout_ref[(g - k) % nb])


S5_TILE = 512


def _s5_perm():
    p = np.arange(S5_TILE)
    src = (p % (S5_TILE // S5_L)) * S5_L + p // (S5_TILE // S5_L)
    return (src[:, None] == np.arange(S5_TILE)[None, :]).astype(np.float32)


def _s5_gather_kernel(u_ref, p_ref, o_ref):
    p = p_ref[...]
    hi, mid, lo = _split3(u_ref[...])
    o_ref[...] = (_dot(p, hi) + _dot(p, mid) + _dot(p, lo)).reshape(o_ref.shape)


def s5_gather(cols):
    n = cols.shape[0]
    cpt = S5_TILE // S5_L
    return pl.pallas_call(
        _s5_gather_kernel,
        grid=(n // S5_TILE,),
        in_specs=[pl.BlockSpec((S5_TILE, GROUP_W), lambda i: (i, CB_S5)),
                  pl.BlockSpec((S5_TILE, S5_TILE), lambda i: (0, 0))],
        out_specs=pl.BlockSpec((S5_L, cpt, GROUP_W), lambda i: (0, i, 0)),
        out_shape=jax.ShapeDtypeStruct((S5_L, n // S5_L, GROUP_W), F32),
        compiler_params=_cparams(("parallel",)),
        name="s5_gather",
    )(cols, jnp.asarray(_s5_perm(), BF16))


def s5_scan(ud, ops, layer, h0, prev, row_off, nseq, seglen):
    kmat, wst, wout, lam = ops
    nchunks = ud.shape[1]
    r = nseq * seglen
    rb = row_off // (r * S5_L)
    assert row_off % (r * S5_L) == 0
    lw = S5_L * S5_CH
    nb = S5_BUNDLE
    in_specs = [pl.BlockSpec((S5_L, r, 128), lambda j: (0, rb, j)),
                pl.BlockSpec((None, 2, nb, lw, lw), lambda j: (layer, 0, j, 0, 0)),
                pl.BlockSpec((None, 2, nb, lw, 2 * S5_N), lambda j: (layer, 0, j, 0, 0)),
                pl.BlockSpec((None, 2, nb, 2 * S5_N, lw), lambda j: (layer, 0, j, 0, 0)),
                pl.BlockSpec((None, 2, nb, 9, 2, 2 * S5_N), lambda j: (layer, 0, j, 0, 0, 0))]
    args = [ud, kmat, wst, wout, lam]
    if h0 is not None:
        in_specs.append(pl.BlockSpec((nb, 2, h0.shape[2], 2 * S5_N), lambda j: (j, 0, 0, 0)))
        args.append(h0)
    aliases = {}
    if prev is not None:
        aliases = {len(args): 0}
        in_specs.append(pl.BlockSpec(memory_space=pl.ANY))
        args.append(prev)
    out_specs = [pl.BlockSpec((S5_L, r, 128), lambda j: (0, rb, j))]
    out_shape = [jax.ShapeDtypeStruct((S5_L, nchunks, GROUP_W), F32)]
    has_fin = prev is None
    if has_fin:
        out_specs.append(pl.BlockSpec((nb, 2, r, 2 * S5_N), lambda j: (j, 0, 0, 0)))
        out_shape.append(jax.ShapeDtypeStruct((S5_GROUPS, 2, r, 2 * S5_N), F32))
    return pl.pallas_call(
        functools.partial(_s5_kernel, seglen=seglen, nseq=nseq, has_h0=h0 is not None,
                          has_prev=prev is not None, has_fin=has_fin),
        grid=(S5_GROUPS // nb,),
        in_specs=in_specs,
        out_specs=out_specs,
        out_shape=out_shape,
        input_output_aliases=aliases,
        scratch_shapes=[pltpu.VMEM((nb, r, lw), F32), pltpu.VMEM((S5_L // nb, nb, r, 128), F32),
                        pltpu.VMEM((nb, r, 128), F32)],
        compiler_params=_cparams(("parallel",)),
        name="s5_scan",
    )(*args)


def _s5_post_kernel(u_ref, y_ref, p_ref, d_ref, w_ref, o_ref):
    out = (d_ref[...] * u_ref[...] + y_ref[...]).reshape(S5_TILE, GROUP_W)
    yy = jax.nn.gelu(out)
    zz = _dot(yy.astype(BF16), w_ref[...])
    res = (zz[:, :GROUP_W] * jax.nn.sigmoid(zz[:, GROUP_W:])).astype(BF16)
    o_ref[...] = _dot_tn(p_ref[...], res).astype(o_ref.dtype)


def s5_post(ud, y, s5_d, w_glu):
    n = ud.shape[1] * S5_L
    cpt = S5_TILE // S5_L
    return pl.pallas_call(
        _s5_post_kernel,
        grid=(n // S5_TILE,),
        in_specs=[pl.BlockSpec((S5_L, cpt, GROUP_W), lambda i: (0, i, 0)),
                  pl.BlockSpec((S5_L, cpt, GROUP_W), lambda i: (0, i, 0)),
                  pl.BlockSpec((S5_TILE, S5_TILE), lambda i: (0, 0)),
                  pl.BlockSpec((1, GROUP_W), lambda i: (0, 0)),
                  pl.BlockSpec((GROUP_W, 2 * GROUP_W), lambda i: (0, 0))],
        out_specs=pl.BlockSpec((S5_TILE, GROUP_W), lambda i: (i, 0)),
        out_shape=jax.ShapeDtypeStruct((n, GROUP_W), BF16),
        compiler_params=_cparams(("parallel",)),
        name="s5_post",
    )(ud, y, jnp.asarray(_s5_perm(), BF16), s5_d.reshape(1, GROUP_W), w_glu)


def _outproj_kernel(p0_ref, of_ref, ob_ref, hg_ref, ng_ref, p2_ref, p3_ref, w_ref, xc_ref, xl_ref, mod_ref, g_ref,
                    x1_ref, h2_ref, *, tm):
    y = _dot(p0_ref[...], w_ref[0:GROUP_W, :])
    y_hg = _hgrn_finish(of_ref[...], ob_ref[...], hg_ref[...], ng_ref[...])
    y = y + _dot(y_hg, w_ref[GROUP_W:2 * GROUP_W, :])
    y = y + _dot(p2_ref[...], w_ref[2 * GROUP_W:3 * GROUP_W, :])
    y = y + _dot(p3_ref[...], w_ref[3 * GROUP_W:4 * GROUP_W, :])
    x = jnp.where(pl.program_id(0) * tm < N_CTX, xc_ref[...], xl_ref[...])
    x1 = x + mod_ref[2:3, :] * y
    x1_ref[...] = x1
    ms = jnp.mean(x1 * x1, axis=-1, keepdims=True)
    h = x1 * lax.rsqrt(ms + EPS) * g_ref[...]
    h2_ref[...] = (h * (1.0 + mod_ref[4:5, :]) + mod_ref[3:4, :]).astype(h2_ref.dtype)


def out_proj(y_pool, hgrn, y_att, y_s5, w_out, layer, x_ctx, x_lat, mod_l, g2, tm=512):
    o_f, o_b, cols, norm_g = hgrn
    d = x_ctx.shape[1]
    n = x_ctx.shape[0] + x_lat.shape[0]
    part_spec = pl.BlockSpec((tm, GROUP_W), lambda i: (i, 0))
    xc_spec, xl_spec = _split_specs(tm, d, lambda i: i)
    return pl.pallas_call(
        functools.partial(_outproj_kernel, tm=tm),
        grid=(n // tm,),
        in_specs=[part_spec, part_spec, part_spec,
                  pl.BlockSpec((tm, GROUP_W), lambda i: (i, CB_HG)),
                  pl.BlockSpec((1, HG_DK), lambda i: (0, 0)),
                  part_spec, part_spec,
                  pl.BlockSpec((None, 4 * GROUP_W, d), lambda i: (layer, 0, 0)),
                  xc_spec, xl_spec,
                  pl.BlockSpec((None, 6, d), lambda i: (_mod_index(i * tm), 0, 0)),
                  pl.BlockSpec((1, d), lambda i: (0, 0))],
        out_specs=[pl.BlockSpec((tm, d), lambda i: (i, 0)),
                   pl.BlockSpec((tm, d), lambda i: (i, 0))],
        out_shape=[jax.ShapeDtypeStruct((n, d), F32),
                   jax.ShapeDtypeStruct((n, d), BF16)],
        compiler_params=_cparams(("parallel",)),
        name="out_proj",
    )(y_pool, o_f, o_b, cols, norm_g.reshape(1, HG_DK), y_att, y_s5, w_out, x_ctx, x_lat, mod_l, g2.reshape(1, d))


FFN_HALO = 16


def _ffn_kernel(h_ref, hp_ref, hn_ref, wa_ref, wb_ref, wd_ref, cw_ref, cb_ref, x1_ref, mod_ref, oc_ref, ol_ref,
                ext_ref, aext_ref, acc_ref, *, tm, nf):
    i = pl.program_id(0)
    j = pl.program_id(1)

    @pl.when(j == 0)
    def _():
        ext_ref[0:FFN_HALO, :] = hp_ref[...]
        ext_ref[FFN_HALO:FFN_HALO + tm, :] = h_ref[...]
        ext_ref[FFN_HALO + tm:2 * FFN_HALO + tm, :] = hn_ref[...]
        acc_ref[...] = jnp.zeros_like(acc_ref)

    aext_ref[...] = _dot(ext_ref[...], wa_ref[...])
    b = _dot(h_ref[...], wb_ref[...])
    row0 = i * tm
    tseq = jnp.where(row0 < N_CTX, SEQ, DEC_SEQ)
    pos = (row0 + lax.broadcasted_iota(jnp.int32, (tm, 1), 0)) & (tseq - 1)
    a_prev = jnp.where(pos == 0, 0.0, aext_ref[FFN_HALO - 1:FFN_HALO - 1 + tm, :])
    a_mid = aext_ref[FFN_HALO:FFN_HALO + tm, :]
    a_next = jnp.where(pos == tseq - 1, 0.0, aext_ref[FFN_HALO + 1:FFN_HALO + 1 + tm, :])
    a = a_prev * cw_ref[0:1, :] + a_mid * cw_ref[1:2, :] + a_next * cw_ref[2:3, :] + cb_ref[...]
    act = a * jax.nn.sigmoid(a) * b
    acc_ref[...] += _dot(act.astype(BF16), wd_ref[...])

    @pl.when((j == nf - 1) & (row0 < N_CTX))
    def _():
        oc_ref[...] = x1_ref[...] + mod_ref[5:6, :] * acc_ref[...]

    @pl.when((j == nf - 1) & (row0 >= N_CTX))
    def _():
        ol_ref[...] = x1_ref[...] + mod_ref[5:6, :] * acc_ref[...]


def conv_ffn(h2, x1, mod_l, w_up, conv_w, conv_b, w_down, layer, tm=512, tf=512):
    n, d = x1.shape
    f = w_down.shape[1]
    nf = f // tf
    nhb = n // FFN_HALO
    oc_spec, ol_spec = _split_specs(tm, d, lambda i, j: i)
    return pl.pallas_call(
        functools.partial(_ffn_kernel, tm=tm, nf=nf),
        grid=(n // tm, nf),
        in_specs=[pl.BlockSpec((tm, d), lambda i, j: (i, 0)),
                  pl.BlockSpec((FFN_HALO, d), lambda i, j: (jnp.maximum(i * (tm // FFN_HALO) - 1, 0), 0)),
                  pl.BlockSpec((FFN_HALO, d), lambda i, j: (jnp.minimum((i + 1) * (tm // FFN_HALO), nhb - 1), 0)),
                  pl.BlockSpec((None, d, tf), lambda i, j: (layer, 0, j)),
                  pl.BlockSpec((None, d, tf), lambda i, j: (layer, 0, nf + j)),
                  pl.BlockSpec((None, tf, d), lambda i, j: (layer, j, 0)),
                  pl.BlockSpec((3, tf), lambda i, j: (0, j)),
                  pl.BlockSpec((1, tf), lambda i, j: (0, j)),
                  pl.BlockSpec((tm, d), lambda i, j: (i, 0)),
                  pl.BlockSpec((None, 6, d), lambda i, j: (_mod_index(i * tm), 0, 0))],
        out_specs=[oc_spec, ol_spec],
        out_shape=[jax.ShapeDtypeStruct((N_CTX, d), F32), jax.ShapeDtypeStruct((n - N_CTX, d), F32)],
        scratch_shapes=[pltpu.VMEM((tm + 2 * FFN_HALO, d), BF16),
                        pltpu.VMEM((tm + 2 * FFN_HALO, tf), F32),
                        pltpu.VMEM((tm, d), F32)],
        compiler_params=_cparams(("arbitrary", "arbitrary")),
        name="conv_ffn",
    )(h2, h2, h2, w_up, w_up, w_down, conv_w, conv_b.reshape(1, f), x1, mod_l)


def _rope_tables():
    half = HEAD_DIM // 2
    nf = half // 2
    inv = ROPE_BASE ** (-jnp.arange(nf, dtype=F32) / nf)
    t = jnp.arange(DEC_SEQ)
    rows = (t // GRID_W).astype(F32)
    cols = (t % GRID_W).astype(F32)
    ang = jnp.concatenate([rows[:, None] * inv[None]] * 2 + [cols[:, None] * inv[None]] * 2, axis=1)
    sign = jnp.tile(jnp.concatenate([-jnp.ones(nf, F32), jnp.ones(nf, F32)]), 2)
    cos = jnp.tile(jnp.cos(ang), (DEC_BATCH, ATT_KV_HEADS))
    sin = jnp.tile(jnp.sin(ang) * sign, (DEC_BATCH, ATT_KV_HEADS))
    cos = jnp.concatenate([jnp.ones((N_CTX, 128), F32), cos], axis=0)
    sin = jnp.concatenate([jnp.zeros((N_CTX, 128), F32), sin], axis=0)
    return cos, sin


def kernel(x_prompt, x_sample, cache_k, cache_v, state_hgrn, state_s5_re, state_s5_im, c, c_ctx, norm1_g, norm2_g, ada_w, ada_b, w_in, w_out, pool_w, pool_scale, hg_lb_raw, hg_norm_g, q_norm_g, k_norm_g, att_sink, s5_a_re, s5_a_im, s5_log_dt, s5_b_re, s5_b_im, s5_c_re, s5_c_im, s5_d, s5_w_glu, ffn_w_up, ffn_conv_w, ffn_conv_b, ffn_w_down):
    d = D_MODEL
    x_ctx = x_prompt.reshape(N_CTX, d).astype(F32)
    x_lat = x_sample.reshape(N_LAT, d).astype(F32)

    cond = jnp.concatenate([c_ctx[None].astype(F32), c.astype(F32), jnp.zeros((8 - 1 - DEC_BATCH, d), F32)], axis=0)
    mod = adaln_mod(cond, ada_w, ada_b)

    lb_cum = jnp.cumsum(jax.nn.softmax(hg_lb_raw.astype(F32), axis=0), axis=0)
    hg_lb = lb_cum - lb_cum[:1]
    lp = jnp.stack([jnp.log(hg_lb), jnp.log1p(-hg_lb)], axis=2)

    cos_t, sin_t = _rope_tables()
    s0_ctx = jnp.zeros((BATCH, 2, HG_HEADS, HG_DK, HG_DK), F32)
    s5_ops = s5_operators(s5_a_re, s5_a_im, s5_log_dt, s5_b_re, s5_b_im, s5_c_re, s5_c_im)

    w_in_p = jnp.concatenate([w_in[..., :3584], w_in[..., 3840:4352], w_in[..., 3584:3840]], axis=-1).astype(BF16)
    w_out_b = w_out.astype(BF16)
    w_up_b = ffn_w_up.astype(BF16)
    w_down_b = ffn_w_down.astype(BF16)

    ks_, vs_, hs_, s5_ = [], [], [], []
    for l in range(DEPTH):
        cols = in_proj(x_ctx, x_lat, mod[l], norm1_g[l], w_in_p, l)

        y_pool = pool_mixer(cols, pool_w[l].astype(BF16), pool_scale[l])

        s0_lat = jnp.swapaxes(state_hgrn[:, l].astype(F32), -1, -2)
        o_f, o_b, sfin = hgrn_scan(cols, lp[l], s0_ctx, 0, BATCH, SEQ)
        o_f, o_b, _ = hgrn_scan(cols, lp[l], s0_lat, N_CTX, DEC_BATCH, DEC_SEQ, prev=(o_f, o_b))
        hs_.append(sfin)

        qs, kn = qk_prep(cols, cos_t, sin_t, q_norm_g[l], k_norm_g[l])
        sink = att_sink[l].astype(F32)
        y_att = attend_ctx(qs, kn, cols, sink, BATCH, SEQ)
        y_att = attend_lat(qs, kn, cols, cache_k[:, l].reshape(DEC_BATCH, PAST_LEN, 128).astype(F32),
                           cache_v[:, l].reshape(DEC_BATCH, PAST_LEN, 128).astype(F32), sink, y_att,
                           N_CTX, DEC_BATCH, DEC_SEQ)
        ks_.append(kn[:N_CTX].reshape(BATCH, SEQ, ATT_KV_HEADS, HEAD_DIM))
        vs_.append(cols[:N_CTX, 4224:4352].reshape(BATCH, SEQ, ATT_KV_HEADS, HEAD_DIM))

        h0_lat = jnp.concatenate([state_s5_re[:, l], state_s5_im[:, l]], axis=-1).astype(F32)
        h0_lat = jnp.pad(jnp.transpose(h0_lat, (2, 1, 0, 3)), ((0, 0), (0, 0), (0, 8 - DEC_BATCH), (0, 0)))
        ud = s5_gather(cols)
        ys, hf_c = s5_scan(ud, s5_ops, l, None, None, 0, BATCH, SEQ // S5_L)
        ys, = s5_scan(ud, s5_ops, l, h0_lat, ys, N_CTX, DEC_BATCH, DEC_SEQ // S5_L)
        y_s5 = s5_post(ud, ys, s5_d[l], s5_w_glu[l].astype(BF16))
        hf = hf_c.reshape(S5_GROUPS, 2, BATCH, SEQ // S5_L, 2 * S5_N)
        fin = jnp.stack([hf[:, 0, :, -1], hf[:, 1, :, 0]], axis=1)
        s5_.append(jnp.transpose(fin, (2, 1, 0, 3)))

        x1, h2 = out_proj(y_pool, (o_f, o_b, cols, hg_norm_g[l]), y_att, y_s5, w_out_b, l, x_ctx, x_lat, mod[l],
                          norm2_g[l])
        x_ctx, x_lat = conv_ffn(h2, x1, mod[l], w_up_b, ffn_conv_w[l].astype(F32), ffn_conv_b[l].astype(F32),
                                w_down_b, l)

    y_prompt = x_ctx.reshape(BATCH, SEQ, d).astype(x_prompt.dtype)
    y_sample = x_lat.reshape(DEC_BATCH, DEC_SEQ, d).astype(x_sample.dtype)
    s5_all = jnp.stack(s5_, axis=1)
    return (y_prompt, y_sample, jnp.stack(ks_, axis=1), jnp.stack(vs_, axis=1), jnp.stack(hs_, axis=1),
            s5_all[..., :S5_N], s5_all[..., S5_N:])
```

```python
import functools

import jax
import jax.numpy as jnp
import numpy as np
from jax import lax
from jax.experimental import pallas as pl
from jax.experimental.pallas import tpu as pltpu

F32 = jnp.float32
BF16 = jnp.bfloat16
HIGHEST = lax.Precision.HIGHEST
LOG2E = 1.4426950408889634

D_MODEL = 2048
BATCH = 32
SEQ = 256
DEPTH = 2
DEC_BATCH = 2
DEC_SEQ = 4096
PAST_LEN = 256
GRID_W = 64
GROUP_W = 512
POOL_WINDOWS = (2, 4, 8, 16)
POOL_GW = 128
HG_HEADS = 4
HG_DK = 128
ATT_HEADS = 8
ATT_KV_HEADS = 2
ATT_GROUP = 4
HEAD_DIM = 64
WINDOW = 128
BLOCK = 128
ROPE_BASE = 10000.0
S5_CH = 16
S5_GROUPS = 32
S5_N = 64
FFN_DIM = 5632
EPS = 1e-6
NEG_INF = -1e30

N_CTX = BATCH * SEQ
N_LAT = DEC_BATCH * DEC_SEQ
N_TOK = N_CTX + N_LAT
IN_COLS = 4352

CB_POOL, CB_HQ, CB_FF, CB_FB, CB_HI, CB_HG, CB_AQ, CB_S5 = range(8)
CB_AK, CB_AV = 32, 33

S5_L = 16
HG_C = 128
VMEM_LIMIT = 56 * 1024 * 1024


def _cparams(sem):
    return pltpu.CompilerParams(dimension_semantics=sem, vmem_limit_bytes=VMEM_LIMIT)


def _dot(a, b):
    return jnp.dot(a, b, preferred_element_type=F32)


def _dot_nt(a, b):
    return lax.dot_general(a, b, (((1,), (1,)), ((), ())), preferred_element_type=F32)


def _dot_tn(a, b):
    return lax.dot_general(a, b, (((0,), (0,)), ((), ())), preferred_element_type=F32)


def _split3(x):
    hi = x.astype(BF16)
    r1 = x - hi.astype(F32)
    mid = r1.astype(BF16)
    lo = (r1 - mid.astype(F32)).astype(BF16)
    return hi, mid, lo


def _dot_exact01(x, b01):
    hi, mid, lo = _split3(x)
    return _dot(hi, b01) + _dot(mid, b01) + _dot(lo, b01)


def _mod_index(row0):
    return jnp.where(row0 < N_CTX, 0, 1 + (row0 - N_CTX) // DEC_SEQ)


def _mod_kernel(cond_ref, w_ref, b_ref, o_ref):
    x = cond_ref[...]
    s = x * jax.nn.sigmoid(x)
    o_ref[...] = _dot(s.astype(BF16), w_ref[...].astype(BF16)) + b_ref[...]


def adaln_mod(cond, ada_w, ada_b, tn=1024):
    depth, d, n6 = ada_w.shape
    out = pl.pallas_call(
        _mod_kernel,
        grid=(depth, n6 // tn),
        in_specs=[pl.BlockSpec((8, d), lambda l, j: (0, 0)),
                  pl.BlockSpec((None, d, tn), lambda l, j: (l, 0, j)),
                  pl.BlockSpec((None, 1, tn), lambda l, j: (l, 0, j))],
        out_specs=pl.BlockSpec((None, 8, tn), lambda l, j: (l, 0, j)),
        out_shape=jax.ShapeDtypeStruct((depth, 8, n6), F32),
        compiler_params=_cparams(("parallel", "parallel")),
        name="adaln_mod",
    )(cond, ada_w, ada_b.reshape(depth, 1, n6))
    return out.reshape(depth, 8, 6, d)


def _split_specs(tm, d, row_tile):
    nc = N_CTX // tm
    return (pl.BlockSpec((tm, d), lambda *g: (jnp.minimum(row_tile(*g), nc - 1), 0)),
            pl.BlockSpec((tm, d), lambda *g: (jnp.maximum(row_tile(*g) - nc, 0), 0)))


def _inproj_kernel(xc_ref, xl_ref, mod_ref, g_ref, w_ref, o_ref, *, tm):
    x = jnp.where(pl.program_id(1) * tm < N_CTX, xc_ref[...], xl_ref[...])
    ms = jnp.mean(x * x, axis=-1, keepdims=True)
    h = x * lax.rsqrt(ms + EPS) * g_ref[...]
    h = h * (1.0 + mod_ref[1:2, :]) + mod_ref[0:1, :]
    o_ref[...] = _dot(h.astype(BF16), w_ref[...])


def in_proj(x_ctx, x_lat, mod_l, g1, w_in, layer, tm=256):
    d = x_ctx.shape[1]
    n = x_ctx.shape[0] + x_lat.shape[0]
    ncols = w_in.shape[2]
    tn = ncols
    xc_spec, xl_spec = _split_specs(tm, d, lambda j, i: i)
    return pl.pallas_call(
        functools.partial(_inproj_kernel, tm=tm),
        grid=(ncols // tn, n // tm),
        in_specs=[xc_spec, xl_spec,
                  pl.BlockSpec((None, 6, d), lambda j, i: (_mod_index(i * tm), 0, 0)),
                  pl.BlockSpec((1, d), lambda j, i: (0, 0)),
                  pl.BlockSpec((None, d, tn), lambda j, i: (layer, 0, j), pipeline_mode=pl.Buffered(1))],
        out_specs=pl.BlockSpec((tm, tn), lambda j, i: (i, j)),
        out_shape=jax.ShapeDtypeStruct((n, ncols), F32),
        compiler_params=_cparams(("parallel", "parallel")),
        name="in_proj",
    )(x_ctx, x_lat, mod_l, g1.reshape(1, d), w_in)


POOL_HALO = 64


def _pool_kernel(u_ref, up_ref, un_ref, w_ref, sc_ref, o_ref, *, tm):
    i = pl.program_id(0)
    row0 = i * tm
    tseq = jnp.where(row0 < N_CTX, SEQ, DEC_SEQ)
    ne = tm + 2 * POOL_HALO
    parts = _split3(jnp.concatenate([up_ref[...], u_ref[...], un_ref[...]], axis=0))
    r_idx = lax.broadcasted_iota(jnp.int32, (tm, ne), 0)
    off = lax.broadcasted_iota(jnp.int32, (tm, ne), 1) - POOL_HALO - r_idx
    pos_d = (row0 + r_idx) & (tseq - 1)
    inside = lax.bitcast_convert_type(pos_d + off, jnp.uint32) < tseq.astype(jnp.uint32)
    inside01 = jnp.where(inside, 1.0, 0.0)
    pos = (row0 + lax.broadcasted_iota(jnp.int32, (tm, 1), 0)) & (tseq - 1)
    outs = []
    for gi, w in enumerate(POOL_WINDOWS):
        ls = slice(gi * POOL_GW, (gi + 1) * POOL_GW)
        in_win = lax.bitcast_convert_type(off + w // 2, jnp.uint32) < w
        band = jnp.where(in_win, inside01, 0.0).astype(BF16)
        acc = _dot(band, parts[0][:, ls]) + _dot(band, parts[1][:, ls]) + _dot(band, parts[2][:, ls])
        lo = jnp.maximum(pos - w // 2, 0)
        hi = jnp.minimum(pos + w // 2, tseq)
        cnt = (hi - lo).astype(F32)
        pooled = acc / cnt - u_ref[:, ls]
        outs.append(_dot(pooled.astype(BF16), w_ref[gi]))
    o_ref[...] = (jnp.concatenate(outs, axis=1) * sc_ref[...]).astype(o_ref.dtype)


def pool_mixer(cols, pool_w, pool_scale, tm=256):
    n = cols.shape[0]
    nbh = n // POOL_HALO
    return pl.pallas_call(
        functools.partial(_pool_kernel, tm=tm),
        grid=(n // tm,),
        in_specs=[pl.BlockSpec((tm, GROUP_W), lambda i: (i, CB_POOL)),
                  pl.BlockSpec((POOL_HALO, GROUP_W), lambda i: (jnp.maximum(i * (tm // POOL_HALO) - 1, 0), CB_POOL)),
                  pl.BlockSpec((POOL_HALO, GROUP_W),
                               lambda i: (jnp.minimum((i + 1) * (tm // POOL_HALO), nbh - 1), CB_POOL)),
                  pl.BlockSpec((4, POOL_GW, POOL_GW), lambda i: (0, 0, 0)),
                  pl.BlockSpec((1, GROUP_W), lambda i: (0, 0))],
        out_specs=pl.BlockSpec((tm, GROUP_W), lambda i: (i, 0)),
        out_shape=jax.ShapeDtypeStruct((n, GROUP_W), BF16),
        compiler_params=_cparams(("parallel",)),
        name="pool_mixer",
    )(cols, cols, cols, pool_w, pool_scale.reshape(1, GROUP_W))


def _block_ref_rows(x, c, m, rev):
    w = x.shape[1]
    if 2 * m >= 8:
        r = m if rev else m - 1
        xr = x.reshape(c // (2 * m), 2 * m, w)[:, r:r + 1, :]
        return jnp.broadcast_to(xr, (c // (2 * m), 2 * m, w)).reshape(c, w)
    x8 = x.reshape(c // 8, 8, w)
    sub = lax.broadcasted_iota(jnp.int32, (c // 8, 8, w), 1)
    out = None
    for kb in range(8 // (2 * m)):
        r = kb * 2 * m + (m if rev else m - 1)
        cand = jnp.broadcast_to(x8[:, r:r + 1, :], (c // 8, 8, w))
        out = cand if out is None else jnp.where(sub >= kb * 2 * m, cand, out)
    return out.reshape(c, w)


def _hgrn_chunk(q, fx, v, log_l, log1m_l, st_ref, rev, c):
    ls = jnp.minimum(fx, 0.0) - jnp.log(1.0 + jnp.exp(-jnp.abs(fx)))
    bb = log1m_l + ls
    logf = jnp.maximum(log_l, bb) + jnp.log(1.0 + jnp.exp(-jnp.abs(log_l - bb)))
    kk = 1.0 - jnp.exp(logf)
    row = lax.broadcasted_iota(jnp.int32, (c, c), 0)
    col = lax.broadcasted_iota(jnp.int32, (c, c), 1)
    causal = (row <= col) if rev else (row >= col)
    tri = jnp.where(causal, 1.0, 0.0).astype(BF16)
    hi, mid, lo = _split3(logf)
    b = (_dot(tri, hi) + _dot(tri, mid) + _dot(tri, lo)) * LOG2E
    b_last = b[0:1] if rev else b[c - 1:c]
    qt = (q * jnp.exp2(b)).astype(BF16)
    kd = (kk * jnp.exp2(b_last - b)).astype(BF16)
    dl = jnp.exp2(b_last)
    xor = row ^ col
    tbit = lax.broadcasted_iota(jnp.int32, (c, 1), 0)
    levels = []
    m = c // 2
    while m >= 1:
        query_side = ((tbit & m) == 0) if rev else ((tbit & m) != 0)
        wgt = jnp.exp2((b - _block_ref_rows(b, c, m, rev)) * jnp.where(query_side, 1.0, -1.0))
        levels.append(((jnp.where(query_side, q, kk) * wgt).astype(BF16), (xor >> (m.bit_length() - 1)) == 1))
        m //= 2
    qb = q.astype(BF16)
    kb = kk.astype(BF16)
    vb = v.astype(BF16)
    outs = []
    for h in range(HG_HEADS):
        sl = slice(h * HG_DK, (h + 1) * HG_DK)
        att = _dot_nt(qb[:, sl], kb[:, sl])
        for z, msk in levels:
            att = jnp.where(msk, _dot_nt(z[:, sl], z[:, sl]), att)
        att = jnp.where(causal, att, 0.0)
        intra = _dot(att.astype(BF16), vb[:, sl])
        st = st_ref[h]
        inter = _dot_nt(qt[:, sl], st.astype(BF16))
        st_ref[h] = st * dl[:, sl] + _dot_tn(vb[:, sl], kd[:, sl])
        outs.append(intra + inter)
    return jnp.concatenate(outs, axis=1)


def _hgrn_kernel(qf_ref, ff_ref, vf_ref, qb_ref, fb_ref, vb_ref, lp_ref, s0_ref, *rest, c, nchunks):
    of_ref, ob_ref, sfin_ref, st_ref = rest[-4:]
    ci = pl.program_id(1)

    @pl.when(ci == 0)
    def _():
        st_ref[...] = s0_ref[...]

    of_ref[...] = _hgrn_chunk(qf_ref[...], ff_ref[...], vf_ref[...], lp_ref[0, 0:1, :], lp_ref[0, 1:2, :],
                              st_ref.at[0], False, c)
    ob_ref[...] = _hgrn_chunk(qb_ref[...], fb_ref[...], vb_ref[...], lp_ref[1, 0:1, :], lp_ref[1, 1:2, :],
                              st_ref.at[1], True, c)

    @pl.when(ci == nchunks - 1)
    def _():
        for d in range(2):
            for h in range(HG_HEADS):
                sfin_ref[d, h] = st_ref[d, h].T


def hgrn_scan(cols, lp, s0_t, row_off, nseq, t, prev=None, c=HG_C):
    n = cols.shape[0]
    nchunks = t // c
    base = row_off // c
    extra = [] if prev is None else list(prev)
    any_spec = pl.BlockSpec(memory_space=pl.ANY)

    def fwd(cb):
        return pl.BlockSpec((c, GROUP_W), lambda s, i: (base + s * nchunks + i, cb))

    def bwd(cb):
        return pl.BlockSpec((c, GROUP_W), lambda s, i: (base + s * nchunks + nchunks - 1 - i, cb))

    st_spec = pl.BlockSpec((None, 2, HG_HEADS, HG_DK, HG_DK), lambda s, i: (s, 0, 0, 0, 0))
    return pl.pallas_call(
        functools.partial(_hgrn_kernel, c=c, nchunks=nchunks),
        grid=(nseq, nchunks),
        in_specs=[fwd(CB_HQ), fwd(CB_FF), fwd(CB_HI), bwd(CB_HQ), bwd(CB_FB), bwd(CB_HI),
                  pl.BlockSpec((2, 2, GROUP_W), lambda s, i: (0, 0, 0)), st_spec] + [any_spec] * len(extra),
        out_specs=[fwd(0), bwd(0), st_spec],
        out_shape=[jax.ShapeDtypeStruct((n, GROUP_W), F32),
                   jax.ShapeDtypeStruct((n, GROUP_W), F32),
                   jax.ShapeDtypeStruct((nseq, 2, HG_HEADS, HG_DK, HG_DK), F32)],
        input_output_aliases={8 + k: k for k in range(len(extra))},
        scratch_shapes=[pltpu.VMEM((2, HG_HEADS, HG_DK, HG_DK), F32)],
        compiler_params=_cparams(("parallel", "arbitrary")),
        name="hgrn_scan",
    )(cols, cols, cols, cols, cols, cols, lp, s0_t, *extra)


def _hgrn_finish(o_f, o_b, g, norm_g):
    o = o_f + o_b
    gate = g * jax.nn.sigmoid(g)
    outs = []
    for h in range(HG_HEADS):
        oh = o[:, h * HG_DK:(h + 1) * HG_DK]
        ms = jnp.mean(oh * oh, axis=-1, keepdims=True)
        outs.append(oh * lax.rsqrt(ms + EPS) * norm_g)
    return (jnp.concatenate(outs, axis=1) * gate).astype(BF16)


def _swap16(x):
    w = x.shape[1]
    lane = lax.broadcasted_iota(jnp.int32, x.shape, 1)
    return jnp.where((lane & 31) < 16, pltpu.roll(x, w - 16, 1), pltpu.roll(x, 16, 1))


def _qkprep_kernel(q_ref, k_ref, cos_ref, sin_ref, gq_ref, gk_ref, bq_ref, qo_ref, ko_ref):
    cos = cos_ref[...]
    sin = sin_ref[...]
    q = q_ref[...]
    msq = _dot_exact01(q * q, bq_ref[...]) * (1.0 / HEAD_DIM)
    qn = q * lax.rsqrt(msq + EPS) * gq_ref[...]
    cos4 = jnp.concatenate([cos] * 4, axis=1)
    sin4 = jnp.concatenate([sin] * 4, axis=1)
    qr = qn * cos4 + _swap16(qn) * sin4
    qo_ref[...] = (qr * (HEAD_DIM ** -0.5)).astype(qo_ref.dtype)
    k = k_ref[...]
    msk = _dot_exact01(k * k, bq_ref[0:128, 0:128]) * (1.0 / HEAD_DIM)
    kn = k * lax.rsqrt(msk + EPS) * gk_ref[...]
    ko_ref[...] = kn * cos + _swap16(kn) * sin


def qk_prep(cols, cos_t, sin_t, q_norm_g, k_norm_g, tm=512):
    n = cols.shape[0]
    head = np.arange(GROUP_W) // HEAD_DIM
    bones = jnp.asarray((head[:, None] == head[None, :]).astype(np.float32), BF16)
    return pl.pallas_call(
        _qkprep_kernel,
        grid=(n // tm,),
        in_specs=[pl.BlockSpec((tm, GROUP_W), lambda i: (i, CB_AQ)),
                  pl.BlockSpec((tm, 128), lambda i: (i, CB_AK)),
                  pl.BlockSpec((tm, 128), lambda i: (i, 0)),
                  pl.BlockSpec((tm, 128), lambda i: (i, 0)),
                  pl.BlockSpec((1, GROUP_W), lambda i: (0, 0)),
                  pl.BlockSpec((1, 128), lambda i: (0, 0)),
                  pl.BlockSpec((GROUP_W, GROUP_W), lambda i: (0, 0))],
        out_specs=[pl.BlockSpec((tm, GROUP_W), lambda i: (i, 0)),
                   pl.BlockSpec((tm, 128), lambda i: (i, 0))],
        out_shape=[jax.ShapeDtypeStruct((n, GROUP_W), BF16),
                   jax.ShapeDtypeStruct((n, 128), F32)],
        compiler_params=_cparams(("parallel",)),
        name="qk_prep",
    )(cols, cols, cos_t, sin_t, jnp.tile(q_norm_g, ATT_HEADS).reshape(1, GROUP_W),
      jnp.tile(k_norm_g, ATT_KV_HEADS).reshape(1, 128), bones)


def _attn_ctx_kernel(sink_ref, q_ref, k_ref, v_ref, o_ref, *, t, spb):
    for sq in range(spb):
        rs = slice(sq * t, (sq + 1) * t)
        q = q_ref[rs, :]
        k = k_ref[rs, :].astype(BF16)
        v = v_ref[rs, :].astype(BF16)
        pieces = [None] * ATT_HEADS
        for kh in range(ATT_KV_HEADS):
            hs = slice(kh * HEAD_DIM, (kh + 1) * HEAD_DIM)
            for g in range(ATT_GROUP):
                hq = kh * ATT_GROUP + g
                s = _dot_nt(q[:, hq * HEAD_DIM:(hq + 1) * HEAD_DIM], k[:, hs])
                sk = sink_ref[hq]
                mx = jnp.maximum(jnp.max(s, axis=1, keepdims=True), sk)
                p = jnp.exp(s - mx)
                den = jnp.sum(p, axis=1, keepdims=True) + jnp.exp(sk - mx)
                pieces[hq] = _dot(p.astype(BF16), v[:, hs]) / den
        o_ref[rs, :] = jnp.concatenate(pieces, axis=1).astype(o_ref.dtype)


def attend_ctx(qs, kn, cols, sink, nseq, t, spb=2):
    return pl.pallas_call(
        functools.partial(_attn_ctx_kernel, t=t, spb=spb),
        grid=(nseq // spb,),
        in_specs=[pl.BlockSpec(memory_space=pltpu.SMEM),
                  pl.BlockSpec((spb * t, GROUP_W), lambda s: (s, 0)),
                  pl.BlockSpec((spb * t, 128), lambda s: (s, 0)),
                  pl.BlockSpec((spb * t, 128), lambda s: (s, CB_AV))],
        out_specs=pl.BlockSpec((spb * t, GROUP_W), lambda s: (s, 0)),
        out_shape=jax.ShapeDtypeStruct((qs.shape[0], GROUP_W), BF16),
        compiler_params=_cparams(("parallel",)),
        name="attend_ctx",
    )(sink, qs, kn, cols)


def _attn_lat_kernel(sink_ref, q_ref, *refs, t, blk, nk):
    k_refs, v_refs = refs[:nk], refs[nk:2 * nk]
    ck_ref, cv_ref, _, o_ref = refs[2 * nk:]
    i = pl.program_id(1)
    q = q_ref[...]
    kl = jnp.concatenate([r[...] for r in k_refs], axis=0).astype(BF16)
    vl = jnp.concatenate([r[...] for r in v_refs], axis=0).astype(BF16)
    ck = ck_ref[...].astype(BF16)
    cv = cv_ref[...].astype(BF16)
    r_idx = lax.broadcasted_iota(jnp.int32, (blk, nk * WINDOW), 0)
    c_idx = lax.broadcasted_iota(jnp.int32, (blk, nk * WINDOW), 1)
    kpos = i * blk - WINDOW + lax.broadcasted_iota(jnp.int32, (1, nk * WINDOW), 1)
    col_bias = jnp.where(kpos >= 0, jnp.where(kpos < t, 0.0, NEG_INF), NEG_INF)
    bias = jnp.where(r_idx <= c_idx, jnp.where(r_idx >= c_idx - 2 * WINDOW, col_bias, NEG_INF), NEG_INF)
    pieces = [None] * ATT_HEADS
    for kh in range(ATT_KV_HEADS):
        hs = slice(kh * HEAD_DIM, (kh + 1) * HEAD_DIM)
        for g in range(ATT_GROUP):
            hq = kh * ATT_GROUP + g
            qh = q[:, hq * HEAD_DIM:(hq + 1) * HEAD_DIM]
            s_loc = _dot_nt(qh, kl[:, hs]) + bias
            s_ctx = _dot_nt(qh, ck[:, hs])
            sk = sink_ref[hq]
            mx = jnp.maximum(jnp.maximum(jnp.max(s_loc, axis=1, keepdims=True),
                                         jnp.max(s_ctx, axis=1, keepdims=True)), sk)
            p_loc = jnp.exp(s_loc - mx)
            p_ctx = jnp.exp(s_ctx - mx)
            den = (jnp.sum(p_loc, axis=1, keepdims=True) + jnp.sum(p_ctx, axis=1, keepdims=True)
                   + jnp.exp(sk - mx))
            pieces[hq] = (_dot(p_loc.astype(BF16), vl[:, hs]) + _dot(p_ctx.astype(BF16), cv[:, hs])) / den
    o_ref[...] = jnp.concatenate(pieces, axis=1).astype(o_ref.dtype)


def attend_lat(qs, kn, cols, cache_k, cache_v, sink, prev, row_off, nseq, t, blk=2 * WINDOW):
    nb = t // blk
    base = row_off // blk
    lctx = cache_k.shape[1]
    kpb = blk // WINDOW
    nk = kpb + 2
    nkb = t // WINDOW
    kbase = row_off // WINDOW

    def nbr(cb, j):
        return pl.BlockSpec((WINDOW, 128),
                            lambda b, i: (kbase + b * nkb + jnp.clip(i * kpb - 1 + j, 0, nkb - 1), cb))

    return pl.pallas_call(
        functools.partial(_attn_lat_kernel, t=t, blk=blk, nk=nk),
        grid=(nseq, nb),
        in_specs=[pl.BlockSpec(memory_space=pltpu.SMEM),
                  pl.BlockSpec((blk, GROUP_W), lambda b, i: (base + b * nb + i, 0))]
                 + [nbr(0, j) for j in range(nk)] + [nbr(CB_AV, j) for j in range(nk)]
                 + [pl.BlockSpec((None, lctx, 128), lambda b, i: (b, 0, 0)),
                    pl.BlockSpec((None, lctx, 128), lambda b, i: (b, 0, 0)),
                    pl.BlockSpec(memory_space=pl.ANY)],
        out_specs=pl.BlockSpec((blk, GROUP_W), lambda b, i: (base + b * nb + i, 0)),
        out_shape=jax.ShapeDtypeStruct(prev.shape, BF16),
        input_output_aliases={4 + 2 * nk: 0},
        compiler_params=_cparams(("parallel", "parallel")),
        name="attend_lat",
    )(sink, qs, *([kn] * nk), *([cols] * nk), cache_k, cache_v, prev)


def s5_operators(a_re, a_im, log_dt, b_re, b_im, c_re, c_im):
    L = S5_L
    dt = jnp.exp(log_dt.astype(F32))[..., None]
    are, aim = a_re.astype(F32), a_im.astype(F32)
    adr, adi = (are * dt)[..., None, :], (aim * dt)[..., None, :]

    def lam_pow(p):
        p = jnp.asarray(p, F32)[None, :, None, :, None]
        mag = jnp.exp(adr * p)
        return mag * jnp.cos(adi * p), mag * jnp.sin(adi * p)

    ar = np.arange(L)
    one = np.ones((2, 1))
    lre, lim = lam_pow(one)
    lre, lim = lre[..., 0, :], lim[..., 0, :]
    den = are * are + aim * aim
    fre = ((lre - 1.0) * are + lim * aim) / den
    fim = (lim * are - (lre - 1.0) * aim) / den
    bre = fre[..., None] * b_re - fim[..., None] * b_im
    bim = fre[..., None] * b_im + fim[..., None] * b_re
    bret, bimt = jnp.swapaxes(bre, -1, -2), jnp.swapaxes(bim, -1, -2)
    cre, cim = c_re.astype(F32)[..., None, :, :], c_im.astype(F32)[..., None, :, :]
    ein = functools.partial(jnp.einsum, precision=HIGHEST)

    def c_lam(p):
        pr, pi = lam_pow(p)
        pr, pi = pr[..., None, :], pi[..., None, :]
        return cre * pr - cim * pi, cre * pi + cim * pr

    cpr, cpi = c_lam(one * ar)
    ktau = ein('ldgtcn,ldgne->ldgtce', cpr, bre) - ein('ldgtcn,ldgne->ldgtce', cpi, bim)
    lw = L * S5_CH
    rowf = jnp.transpose(ktau[:, 0], (0, 1, 4, 2, 3)).reshape(ktau.shape[0], -1, S5_CH, lw)
    rowb = jnp.transpose(jnp.flip(ktau[:, 1], axis=2), (0, 1, 4, 2, 3)).reshape(ktau.shape[0], -1, S5_CH, lw)
    zeros = jnp.zeros_like(rowf)
    padf = jnp.concatenate([zeros, rowf], axis=-1)
    padb = jnp.concatenate([rowb, zeros], axis=-1)
    kf = jnp.stack([padf[..., lw - S5_CH * s:2 * lw - S5_CH * s] for s in range(L)], axis=2)
    kb = jnp.stack([padb[..., S5_CH * (L - 1 - s):S5_CH * (L - 1 - s) + lw] for s in range(L)], axis=2)
    kmat = jnp.stack([kf, kb], axis=1).reshape(ktau.shape[:3] + (lw, lw))
    pr, pi = lam_pow(np.stack([L - 1 - ar, ar]))
    pr, pi = pr[..., None, :], pi[..., None, :]
    wr = pr * bret[..., None, :, :] - pi * bimt[..., None, :, :]
    wi = pr * bimt[..., None, :, :] + pi * bret[..., None, :, :]
    wst = jnp.concatenate([wr, wi], axis=-1).reshape(ktau.shape[:3] + (L * S5_CH, 2 * S5_N))
    orr, oii = c_lam(np.stack([ar + 1, L - ar]))
    wout = jnp.concatenate([jnp.moveaxis(orr, -1, -3), -jnp.moveaxis(oii, -1, -3)], axis=-3)
    wout = wout.reshape(ktau.shape[:3] + (2 * S5_N, L * S5_CH))
    jre, jim = lam_pow(one * (L * 2.0 ** np.arange(9)))
    lam = jnp.stack([jnp.concatenate([jre, jre], axis=-1), jnp.concatenate([-jim, jim], axis=-1)], axis=-2)
    return kmat.astype(BF16), wst.astype(BF16), wout.astype(BF16), lam


S5_BUNDLE = 128 // S5_CH


def _s5_group(gl, u, km_ref, ws_ref, wo_ref, lam_ref, h0_ref, hfin_ref, seglen, nseq):
    r = nseq * seglen
    rowi = lax.broadcasted_iota(jnp.int32, (r, 1), 0)
    cl = rowi & (seglen - 1)

    def cmul(d, j, x):
        return lam_ref[d, gl, j, 0:1, :] * x + lam_ref[d, gl, j, 1:2, :] * pltpu.roll(x, S5_N, 1)

    y = None
    for d in range(2):
        edge = 0 if d == 0 else seglen - 1
        h0rows = jnp.zeros((r, 2 * S5_N), F32)
        x = _dot(u, ws_ref[d, gl])
        if h0_ref is not None:
            for s in range(nseq):
                h0rows = jnp.where(rowi == s * seglen + edge, h0_ref[gl, d, s:s + 1, :], h0rows)
            x = x + cmul(d, 0, h0rows)
        sh, j = 1, 0
        while sh < seglen:
            if d == 0:
                xs, ok = pltpu.roll(x, sh, 0), cl >= sh
            else:
                xs, ok = pltpu.roll(x, r - sh, 0), cl < seglen - sh
            x = x + jnp.where(ok, cmul(d, j, xs), 0.0)
            sh, j = sh * 2, j + 1
        if hfin_ref is not None:
            hfin_ref[gl, d] = x
        if d == 0:
            hin = jnp.where(cl >= 1, pltpu.roll(x, 1, 0), h0rows)
        else:
            hin = jnp.where(cl < seglen - 1, pltpu.roll(x, r - 1, 0), h0rows)
        yd = _dot(u, km_ref[d, gl]) + _dot(hin.astype(BF16), wo_ref[d, gl])
        y = yd if y is None else y + yd
    return y


def _s5_kernel(*refs, seglen, nseq, has_h0, has_prev, has_fin):
    u_ref, km_ref, ws_ref, wo_ref, lam_ref = refs[:5]
    pos = 5
    h0_ref = refs[pos] if has_h0 else None
    pos += has_h0 + has_prev
    y_ref = refs[pos]
    hfin_ref = refs[pos + 1] if has_fin else None
    ycat_ref, rin_ref, rout_ref = refs[-3:]
    lane_blk = lax.broadcasted_iota(jnp.int32, (1, 128), 1) // S5_CH
    nv = S5_L // S5_BUNDLE
    nb = S5_BUNDLE

    def merge(pick):
        acc = pick(0)
        for b in range(1, nb):
            acc = jnp.where(lane_blk == b, pick(b), acc)
        return acc

    for v in range(nv):
        for s in range(nb):
            m = merge(lambda g: u_ref[v * nb + (g + s) % nb])
            rin_ref[v, s] = pltpu.roll(m, s * S5_CH, 1) if s else m

    def body(gl, carry):
        halves = [merge(lambda k: rin_ref[v, (k - gl) & (nb - 1)]) for v in range(nv)]
        u = jnp.concatenate(halves, axis=1).astype(BF16)
        ycat_ref[gl] = _s5_group(gl, u, km_ref, ws_ref, wo_ref, lam_ref, h0_ref, hfin_ref, seglen, nseq)
        return carry

    lax.fori_loop(0, nb, body, 0)
    for v in range(nv):
        for s in range(nb):
            m = merge(lambda k: ycat_ref[(k + s) % nb, :, v * 128:(v + 1) * 128])
            rout_ref[s] = pltpu.roll(m, s * S5_CH, 1) if s else m
        for k in range(nb):
            y_ref[v * nb + k] = merge(lambda g: rout_ref[(g - k) % nb])


S5_TILE = 512


def _s5_perm():
    p = np.arange(S5_TILE)
    src = (p % (S5_TILE // S5_L)) * S5_L + p // (S5_TILE // S5_L)
    return (src[:, None] == np.arange(S5_TILE)[None, :]).astype(np.float32)


def _s5_gather_kernel(u_ref, p_ref, o_ref):
    p = p_ref[...]
    hi, mid, lo = _split3(u_ref[...])
    o_ref[...] = (_dot(p, hi) + _dot(p, mid) + _dot(p, lo)).reshape(o_ref.shape)


def s5_gather(cols):
    n = cols.shape[0]
    cpt = S5_TILE // S5_L
    return pl.pallas_call(
        _s5_gather_kernel,
        grid=(n // S5_TILE,),
        in_specs=[pl.BlockSpec((S5_TILE, GROUP_W), lambda i: (i, CB_S5)),
                  pl.BlockSpec((S5_TILE, S5_TILE), lambda i: (0, 0))],
        out_specs=pl.BlockSpec((S5_L, cpt, GROUP_W), lambda i: (0, i, 0)),
        out_shape=jax.ShapeDtypeStruct((S5_L, n // S5_L, GROUP_W), F32),
        compiler_params=_cparams(("parallel",)),
        name="s5_gather",
    )(cols, jnp.asarray(_s5_perm(), BF16))


def s5_scan(ud, ops, layer, h0, prev, row_off, nseq, seglen):
    kmat, wst, wout, lam = ops
    nchunks = ud.shape[1]
    r = nseq * seglen
    rb = row_off // (r * S5_L)
    assert row_off % (r * S5_L) == 0
    lw = S5_L * S5_CH
    nb = S5_BUNDLE
    in_specs = [pl.BlockSpec((S5_L, r, 128), lambda j: (0, rb, j)),
                pl.BlockSpec((None, 2, nb, lw, lw), lambda j: (layer, 0, j, 0, 0)),
                pl.BlockSpec((None, 2, nb, lw, 2 * S5_N), lambda j: (layer, 0, j, 0, 0)),
                pl.BlockSpec((None, 2, nb, 2 * S5_N, lw), lambda j: (layer, 0, j, 0, 0)),
                pl.BlockSpec((None, 2, nb, 9, 2, 2 * S5_N), lambda j: (layer, 0, j, 0, 0, 0))]
    args = [ud, kmat, wst, wout, lam]
    if h0 is not None:
        in_specs.append(pl.BlockSpec((nb, 2, h0.shape[2], 2 * S5_N), lambda j: (j, 0, 0, 0)))
        args.append(h0)
    aliases = {}
    if prev is not None:
        aliases = {len(args): 0}
        in_specs.append(pl.BlockSpec(memory_space=pl.ANY))
        args.append(prev)
    out_specs = [pl.BlockSpec((S5_L, r, 128), lambda j: (0, rb, j))]
    out_shape = [jax.ShapeDtypeStruct((S5_L, nchunks, GROUP_W), F32)]
    has_fin = prev is None
    if has_fin:
        out_specs.append(pl.BlockSpec((nb, 2, r, 2 * S5_N), lambda j: (j, 0, 0, 0)))
        out_shape.append(jax.ShapeDtypeStruct((S5_GROUPS, 2, r, 2 * S5_N), F32))
    return pl.pallas_call(
        functools.partial(_s5_kernel, seglen=seglen, nseq=nseq, has_h0=h0 is not None,
                          has_prev=prev is not None, has_fin=has_fin),
        grid=(S5_GROUPS // nb,),
        in_specs=in_specs,
        out_specs=out_specs,
        out_shape=out_shape,
        input_output_aliases=aliases,
        scratch_shapes=[pltpu.VMEM((nb, r, lw), F32), pltpu.VMEM((S5_L // nb, nb, r, 128), F32),
                        pltpu.VMEM((nb, r, 128), F32)],
        compiler_params=_cparams(("parallel",)),
        name="s5_scan",
    )(*args)


def _s5_post_kernel(u_ref, y_ref, p_ref, d_ref, w_ref, o_ref):
    out = (d_ref[...] * u_ref[...] + y_ref[...]).reshape(S5_TILE, GROUP_W)
    yy = jax.nn.gelu(out)
    zz = _dot(yy.astype(BF16), w_ref[...])
    res = (zz[:, :GROUP_W] * jax.nn.sigmoid(zz[:, GROUP_W:])).astype(BF16)
    o_ref[...] = _dot_tn(p_ref[...], res).astype(o_ref.dtype)


def s5_post(ud, y, s5_d, w_glu):
    n = ud.shape[1] * S5_L
    cpt = S5_TILE // S5_L
    return pl.pallas_call(
        _s5_post_kernel,
        grid=(n // S5_TILE,),
        in_specs=[pl.BlockSpec((S5_L, cpt, GROUP_W), lambda i: (0, i, 0)),
                  pl.BlockSpec((S5_L, cpt, GROUP_W), lambda i: (0, i, 0)),
                  pl.BlockSpec((S5_TILE, S5_TILE), lambda i: (0, 0)),
                  pl.BlockSpec((1, GROUP_W), lambda i: (0, 0)),
                  pl.BlockSpec((GROUP_W, 2 * GROUP_W), lambda i: (0, 0))],
        out_specs=pl.BlockSpec((S5_TILE, GROUP_W), lambda i: (i, 0)),
        out_shape=jax.ShapeDtypeStruct((n, GROUP_W), BF16),
        compiler_params=_cparams(("parallel",)),
        name="s5_post",
    )(ud, y, jnp.asarray(_s5_perm(), BF16), s5_d.reshape(1, GROUP_W), w_glu)


def _outproj_kernel(p0_ref, of_ref, ob_ref, hg_ref, ng_ref, p2_ref, p3_ref, w_ref, xc_ref, xl_ref, mod_ref, g_ref,
                    x1_ref, h2_ref, *, tm):
    y = _dot(p0_ref[...], w_ref[0:GROUP_W, :])
    y_hg = _hgrn_finish(of_ref[...], ob_ref[...], hg_ref[...], ng_ref[...])
    y = y + _dot(y_hg, w_ref[GROUP_W:2 * GROUP_W, :])
    y = y + _dot(p2_ref[...], w_ref[2 * GROUP_W:3 * GROUP_W, :])
    y = y + _dot(p3_ref[...], w_ref[3 * GROUP_W:4 * GROUP_W, :])
    x = jnp.where(pl.program_id(0) * tm < N_CTX, xc_ref[...], xl_ref[...])
    x1 = x + mod_ref[2:3, :] * y
    x1_ref[...] = x1
    ms = jnp.mean(x1 * x1, axis=-1, keepdims=True)
    h = x1 * lax.rsqrt(ms + EPS) * g_ref[...]
    h2_ref[...] = (h * (1.0 + mod_ref[4:5, :]) + mod_ref[3:4, :]).astype(h2_ref.dtype)


def out_proj(y_pool, hgrn, y_att, y_s5, w_out, layer, x_ctx, x_lat, mod_l, g2, tm=512):
    o_f, o_b, cols, norm_g = hgrn
    d = x_ctx.shape[1]
    n = x_ctx.shape[0] + x_lat.shape[0]
    part_spec = pl.BlockSpec((tm, GROUP_W), lambda i: (i, 0))
    xc_spec, xl_spec = _split_specs(tm, d, lambda i: i)
    return pl.pallas_call(
        functools.partial(_outproj_kernel, tm=tm),
        grid=(n // tm,),
        in_specs=[part_spec, part_spec, part_spec,
                  pl.BlockSpec((tm, GROUP_W), lambda i: (i, CB_HG)),
                  pl.BlockSpec((1, HG_DK), lambda i: (0, 0)),
                  part_spec, part_spec,
                  pl.BlockSpec((None, 4 * GROUP_W, d), lambda i: (layer, 0, 0)),
                  xc_spec, xl_spec,
                  pl.BlockSpec((None, 6, d), lambda i: (_mod_index(i * tm), 0, 0)),
                  pl.BlockSpec((1, d), lambda i: (0, 0))],
        out_specs=[pl.BlockSpec((tm, d), lambda i: (i, 0)),
                   pl.BlockSpec((tm, d), lambda i: (i, 0))],
        out_shape=[jax.ShapeDtypeStruct((n, d), F32),
                   jax.ShapeDtypeStruct((n, d), BF16)],
        compiler_params=_cparams(("parallel",)),
        name="out_proj",
    )(y_pool, o_f, o_b, cols, norm_g.reshape(1, HG_DK), y_att, y_s5, w_out, x_ctx, x_lat, mod_l, g2.reshape(1, d))


FFN_HALO = 16


def _ffn_kernel(h_ref, hp_ref, hn_ref, wa_ref, wb_ref, wd_ref, cw_ref, cb_ref, x1_ref, mod_ref, oc_ref, ol_ref,
                ext_ref, aext_ref, acc_ref, *, tm, nf):
    i = pl.program_id(0)
    j = pl.program_id(1)

    @pl.when(j == 0)
    def _():
        ext_ref[0:FFN_HALO, :] = hp_ref[...]
        ext_ref[FFN_HALO:FFN_HALO + tm, :] = h_ref[...]
        ext_ref[FFN_HALO + tm:2 * FFN_HALO + tm, :] = hn_ref[...]
        acc_ref[...] = jnp.zeros_like(acc_ref)

    aext_ref[...] = _dot(ext_ref[...], wa_ref[...])
    b = _dot(h_ref[...], wb_ref[...])
    row0 = i * tm
    tseq = jnp.where(row0 < N_CTX, SEQ, DEC_SEQ)
    pos = (row0 + lax.broadcasted_iota(jnp.int32, (tm, 1), 0)) & (tseq - 1)
    a_prev = jnp.where(pos == 0, 0.0, aext_ref[FFN_HALO - 1:FFN_HALO - 1 + tm, :])
    a_mid = aext_ref[FFN_HALO:FFN_HALO + tm, :]
    a_next = jnp.where(pos == tseq - 1, 0.0, aext_ref[FFN_HALO + 1:FFN_HALO + 1 + tm, :])
    a = a_prev * cw_ref[0:1, :] + a_mid * cw_ref[1:2, :] + a_next * cw_ref[2:3, :] + cb_ref[...]
    act = a * jax.nn.sigmoid(a) * b
    acc_ref[...] += _dot(act.astype(BF16), wd_ref[...])

    @pl.when((j == nf - 1) & (row0 < N_CTX))
    def _():
        oc_ref[...] = x1_ref[...] + mod_ref[5:6, :] * acc_ref[...]

    @pl.when((j == nf - 1) & (row0 >= N_CTX))
    def _():
        ol_ref[...] = x1_ref[...] + mod_ref[5:6, :] * acc_ref[...]


def conv_ffn(h2, x1, mod_l, w_up, conv_w, conv_b, w_down, layer, tm=512, tf=512):
    n, d = x1.shape
    f = w_down.shape[1]
    nf = f // tf
    nhb = n // FFN_HALO
    oc_spec, ol_spec = _split_specs(tm, d, lambda i, j: i)
    return pl.pallas_call(
        functools.partial(_ffn_kernel, tm=tm, nf=nf),
        grid=(n // tm, nf),
        in_specs=[pl.BlockSpec((tm, d), lambda i, j: (i, 0)),
                  pl.BlockSpec((FFN_HALO, d), lambda i, j: (jnp.maximum(i * (tm // FFN_HALO) - 1, 0), 0)),
                  pl.BlockSpec((FFN_HALO, d), lambda i, j: (jnp.minimum((i + 1) * (tm // FFN_HALO), nhb - 1), 0)),
                  pl.BlockSpec((None, d, tf), lambda i, j: (layer, 0, j)),
                  pl.BlockSpec((None, d, tf), lambda i, j: (layer, 0, nf + j)),
                  pl.BlockSpec((None, tf, d), lambda i, j: (layer, j, 0)),
                  pl.BlockSpec((3, tf), lambda i, j: (0, j)),
                  pl.BlockSpec((1, tf), lambda i, j: (0, j)),
                  pl.BlockSpec((tm, d), lambda i, j: (i, 0)),
                  pl.BlockSpec((None, 6, d), lambda i, j: (_mod_index(i * tm), 0, 0))],
        out_specs=[oc_spec, ol_spec],
        out_shape=[jax.ShapeDtypeStruct((N_CTX, d), F32), jax.ShapeDtypeStruct((n - N_CTX, d), F32)],
        scratch_shapes=[pltpu.VMEM((tm + 2 * FFN_HALO, d), BF16),
                        pltpu.VMEM((tm + 2 * FFN_HALO, tf), F32),
                        pltpu.VMEM((tm, d), F32)],
        compiler_params=_cparams(("arbitrary", "arbitrary")),
        name="conv_ffn",
    )(h2, h2, h2, w_up, w_up, w_down, conv_w, conv_b.reshape(1, f), x1, mod_l)


def _rope_tables():
    half = HEAD_DIM // 2
    nf = half // 2
    inv = ROPE_BASE ** (-jnp.arange(nf, dtype=F32) / nf)
    t = jnp.arange(DEC_SEQ)
    rows = (t // GRID_W).astype(F32)
    cols = (t % GRID_W).astype(F32)
    ang = jnp.concatenate([rows[:, None] * inv[None]] * 2 + [cols[:, None] * inv[None]] * 2, axis=1)
    sign = jnp.tile(jnp.concatenate([-jnp.ones(nf, F32), jnp.ones(nf, F32)]), 2)
    cos = jnp.tile(jnp.cos(ang), (DEC_BATCH, ATT_KV_HEADS))
    sin = jnp.tile(jnp.sin(ang) * sign, (DEC_BATCH, ATT_KV_HEADS))
    cos = jnp.concatenate([jnp.ones((N_CTX, 128), F32), cos], axis=0)
    sin = jnp.concatenate([jnp.zeros((N_CTX, 128), F32), sin], axis=0)
    return cos, sin


def kernel(x_prompt, x_sample, cache_k, cache_v, state_hgrn, state_s5_re, state_s5_im, c, c_ctx, norm1_g, norm2_g, ada_w, ada_b, w_in, w_out, pool_w, pool_scale, hg_lb_raw, hg_norm_g, q_norm_g, k_norm_g, att_sink, s5_a_re, s5_a_im, s5_log_dt, s5_b_re, s5_b_im, s5_c_re, s5_c_im, s5_d, s5_w_glu, ffn_w_up, ffn_conv_w, ffn_conv_b, ffn_w_down):
    d = D_MODEL
    x_ctx = x_prompt.reshape(N_CTX, d).astype(F32)
    x_lat = x_sample.reshape(N_LAT, d).astype(F32)

    cond = jnp.concatenate([c_ctx[None].astype(F32), c.astype(F32), jnp.zeros((8 - 1 - DEC_BATCH, d), F32)], axis=0)
    mod = adaln_mod(cond, ada_w, ada_b)

    lb_cum = jnp.cumsum(jax.nn.softmax(hg_lb_raw.astype(F32), axis=0), axis=0)
    hg_lb = lb_cum - lb_cum[:1]
    lp = jnp.stack([jnp.log(hg_lb), jnp.log1p(-hg_lb)], axis=2)

    cos_t, sin_t = _rope_tables()
    s0_ctx = jnp.zeros((BATCH, 2, HG_HEADS, HG_DK, HG_DK), F32)
    s5_ops = s5_operators(s5_a_re, s5_a_im, s5_log_dt, s5_b_re, s5_b_im, s5_c_re, s5_c_im)

    w_in_p = jnp.concatenate([w_in[..., :3584], w_in[..., 3840:4352], w_in[..., 3584:3840]], axis=-1).astype(BF16)
    w_out_b = w_out.astype(BF16)
    w_up_b = ffn_w_up.astype(BF16)
    w_down_b = ffn_w_down.astype(BF16)

    ks_, vs_, hs_, s5_ = [], [], [], []
    for l in range(DEPTH):
        cols = in_proj(x_ctx, x_lat, mod[l], norm1_g[l], w_in_p, l)

        y_pool = pool_mixer(cols, pool_w[l].astype(BF16), pool_scale[l])

        s0_lat = jnp.swapaxes(state_hgrn[:, l].astype(F32), -1, -2)
        o_f, o_b, sfin = hgrn_scan(cols, lp[l], s0_ctx, 0, BATCH, SEQ)
        o_f, o_b, _ = hgrn_scan(cols, lp[l], s0_lat, N_CTX, DEC_BATCH, DEC_SEQ, prev=(o_f, o_b))
        hs_.append(sfin)

        qs, kn = qk_prep(cols, cos_t, sin_t, q_norm_g[l], k_norm_g[l])
        sink = att_sink[l].astype(F32)
        y_att = attend_ctx(qs, kn, cols, sink, BATCH, SEQ)
        y_att = attend_lat(qs, kn, cols, cache_k[:, l].reshape(DEC_BATCH, PAST_LEN, 128).astype(F32),
                           cache_v[:, l].reshape(DEC_BATCH, PAST_LEN, 128).astype(F32), sink, y_att,
                           N_CTX, DEC_BATCH, DEC_SEQ)
        ks_.append(kn[:N_CTX].reshape(BATCH, SEQ, ATT_KV_HEADS, HEAD_DIM))
        vs_.append(cols[:N_CTX, 4224:4352].reshape(BATCH, SEQ, ATT_KV_HEADS, HEAD_DIM))

        h0_lat = jnp.concatenate([state_s5_re[:, l], state_s5_im[:, l]], axis=-1).astype(F32)
        h0_lat = jnp.pad(jnp.transpose(h0_lat, (2, 1, 0, 3)), ((0, 0), (0, 0), (0, 8 - DEC_BATCH), (0, 0)))
        ud = s5_gather(cols)
        ys, hf_c = s5_scan(ud, s5_ops, l, None, None, 0, BATCH, SEQ // S5_L)
        ys, = s5_scan(ud, s5_ops, l, h0_lat, ys, N_CTX, DEC_BATCH, DEC_SEQ // S5_L)
        y_s5 = s5_post(ud, ys, s5_d[l], s5_w_glu[l].astype(BF16))
        hf = hf_c.reshape(S5_GROUPS, 2, BATCH, SEQ // S5_L, 2 * S5_N)
        fin = jnp.stack([hf[:, 0, :, -1], hf[:, 1, :, 0]], axis=1)
        s5_.append(jnp.transpose(fin, (2, 1, 0, 3)))

        x1, h2 = out_proj(y_pool, (o_f, o_b, cols, hg_norm_g[l]), y_att, y_s5, w_out_b, l, x_ctx, x_lat, mod[l],
                          norm2_g[l])
        x_ctx, x_lat = conv_ffn(h2, x1, mod[l], w_up_b, ffn_conv_w[l].astype(F32), ffn_conv_b[l].astype(F32),
                                w_down_b, l)

    y_prompt = x_ctx.reshape(BATCH, SEQ, d).astype(x_prompt.dtype)
    y_sample = x_lat.reshape(DEC_BATCH, DEC_SEQ, d).astype(x_sample.dtype)
    s5_all = jnp.stack(s5_, axis=1)
    return (y_prompt, y_sample, jnp.stack(ks_, axis=1), jnp.stack(vs_, axis=1), jnp.stack(hs_, axis=1),
            s5_all[..., :S5_N], s5_all[..., S5_N:])
```
